```python
import math
import jax, jax.numpy as jnp
from jax import lax
import numpy as np

D_MODEL = 1024
BATCH = 4
SEQ = 4096
DEPTH = 1

N_ATTN_HEADS = 8
ATTN_HEAD_DIM = 64
ATTN_WIDTH = N_ATTN_HEADS * ATTN_HEAD_DIM
MOBA_BLOCK = 256
MOBA_TOPK = 3
MOBA_QCHUNK = 64
ROPE_THETA = 10000.0
N_MLSTM_HEADS = 4
MLSTM_HEAD_DIM = 128
MLSTM_WIDTH = N_MLSTM_HEADS * MLSTM_HEAD_DIM
MLSTM_CHUNK = 64
MLSTM_CONV = 4
D_FF = 2816
FFN_CONV = 3
D_IN = 3 * ATTN_WIDTH + 4 * MLSTM_WIDTH + 2 * N_MLSTM_HEADS + 2 * D_MODEL
EPS = 1e-6
NEG = -1e30

kernel_name = "hybrid_moba_mlstm_convglu_block"


def rms_norm(x, g):
    xf = x.astype(jnp.float32)
    y = xf * lax.rsqrt(jnp.mean(xf * xf, axis=-1, keepdims=True) + EPS)
    return (y * g.astype(jnp.float32)).astype(x.dtype)


def causal_dwconv(x, w):
    width = w.shape[0]
    s = x.shape[1]
    xp = jnp.pad(x, ((0, 0), (width - 1, 0), (0, 0)))
    out = w[0] * xp[:, 0:s]
    for j in range(1, width):
        out = out + w[j] * xp[:, j:j + s]
    return out


def split_heads(t, n_heads, head_dim):
    b, s, _ = t.shape
    return t.reshape(b, s, n_heads, head_dim).transpose(0, 2, 1, 3)


def merge_heads(t):
    b, h, s, d = t.shape
    return t.transpose(0, 2, 1, 3).reshape(b, s, h * d)


def rope(t, positions):
    half = t.shape[-1] // 2
    inv_freq = ROPE_THETA ** (-(jnp.arange(half, dtype=jnp.float32) / half))
    ang = positions.astype(jnp.float32)[:, None, :, None] * inv_freq
    cos, sin = jnp.cos(ang), jnp.sin(ang)
    t1 = t[..., :half].astype(jnp.float32)
    t2 = t[..., half:].astype(jnp.float32)
    return jnp.concatenate([t1 * cos - t2 * sin, t2 * cos + t1 * sin], axis=-1).astype(t.dtype)


def moba_attention(q, k, v):
    b, h, s, d = q.shape
    nb = -(-s // MOBA_BLOCK)
    pad = nb * MOBA_BLOCK - s
    kp = jnp.pad(k, ((0, 0), (0, 0), (0, pad), (0, 0))).reshape(b, h, nb, MOBA_BLOCK, d)
    vp = jnp.pad(v, ((0, 0), (0, 0), (0, pad), (0, 0))).reshape(b, h, nb, MOBA_BLOCK, d)
    k_mean = jnp.mean(kp.astype(jnp.float32), axis=3).astype(k.dtype)
    topk = min(MOBA_TOPK, nb)
    scale = 1.0 / math.sqrt(d)
    n_chunks = s // MOBA_QCHUNK
    bi = jnp.arange(b)[:, None, None, None]
    hi = jnp.arange(h)[None, :, None, None]
    blk_ids = jnp.arange(nb)

    def chunk_fn(c):
        start = c * MOBA_QCHUNK
        qc = lax.dynamic_slice_in_dim(q, start, MOBA_QCHUNK, axis=2)
        j = start // MOBA_BLOCK
        gate = jnp.einsum('bhqd,bhnd->bhqn', qc, k_mean, preferred_element_type=jnp.float32)
        gate = jnp.where(blk_ids < j, gate, NEG)
        _, idx = lax.top_k(gate, topk)
        valid = idx < j
        kg = kp[bi, hi, idx]
        vg = vp[bi, hi, idx]
        s_sel = jnp.einsum('bhqd,bhqtnd->bhqtn', qc, kg, preferred_element_type=jnp.float32) * scale
        s_sel = jnp.where(valid[..., None], s_sel, NEG).reshape(b, h, MOBA_QCHUNK, topk * MOBA_BLOCK)
        k_own = lax.dynamic_index_in_dim(kp, j, axis=2, keepdims=False)
        v_own = lax.dynamic_index_in_dim(vp, j, axis=2, keepdims=False)
        s_own = jnp.einsum('bhqd,bhkd->bhqk', qc, k_own, preferred_element_type=jnp.float32) * scale
        q_pos = start + jnp.arange(MOBA_QCHUNK)
        k_pos = j * MOBA_BLOCK + jnp.arange(MOBA_BLOCK)
        s_own = jnp.where(k_pos[None, :] <= q_pos[:, None], s_own, NEG)
        p = jax.nn.softmax(jnp.concatenate([s_sel, s_own], axis=-1), axis=-1)
        p_sel = p[..., :topk * MOBA_BLOCK].reshape(b, h, MOBA_QCHUNK, topk, MOBA_BLOCK).astype(v.dtype)
        p_own = p[..., topk * MOBA_BLOCK:].astype(v.dtype)
        return (jnp.einsum('bhqtn,bhqtnd->bhqd', p_sel, vg)
                + jnp.einsum('bhqk,bhkd->bhqd', p_own, v_own))

    out = lax.map(chunk_fn, jnp.arange(n_chunks))
    return out.transpose(1, 2, 0, 3, 4).reshape(b, h, s, d)


def mlstm_chunkwise(q, k, v, i_pre, f_pre):
    dtype = v.dtype
    b, h, s, dk = q.shape
    dv = v.shape[-1]
    nc = s // MLSTM_CHUNK
    L = MLSTM_CHUNK
    qf = q.astype(jnp.float32)
    kf = k.astype(jnp.float32) / math.sqrt(dk)
    vf = v.astype(jnp.float32)
    log_f = jax.nn.log_sigmoid(f_pre.astype(jnp.float32))
    log_i = i_pre.astype(jnp.float32)

    def chunks(t):
        t = t.reshape((b, h, nc, L) + t.shape[3:])
        return jnp.moveaxis(t, 2, 0)

    b_cum = jnp.cumsum(log_f.reshape(b, h, nc, L), axis=-1)
    xs = (chunks(qf), chunks(kf), chunks(vf), chunks(log_i), jnp.moveaxis(b_cum, 2, 0))
    tril = jnp.tril(jnp.ones((L, L), dtype=bool))

    def step(carry, inp):
        C, n, m = carry
        qc, kc, vc, ac, bc = inp
        dmat = jnp.where(tril, bc[..., :, None] - bc[..., None, :] + ac[..., None, :], NEG)
        inter = bc + m[..., None]
        m_t = jnp.maximum(inter, jnp.max(dmat, axis=-1))
        w_intra = jnp.exp(dmat - m_t[..., None])
        w_inter = jnp.exp(inter - m_t)
        scores = w_intra * jnp.einsum('bhld,bhsd->bhls', qc, kc)
        num = (w_inter[..., None] * jnp.einsum('bhld,bhdv->bhlv', qc, C)
               + jnp.einsum('bhls,bhsv->bhlv', scores, vc))
        den = w_inter * jnp.einsum('bhld,bhd->bhl', qc, n) + jnp.sum(scores, axis=-1)
        h_out = num / jnp.maximum(jnp.abs(den), jnp.exp(-m_t))[..., None]
        b_last = bc[..., -1]
        dec = b_last[..., None] - bc + ac
        m_new = jnp.maximum(b_last + m, jnp.max(dec, axis=-1))
        wk = jnp.exp(dec - m_new[..., None])
        carry_scale = jnp.exp(b_last + m - m_new)
        C_new = carry_scale[..., None, None] * C + jnp.einsum('bhs,bhsd,bhsv->bhdv', wk, kc, vc)
        n_new = carry_scale[..., None] * n + jnp.einsum('bhs,bhsd->bhd', wk, kc)
        return (C_new, n_new, m_new), h_out

    init = (jnp.zeros((b, h, dk, dv), jnp.float32), jnp.zeros((b, h, dk), jnp.float32),
            jnp.zeros((b, h), jnp.float32))
    _, hs = lax.scan(step, init, xs)
    return jnp.moveaxis(hs, 0, 2).reshape(b, h, s, dv).astype(dtype)


def setup_inputs(seed: int = 0) -> dict:
    key = jax.random.key(seed)
    ks = jax.random.split(key, 16)

    def nrm(k, shape, scale):
        return jax.random.normal(k, shape, jnp.float32) * scale

    x = nrm(ks[0], (BATCH, SEQ, D_MODEL), 1.0)
    positions = jnp.broadcast_to(jnp.arange(SEQ, dtype=jnp.int32), (BATCH, SEQ))
    return {
        "x": x,
        "positions": positions,
        "norm_mix_g": 1.0 + nrm(ks[1], (DEPTH, D_MODEL), 0.02),
        "w_in": nrm(ks[2], (DEPTH, D_MODEL, D_IN), D_MODEL ** -0.5),
        "conv_mlstm": nrm(ks[3], (DEPTH, MLSTM_CONV, 2 * MLSTM_WIDTH), MLSTM_CONV ** -0.5),
        "i_bias": nrm(ks[4], (DEPTH, N_MLSTM_HEADS), 0.1),
        "f_bias": 3.0 + nrm(ks[5], (DEPTH, N_MLSTM_HEADS), 0.5),
        "mlstm_norm_g": 1.0 + nrm(ks[6], (DEPTH, MLSTM_WIDTH), 0.02),
        "w_branch_attn": nrm(ks[7], (DEPTH, ATTN_WIDTH, D_MODEL), ATTN_WIDTH ** -0.5),
        "w_branch_mlstm": nrm(ks[8], (DEPTH, MLSTM_WIDTH, D_MODEL), MLSTM_WIDTH ** -0.5),
        "w_out": nrm(ks[9], (DEPTH, D_MODEL, D_MODEL), D_MODEL ** -0.5),
        "norm_ffn_g": 1.0 + nrm(ks[10], (DEPTH, D_MODEL), 0.02),
        "w_up": nrm(ks[11], (DEPTH, D_MODEL, 2 * D_FF), D_MODEL ** -0.5),
        "conv_ffn": nrm(ks[12], (DEPTH, FFN_CONV, 2 * D_FF), FFN_CONV ** -0.5),
        "w_down": nrm(ks[13], (DEPTH, D_FF, D_MODEL), D_FF ** -0.5),
        "norm_final_g": 1.0 + nrm(ks[14], (D_MODEL,), 0.02),
    }


def reference(x, positions, norm_mix_g, w_in, conv_mlstm, i_bias, f_bias, mlstm_norm_g,
              w_branch_attn, w_branch_mlstm, w_out, norm_ffn_g, w_up, conv_ffn, w_down,
              norm_final_g):
    sizes = [ATTN_WIDTH] * 3 + [MLSTM_WIDTH] * 4 + [N_MLSTM_HEADS] * 2 + [D_MODEL] * 2
    split_points = [int(p) for p in np.cumsum(sizes)[:-1]]
    for layer in range(DEPTH):
        h = rms_norm(x, norm_mix_g[layer])
        proj = h @ w_in[layer]
        (q_a, k_a, v_a, q_m, k_m, v_m, o_m, i_m, f_m, g_a, g_m) = jnp.split(proj, split_points, axis=-1)

        qa = rope(split_heads(q_a, N_ATTN_HEADS, ATTN_HEAD_DIM), positions)
        ka = rope(split_heads(k_a, N_ATTN_HEADS, ATTN_HEAD_DIM), positions)
        va = split_heads(v_a, N_ATTN_HEADS, ATTN_HEAD_DIM)
        y_attn = merge_heads(moba_attention(qa, ka, va)) @ w_branch_attn[layer]

        qk_m = jax.nn.silu(causal_dwconv(jnp.concatenate([q_m, k_m], axis=-1), conv_mlstm[layer]))
        qm, km = jnp.split(qk_m, 2, axis=-1)
        i_pre = (i_m + i_bias[layer]).transpose(0, 2, 1)
        f_pre = (f_m + f_bias[layer]).transpose(0, 2, 1)
        hm = mlstm_chunkwise(split_heads(qm, N_MLSTM_HEADS, MLSTM_HEAD_DIM),
                             split_heads(km, N_MLSTM_HEADS, MLSTM_HEAD_DIM),
                             split_heads(v_m, N_MLSTM_HEADS, MLSTM_HEAD_DIM), i_pre, f_pre)
        hm = rms_norm(hm, mlstm_norm_g[layer].reshape(N_MLSTM_HEADS, 1, MLSTM_HEAD_DIM))
        y_mlstm = (merge_heads(hm) * jax.nn.sigmoid(o_m)) @ w_branch_mlstm[layer]

        merged = jax.nn.sigmoid(g_a) * y_attn + jax.nn.sigmoid(g_m) * y_mlstm
        x = x + merged @ w_out[layer]

        h2 = rms_norm(x, norm_ffn_g[layer])
        u = causal_dwconv(h2 @ w_up[layer], conv_ffn[layer])
        gate, up = jnp.split(u, 2, axis=-1)
        x = x + (jax.nn.silu(gate) * up) @ w_down[layer]
    return rms_norm(x, norm_final_g)
```

```python
import functools
import math

import jax
import jax.numpy as jnp
from jax import lax
from jax.experimental import pallas as pl
from jax.experimental.pallas import tpu as pltpu

F32 = jnp.float32
BF16 = jnp.bfloat16

EPS = 1e-6
NEG = -1e30
ROPE_THETA = 10000.0

N_ATTN_HEADS = 8
ATTN_HEAD_DIM = 64
ATTN_WIDTH = N_ATTN_HEADS * ATTN_HEAD_DIM
MOBA_BLOCK = 256
MOBA_TOPK = 3
N_MLSTM_HEADS = 4
MLSTM_HEAD_DIM = 128
MLSTM_WIDTH = N_MLSTM_HEADS * MLSTM_HEAD_DIM

LANES = 128
HALO = 16
SEQ_TILE = MOBA_BLOCK
MIX_TILE = 512
FFN_TILE = 512
FFN_CHUNK = 256
VMEM_LIMIT = 56 * 1024 * 1024


def _dot(a, b):
    return jnp.dot(a, b, preferred_element_type=F32)


def _rms(x, g):
    ms = jnp.mean(x * x, axis=-1, keepdims=True)
    return x * lax.rsqrt(ms + EPS) * g


def _sigmoid(x):
    return 1.0 / (1.0 + jnp.exp(-x))


def _const_spec(shape):
    nd = len(shape)
    return pl.BlockSpec(shape, lambda *_: (0,) * nd, pipeline_mode=pl.Buffered(1))


def _in_proj_kernel(x_ref, xh_ref, pos_ref, invf_ref, g_ref, w_ref, cw_ref, bias_ref,
                    qT_ref, k_ref, kmean_ref, vT_ref, qm_ref, kmT_ref, vm_ref, om_ref, gif_ref, gifT_ref,
                    ext_ref):
    tm = SEQ_TILE
    aw, mw = ATTN_WIDTH, MLSTM_WIDTH
    j = pl.program_id(1)
    g = g_ref[...]
    h = _rms(x_ref[0], g).astype(BF16)

    ang = pos_ref[0].astype(F32) * invf_ref[...]
    cos = jnp.cos(ang)
    sin = jnp.sin(ang)
    lane = lax.broadcasted_iota(jnp.int32, (tm, LANES), 1)
    first_half = (lane % ATTN_HEAD_DIM) < (ATTN_HEAD_DIM // 2)
    sin_signed = jnp.where(first_half, -sin, sin)

    def rope(t):
        outs = []
        for c in range(aw // LANES):
            tg = t[:, c * LANES:(c + 1) * LANES]
            swapped = jnp.where(first_half, pltpu.roll(tg, LANES - 32, 1), pltpu.roll(tg, 32, 1))
            outs.append(tg * cos + swapped * sin_signed)
        return jnp.concatenate(outs, axis=1)

    qk = _dot(h, w_ref[:, 0:2 * aw])
    q = rope(qk[:, :aw]) * (1.0 / math.sqrt(ATTN_HEAD_DIM))
    k = rope(qk[:, aw:])
    qT_ref[0, 0] = q.T.astype(BF16)
    k_ref[0, 0] = k.astype(BF16)
    kmean_ref[0] = jnp.mean(k, axis=0, keepdims=True)
    v = _dot(h, w_ref[:, 2 * aw:3 * aw])
    vT_ref[0, 0] = v.T.astype(BF16)

    c0 = 3 * aw
    pm = _dot(h, w_ref[:, c0:c0 + 2 * mw])
    hh = _rms(xh_ref[0], g).astype(BF16)
    ph = _dot(hh, w_ref[:, c0:c0 + 2 * mw])
    ext_ref[0:HALO, :] = jnp.where(j == 0, 0.0, ph)
    ext_ref[HALO:HALO + tm, :] = pm
    cw = cw_ref[...]
    conv = cw[0:1] * ext_ref[HALO - 3:HALO - 3 + tm, :]
    conv = conv + cw[1:2] * ext_ref[HALO - 2:HALO - 2 + tm, :]
    conv = conv + cw[2:3] * ext_ref[HALO - 1:HALO - 1 + tm, :]
    conv = conv + cw[3:4] * pm
    act = conv * _sigmoid(conv)
    qm_ref[0] = (act[:, :mw] * (1.0 / math.sqrt(MLSTM_HEAD_DIM))).astype(BF16)
    kmT_ref[0, 0] = act[:, mw:].T.astype(BF16)

    c1 = c0 + 2 * mw
    vo = _dot(h, w_ref[:, c1:c1 + 2 * mw])
    vm_ref[0] = vo[:, :mw].astype(BF16)
    om_ref[0] = vo[:, mw:]

    c2 = c1 + 2 * mw
    gi = _dot(h, w_ref[:, c2:c2 + LANES]) + bias_ref[...]
    log_sig = jnp.minimum(gi, 0.0) - jnp.log1p(jnp.exp(-jnp.abs(gi)))
    gt = jnp.where(lane < N_MLSTM_HEADS, gi, log_sig)
    gif_ref[0] = gt[:, 0:2 * N_MLSTM_HEADS]
    gifT_ref[0] = gt.T[0:2 * N_MLSTM_HEADS, :]


def _in_proj(x, positions, invf, g, w, cw, bias):
    b, s, d = x.shape
    tm = SEQ_TILE
    nb = s // tm
    aw, mw = ATTN_WIDTH, MLSTM_WIDTH
    nh2 = 2 * N_MLSTM_HEADS
    out_shape = [
        jax.ShapeDtypeStruct((b, nb, aw, tm), BF16),
        jax.ShapeDtypeStruct((b, nb, tm, aw), BF16),
        jax.ShapeDtypeStruct((b * nb, 1, aw), F32),
        jax.ShapeDtypeStruct((b, nb, aw, tm), BF16),
        jax.ShapeDtypeStruct((b, s, mw), BF16),
        jax.ShapeDtypeStruct((b, nb, mw, tm), BF16),
        jax.ShapeDtypeStruct((b, s, mw), BF16),
        jax.ShapeDtypeStruct((b, s, mw), F32),
        jax.ShapeDtypeStruct((b, s, nh2), F32),
        jax.ShapeDtypeStruct((b, nh2, s), F32),
    ]
    blk4 = lambda r, c: pl.BlockSpec((1, 1, r, c), lambda bi, ji: (bi, ji, 0, 0))
    row3 = lambda c: pl.BlockSpec((1, tm, c), lambda bi, ji: (bi, ji, 0))
    out_specs = [
        blk4(aw, tm), blk4(tm, aw),
        pl.BlockSpec((1, 1, aw), lambda bi, ji: (bi * nb + ji, 0, 0)),
        blk4(aw, tm), row3(mw), blk4(mw, tm), row3(mw), row3(mw), row3(nh2),
        pl.BlockSpec((1, nh2, tm), lambda bi, ji: (bi, 0, ji)),
    ]
    in_specs = [
        pl.BlockSpec((1, tm, d), lambda bi, ji: (bi, ji, 0)),
        pl.BlockSpec((1, HALO, d), lambda bi, ji: (bi, jnp.maximum(ji * (tm // HALO) - 1, 0), 0)),
        pl.BlockSpec((1, tm, 1), lambda bi, ji: (bi, ji, 0)),
        _const_spec(invf.shape), _const_spec(g.shape), _const_spec(w.shape),
        _const_spec(cw.shape), _const_spec(bias.shape),
    ]
    return pl.pallas_call(
        _in_proj_kernel,
        grid=(b, nb),
        in_specs=in_specs,
        out_specs=out_specs,
        out_shape=out_shape,
        scratch_shapes=[pltpu.VMEM((HALO + tm, 2 * mw), F32)],
        compiler_params=pltpu.CompilerParams(
            dimension_semantics=("parallel", "parallel"), vmem_limit_bytes=VMEM_LIMIT),
        name="in_proj",
    )(x, x, positions.reshape(b, s, 1), invf, g, w, cw, bias)


def _moba_kernel(qT_ref, k_ref, vT_ref, kmean_ref, out_ref, bias_ref):
    blk = MOBA_BLOCK
    hd = ATTN_HEAD_DIM
    nb = k_ref.shape[1]
    j = pl.program_id(2)
    qT = qT_ref[0, 0]
    dim_row = lax.broadcasted_iota(jnp.int32, qT.shape, 0)
    km = kmean_ref[0]
    km_hi = km.astype(BF16)
    km_lo = (km - km_hi.astype(F32)).astype(BF16)
    blk_id = lax.broadcasted_iota(jnp.int32, (nb, blk), 0)
    key_row = lax.broadcasted_iota(jnp.int32, (blk, blk), 0)
    qry_col = lax.broadcasted_iota(jnp.int32, (blk, blk), 1)

    qTh = []
    for h in range(2):
        qh = jnp.where((dim_row >= hd) == (h == 1), qT, jnp.zeros_like(qT))
        qTh.append(qh)
        gate = _dot(km_hi, qh) + _dot(km_lo, qh)
        gate = jnp.where(blk_id < j, gate, NEG)
        rank = jnp.zeros((nb, blk), jnp.int32)
        for m in range(nb):
            gm = gate[m:m + 1, :]
            ahead = (gm > gate) | ((gm == gate) & (m < blk_id))
            rank = rank + ahead.astype(jnp.int32)
        selected = (rank < MOBA_TOPK) & (blk_id < j)
        bias_ref[h] = jnp.where(selected, 0.0, NEG)

    def attend(n, sT_bias, carry):
        kb = k_ref[0, n]
        new = []
        for h in range(2):
            m_old, l_old, acc = carry[h]
            sT = _dot(kb, qTh[h])
            sT = sT_bias(h, sT)
            m_new = jnp.maximum(m_old, jnp.max(sT, axis=0, keepdims=True))
            alpha = jnp.exp(m_old - m_new)
            p = jnp.exp(sT - m_new)
            l_new = alpha * l_old + jnp.sum(p, axis=0, keepdims=True)
            vTb = vT_ref[0, n, h * hd:(h + 1) * hd, :]
            acc = alpha * acc + _dot(vTb, p.astype(BF16))
            new.append((m_new, l_new, acc))
        return tuple(new)

    init = tuple((jnp.full((1, blk), NEG, F32), jnp.zeros((1, blk), F32), jnp.zeros((hd, blk), F32))
                 for _ in range(2))
    carry = attend(j, lambda h, sT: jnp.where(key_row <= qry_col, sT, NEG), init)
    carry = lax.fori_loop(
        0, j, lambda n, c: attend(n, lambda h, sT: sT + bias_ref[h, pl.ds(n, 1), :], c), carry)

    oT = jnp.concatenate([carry[h][2] / carry[h][1] for h in range(2)], axis=0)
    out_ref[0] = oT.T.astype(BF16)


def _moba(qT, k, vT, kmean):
    b, nb, aw, blk = qT.shape
    s = nb * blk
    hp = aw // LANES
    return pl.pallas_call(
        _moba_kernel,
        grid=(b, hp, nb),
        in_specs=[
            pl.BlockSpec((1, 1, LANES, blk), lambda bi, pi, ji: (bi, ji, pi, 0)),
            pl.BlockSpec((1, nb, blk, LANES), lambda bi, pi, ji: (bi, 0, 0, pi)),
            pl.BlockSpec((1, nb, LANES, blk), lambda bi, pi, ji: (bi, 0, pi, 0)),
            pl.BlockSpec((1, nb, LANES), lambda bi, pi, ji: (bi, 0, pi)),
        ],
        out_specs=pl.BlockSpec((1, blk, LANES), lambda bi, pi, ji: (bi, ji, pi)),
        out_shape=jax.ShapeDtypeStruct((b, s, aw), BF16),
        scratch_shapes=[pltpu.VMEM((2, nb, blk), F32)],
        compiler_params=pltpu.CompilerParams(
            dimension_semantics=("parallel", "parallel", "parallel"), vmem_limit_bytes=VMEM_LIMIT),
        name="moba",
    )(qT, k, vT, kmean)


def _split3(x):
    hi = x.astype(BF16)
    r = x - hi.astype(F32)
    mid = r.astype(BF16)
    lo = (r - mid.astype(F32)).astype(BF16)
    return hi, mid, lo


def _mlstm_kernel(q_ref, kT_ref, v_ref, o_ref, gif_ref, gifT_ref, g_ref, out_ref, c_ref, m_ref):
    L = SEQ_TILE
    nh, hd = N_MLSTM_HEADS, MLSTM_HEAD_DIM
    c = pl.program_id(1)

    @pl.when(c == 0)
    def _():
        c_ref[...] = jnp.zeros_like(c_ref)
        m_ref[...] = jnp.zeros_like(m_ref)

    t_idx = lax.broadcasted_iota(jnp.int32, (L, L), 0)
    s_idx = lax.broadcasted_iota(jnp.int32, (L, L), 1)
    causal = s_idx <= t_idx
    tril = jnp.where(causal, 1.0, 0.0).astype(BF16)
    triu = jnp.where(t_idx <= s_idx, 1.0, 0.0).astype(BF16)

    gi_col = gif_ref[0]
    gi_row = gifT_ref[0]
    b_col_all = sum(_dot(tril, part) for part in _split3(gi_col))
    b_row_all = sum(_dot(part, triu) for part in _split3(gi_row))
    ones_col = jnp.where(lax.broadcasted_iota(jnp.int32, (L, hd), 1) == 0, 1.0, 0.0).astype(BF16)
    gn = g_ref[...]

    for h in range(nh):
        a_row = gi_row[h:h + 1, :]
        b_row = b_row_all[nh + h:nh + h + 1, :]
        b_col = b_col_all[:, nh + h:nh + h + 1]
        m_prev = m_ref[h]
        c_old = c_ref[h]

        dmat = jnp.where(causal, b_col - b_row + a_row, NEG)
        inter = b_col + m_prev
        m_t = jnp.maximum(inter, jnp.max(dmat, axis=1, keepdims=True))
        w_intra = jnp.exp(dmat - m_t)
        w_inter = jnp.exp(inter - m_t)

        q = q_ref[0, :, h * hd:(h + 1) * hd]
        kT = kT_ref[0, 0, h * hd:(h + 1) * hd, :]
        v_ext = jnp.concatenate([v_ref[0, :, h * hd:(h + 1) * hd], ones_col], axis=1)
        scores = w_intra * _dot(q, kT)
        num_ext = w_inter * _dot(q, c_old.astype(BF16)) + _dot(scores.astype(BF16), v_ext)
        num = num_ext[:, :hd]
        den = num_ext[:, hd:hd + 1]
        h_out = num / jnp.maximum(jnp.abs(den), jnp.exp(-m_t))

        b_last = b_row[:, L - 1:L]
        dec = b_last - b_row + a_row
        m_new = jnp.maximum(b_last + m_prev, jnp.max(dec, axis=1, keepdims=True))
        wk = jnp.exp(dec - m_new)
        carry_scale = jnp.exp(b_last + m_prev - m_new)
        kw = (kT.astype(F32) * wk).astype(BF16)
        c_ref[h] = carry_scale * c_old + _dot(kw, v_ext)
        m_ref[h] = m_new

        hn = _rms(h_out, gn[:, h * hd:(h + 1) * hd])
        gate_o = _sigmoid(o_ref[0, :, h * hd:(h + 1) * hd])
        out_ref[0, :, h * hd:(h + 1) * hd] = (hn * gate_o).astype(BF16)


def _mlstm(qm, kmT, vm, om, gif, gifT, g):
    b, s, mw = qm.shape
    L = SEQ_TILE
    nc = s // L
    nh2 = 2 * N_MLSTM_HEADS
    row = lambda cdim: pl.BlockSpec((1, L, cdim), lambda bi, ci: (bi, ci, 0))
    return pl.pallas_call(
        _mlstm_kernel,
        grid=(b, nc),
        in_specs=[
            row(mw),
            pl.BlockSpec((1, 1, mw, L), lambda bi, ci: (bi, ci, 0, 0)),
            row(mw), row(mw), row(nh2),
            pl.BlockSpec((1, nh2, L), lambda bi, ci: (bi, 0, ci)),
            _const_spec(g.shape),
        ],
        out_specs=row(mw),
        out_shape=jax.ShapeDtypeStruct((b, s, mw), BF16),
        scratch_shapes=[pltpu.VMEM((N_MLSTM_HEADS, MLSTM_HEAD_DIM, 2 * MLSTM_HEAD_DIM), F32),
                        pltpu.VMEM((N_MLSTM_HEADS, 1, 1), F32)],
        compiler_params=pltpu.CompilerParams(
            dimension_semantics=("parallel", "arbitrary"), vmem_limit_bytes=VMEM_LIMIT),
        name="mlstm",
    )(qm, kmT, vm, om, gif, gifT, g)


def _mix_kernel(x_ref, a_ref, y_ref, g_ref, wg_ref, wa_ref, wm_ref, wo_ref, out_ref):
    d = x_ref.shape[-1]
    x = x_ref[...]
    h = _rms(x, g_ref[...]).astype(BF16)
    gate_a = _sigmoid(_dot(h, wg_ref[:, :d]))
    gate_m = _sigmoid(_dot(h, wg_ref[:, d:]))
    merged = gate_a * _dot(a_ref[...], wa_ref[...]) + gate_m * _dot(y_ref[...], wm_ref[...])
    out_ref[...] = x + _dot(merged.astype(BF16), wo_ref[...])


def _mix(x2d, a2d, y2d, g, wg, wa, wm, wo):
    t, d = x2d.shape
    tm = MIX_TILE
    row = lambda cdim: pl.BlockSpec((tm, cdim), lambda i: (i, 0))
    return pl.pallas_call(
        _mix_kernel,
        grid=(t // tm,),
        in_specs=[row(d), row(a2d.shape[1]), row(y2d.shape[1]), _const_spec(g.shape),
                  _const_spec(wg.shape), _const_spec(wa.shape), _const_spec(wm.shape), _const_spec(wo.shape)],
        out_specs=row(d),
        out_shape=jax.ShapeDtypeStruct((t, d), F32),
        compiler_params=pltpu.CompilerParams(
            dimension_semantics=("parallel",), vmem_limit_bytes=VMEM_LIMIT),
        name="mix",
    )(x2d, a2d, y2d, g, wg, wa, wm, wo)


def _ffn_kernel(x_ref, xh_ref, g_ref, wg_ref, wu_ref, cg_ref, cu_ref, wd_ref, gf_ref, out_ref,
                hext_ref, acc_ref):
    tm = FFN_TILE
    j = pl.program_id(1)
    x = x_ref[0]
    g = g_ref[...]
    hext_ref[HALO:HALO + tm, :] = _rms(x, g).astype(BF16)
    hh = _rms(xh_ref[0], g)
    hext_ref[0:HALO, :] = jnp.where(j == 0, 0.0, hh).astype(BF16)
    acc_ref[...] = jnp.zeros_like(acc_ref)

    def conv(u, cw):
        out = cw[0:1] * u[HALO - 2:HALO - 2 + tm]
        out = out + cw[1:2] * u[HALO - 1:HALO - 1 + tm]
        return out + cw[2:3] * u[HALO:HALO + tm]

    def body(c, carry):
        hx = hext_ref[...]
        gate = conv(_dot(hx, wg_ref[c]), cg_ref[c])
        up = conv(_dot(hx, wu_ref[c]), cu_ref[c])
        act = (gate * _sigmoid(gate) * up).astype(BF16)
        acc_ref[...] += _dot(act, wd_ref[c])
        return carry

    lax.fori_loop(0, wg_ref.shape[0], body, 0)
    out_ref[0] = _rms(x + acc_ref[...], gf_ref[...])


def _ffn(x1, g2, wg, wu, cg, cu, wd, gf):
    b, s, d = x1.shape
    tm = FFN_TILE
    return pl.pallas_call(
        _ffn_kernel,
        grid=(b, s // tm),
        in_specs=[
            pl.BlockSpec((1, tm, d), lambda bi, ji: (bi, ji, 0)),
            pl.BlockSpec((1, HALO, d), lambda bi, ji: (bi, jnp.maximum(ji * (tm // HALO) - 1, 0), 0)),
            _const_spec(g2.shape), _const_spec(wg.shape), _const_spec(wu.shape),
            _const_spec(cg.shape), _const_spec(cu.shape), _const_spec(wd.shape), _const_spec(gf.shape),
        ],
        out_specs=pl.BlockSpec((1, tm, d), lambda bi, ji: (bi, ji, 0)),
        out_shape=jax.ShapeDtypeStruct((b, s, d), F32),
        scratch_shapes=[pltpu.VMEM((HALO + tm, d), BF16), pltpu.VMEM((tm, d), F32)],
        compiler_params=pltpu.CompilerParams(
            dimension_semantics=("parallel", "parallel"), vmem_limit_bytes=VMEM_LIMIT),
        name="ffn",
    )(x1, x1, g2, wg, wu, cg, cu, wd, gf)


def _chunk_cols(w, chunk):
    kdim, n = w.shape
    return w.reshape(kdim, n // chunk, chunk).transpose(1, 0, 2)


def kernel(x, positions, norm_mix_g, w_in, conv_mlstm, i_bias, f_bias, mlstm_norm_g, w_branch_attn,
           w_branch_mlstm, w_out, norm_ffn_g, w_up, conv_ffn, w_down, norm_final_g):
    b, s, d = x.shape
    aw, mw, nh = ATTN_WIDTH, MLSTM_WIDTH, N_MLSTM_HEADS
    assert s % FFN_TILE == 0 and s % SEQ_TILE == 0 and d % LANES == 0
    n_qkv = 3 * aw + 4 * mw
    d_ff = w_down.shape[1]
    assert d_ff % FFN_CHUNK == 0
    half = ATTN_HEAD_DIM // 2
    inv_freq = ROPE_THETA ** (-(jnp.arange(half, dtype=F32) / half))
    invf = jnp.tile(inv_freq, LANES // half).reshape(1, LANES)

    for layer in range(w_in.shape[0]):
        wl = w_in[layer]
        w1 = jnp.concatenate(
            [wl[:, :n_qkv + 2 * nh], jnp.zeros((d, LANES - 2 * nh), wl.dtype)], axis=1).astype(BF16)
        wg = wl[:, n_qkv + 2 * nh:].astype(BF16)
        bias = jnp.concatenate(
            [i_bias[layer], f_bias[layer], jnp.zeros((LANES - 2 * nh,), F32)]).reshape(1, LANES)

        qT, k, kmean, vT, qm, kmT, vm, om, gif, gifT = _in_proj(
            x, positions, invf, norm_mix_g[layer].reshape(1, d), w1, conv_mlstm[layer], bias)
        attn = _moba(qT, k, vT, kmean.reshape(b, s // MOBA_BLOCK, aw))
        y_m = _mlstm(qm, kmT, vm, om, gif, gifT, mlstm_norm_g[layer].reshape(1, mw))
        x1 = _mix(x.reshape(b * s, d), attn.reshape(b * s, aw), y_m.reshape(b * s, mw),
                  norm_mix_g[layer].reshape(1, d), wg, w_branch_attn[layer].astype(BF16),
                  w_branch_mlstm[layer].astype(BF16), w_out[layer].astype(BF16)).reshape(b, s, d)

        wu = w_up[layer].astype(BF16)
        cf = conv_ffn[layer]
        last = layer == w_in.shape[0] - 1
        gfin = norm_final_g.reshape(1, d) if last else None
        assert last, "only the final layer fuses the output norm"
        x = _ffn(x1, norm_ffn_g[layer].reshape(1, d),
                 _chunk_cols(wu[:, :d_ff], FFN_CHUNK), _chunk_cols(wu[:, d_ff:], FFN_CHUNK),
                 _chunk_cols(cf[:, :d_ff], FFN_CHUNK), _chunk_cols(cf[:, d_ff:], FFN_CHUNK),
                 w_down[layer].astype(BF16).reshape(d_ff // FFN_CHUNK, FFN_CHUNK, d), gfin)
    return x
```

```python
import functools
import math

import jax
import jax.numpy as jnp
from jax import lax
from jax.experimental import pallas as pl
from jax.experimental.pallas import tpu as pltpu

F32 = jnp.float32
BF16 = jnp.bfloat16

EPS = 1e-6
NEG = -1e30
ROPE_THETA = 10000.0

N_ATTN_HEADS = 8
ATTN_HEAD_DIM = 64
ATTN_WIDTH = N_ATTN_HEADS * ATTN_HEAD_DIM
MOBA_BLOCK = 256
MOBA_TOPK = 3
N_MLSTM_HEADS = 4
MLSTM_HEAD_DIM = 128
MLSTM_WIDTH = N_MLSTM_HEADS * MLSTM_HEAD_DIM

LANES = 128
HALO = 16
SEQ_TILE = MOBA_BLOCK
MIX_TILE = 512
FFN_TILE = 512
FFN_CHUNK = 256
VMEM_LIMIT = 56 * 1024 * 1024


def _dot(a, b):
    return jnp.dot(a, b, preferred_element_type=F32)


def _rms(x, g):
    ms = jnp.mean(x * x, axis=-1, keepdims=True)
    return x * lax.rsqrt(ms + EPS) * g


def _sigmoid(x):
    return 1.0 / (1.0 + jnp.exp(-x))


def _const_spec(shape):
    nd = len(shape)
    return pl.BlockSpec(shape, lambda *_: (0,) * nd, pipeline_mode=pl.Buffered(1))


def _in_proj_kernel(x_ref, xh_ref, pos_ref, invf_ref, g_ref, w_ref, cw_ref, bias_ref,
                    qT_ref, k_ref, kmean_ref, vT_ref, qm_ref, kmT_ref, vm_ref, om_ref, gif_ref, gifT_ref,
                    ext_ref):
    tm = SEQ_TILE
    aw, mw = ATTN_WIDTH, MLSTM_WIDTH
    j = pl.program_id(1)
    g = g_ref[...]
    h = _rms(x_ref[0], g).astype(BF16)

    ang = pos_ref[0].astype(F32) * invf_ref[...]
    cos = jnp.cos(ang)
    sin = jnp.sin(ang)
    lane = lax.broadcasted_iota(jnp.int32, (tm, LANES), 1)
    first_half = (lane % ATTN_HEAD_DIM) < (ATTN_HEAD_DIM // 2)
    sin_signed = jnp.where(first_half, -sin, sin)

    def rope(t):
        outs = []
        for c in range(aw // LANES):
            tg = t[:, c * LANES:(c + 1) * LANES]
            swapped = jnp.where(first_half, pltpu.roll(tg, LANES - 32, 1), pltpu.roll(tg, 32, 1))
            outs.append(tg * cos + swapped * sin_signed)
        return jnp.concatenate(outs, axis=1)

    qk = _dot(h, w_ref[:, 0:2 * aw])
    q = rope(qk[:, :aw]) * (1.0 / math.sqrt(ATTN_HEAD_DIM))
    k = rope(qk[:, aw:])
    qT_ref[0, 0] = q.T.astype(BF16)
    k_ref[0, 0] = k.astype(BF16)
    kmean_ref[0] = jnp.mean(k, axis=0, keepdims=True)
    v = _dot(h, w_ref[:, 2 * aw:3 * aw])
    vT_ref[0, 0] = v.T.astype(BF16)

    c0 = 3 * aw
    pm = _dot(h, w_ref[:, c0:c0 + 2 * mw])
    hh = _rms(xh_ref[0], g).astype(BF16)
    ph = _dot(hh, w_ref[:, c0:c0 + 2 * mw])
    ext_ref[0:HALO, :] = jnp.where(j == 0, 0.0, ph)
    ext_ref[HALO:HALO + tm, :] = pm
    cw = cw_ref[...]
    conv = cw[0:1] * ext_ref[HALO - 3:HALO - 3 + tm, :]
    conv = conv + cw[1:2] * ext_ref[HALO - 2:HALO - 2 + tm, :]
    conv = conv + cw[2:3] * ext_ref[HALO - 1:HALO - 1 + tm, :]
    conv = conv + cw[3:4] * pm
    act = conv * _sigmoid(conv)
    qm_ref[0] = (act[:, :mw] * (1.0 / math.sqrt(MLSTM_HEAD_DIM))).astype(BF16)
    kmT_ref[0, 0] = act[:, mw:].T.astype(BF16)

    c1 = c0 + 2 * mw
    vo = _dot(h, w_ref[:, c1:c1 + 2 * mw])
    vm_ref[0] = vo[:, :mw].astype(BF16)
    om_ref[0] = vo[:, mw:]

    c2 = c1 + 2 * mw
    gi = _dot(h, w_ref[:, c2:c2 + LANES]) + bias_ref[...]
    log_sig = jnp.minimum(gi, 0.0) - jnp.log1p(jnp.exp(-jnp.abs(gi)))
    gt = jnp.where(lane < N_MLSTM_HEADS, gi, log_sig)
    gif_ref[0] = gt[:, 0:2 * N_MLSTM_HEADS]
    gifT_ref[0] = gt.T[0:2 * N_MLSTM_HEADS, :]


def _in_proj(x, positions, invf, g, w, cw, bias):
    b, s, d = x.shape
    tm = SEQ_TILE
    nb = s // tm
    aw, mw = ATTN_WIDTH, MLSTM_WIDTH
    nh2 = 2 * N_MLSTM_HEADS
    out_shape = [
        jax.ShapeDtypeStruct((b, nb, aw, tm), BF16),
        jax.ShapeDtypeStruct((b, nb, tm, aw), BF16),
        jax.ShapeDtypeStruct((b * nb, 1, aw), F32),
        jax.ShapeDtypeStruct((b, nb, aw, tm), BF16),
        jax.ShapeDtypeStruct((b, s, mw), BF16),
        jax.ShapeDtypeStruct((b, nb, mw, tm), BF16),
        jax.ShapeDtypeStruct((b, s, mw), BF16),
        jax.ShapeDtypeStruct((b, s, mw), F32),
        jax.ShapeDtypeStruct((b, s, nh2), F32),
        jax.ShapeDtypeStruct((b, nh2, s), F32),
    ]
    blk4 = lambda r, c: pl.BlockSpec((1, 1, r, c), lambda bi, ji: (bi, ji, 0, 0))
    row3 = lambda c: pl.BlockSpec((1, tm, c), lambda bi, ji: (bi, ji, 0))
    out_specs = [
        blk4(aw, tm), blk4(tm, aw),
        pl.BlockSpec((1, 1, aw), lambda bi, ji: (bi * nb + ji, 0, 0)),
        blk4(aw, tm), row3(mw), blk4(mw, tm), row3(mw), row3(mw), row3(nh2),
        pl.BlockSpec((1, nh2, tm), lambda bi, ji: (bi, 0, ji)),
    ]
    in_specs = [
        pl.BlockSpec((1, tm, d), lambda bi, ji: (bi, ji, 0)),
        pl.BlockSpec((1, HALO, d), lambda bi, ji: (bi, jnp.maximum(ji * (tm // HALO) - 1, 0), 0)),
        pl.BlockSpec((1, tm, 1), lambda bi, ji: (bi, ji, 0)),
        _const_spec(invf.shape), _const_spec(g.shape), _const_spec(w.shape),
        _const_spec(cw.shape), _const_spec(bias.shape),
    ]
    return pl.pallas_call(
        _in_proj_kernel,
        grid=(b, nb),
        in_specs=in_specs,
        out_specs=out_specs,
        out_shape=out_shape,
        scratch_shapes=[pltpu.VMEM((HALO + tm, 2 * mw), F32)],
        compiler_params=pltpu.CompilerParams(
            dimension_semantics=("parallel", "parallel"), vmem_limit_bytes=VMEM_LIMIT),
        name="in_proj",
    )(x, x, positions.reshape(b, s, 1), invf, g, w, cw, bias)


def _moba_kernel(qT_ref, k_ref, vT_ref, kmean_ref, out_ref, bias_ref, s_ref, p_ref, a_ref, acc_ref):
    blk = MOBA_BLOCK
    hd = ATTN_HEAD_DIM
    nb = k_ref.shape[1]
    j = pl.program_id(2)
    qT = qT_ref[0, 0]
    dim_row = lax.broadcasted_iota(jnp.int32, qT.shape, 0)
    km = kmean_ref[0]
    km_hi = km.astype(BF16)
    km_lo = (km - km_hi.astype(F32)).astype(BF16)
    blk_id = lax.broadcasted_iota(jnp.int32, (nb, blk), 0)
    key_row = lax.broadcasted_iota(jnp.int32, (blk, blk), 0)
    qry_col = lax.broadcasted_iota(jnp.int32, (blk, blk), 1)
    k_own = k_ref[0, j]

    qTh = []
    for h in range(2):
        qh = jnp.where((dim_row >= hd) == (h == 1), qT, jnp.zeros_like(qT))
        qTh.append(qh)
        gate = _dot(km_hi, qh) + _dot(km_lo, qh)
        gate = jnp.where(blk_id < j, gate, NEG)
        rank = jnp.zeros((nb, blk), jnp.int32)
        for m in range(nb):
            gm = gate[m:m + 1, :]
            ahead = (gm > gate) | ((gm == gate) & (m < blk_id))
            rank = rank + ahead.astype(jnp.int32)
        selected = (rank < MOBA_TOPK) & (blk_id < j)
        bias_ref[h, 0:1, :] = jnp.zeros((1, blk), F32)
        bias_ref[h, 1:nb + 1, :] = jnp.where(selected, 0.0, NEG)
        s_ref[1, h] = jnp.where(key_row <= qry_col, _dot(k_own, qh), NEG)
        p_ref[0, h] = jnp.zeros((blk, blk), BF16)
        a_ref[0, h] = jnp.zeros((1, blk), F32)
        acc_ref[h] = jnp.zeros((hd, blk), F32)

    def body(i, carry):
        slot = i & 1
        other = 1 - slot
        vb = jnp.where(i == 1, j, jnp.clip(i - 2, 0, nb - 1))
        for h in range(2):
            vTb = vT_ref[0, vb, h * hd:(h + 1) * hd, :]
            acc_ref[h] = a_ref[slot, h] * acc_ref[h] + _dot(vTb, p_ref[slot, h])
        brow = jnp.minimum(i, j + 1)
        new = []
        for h in range(2):
            m_old, l_old = carry[h]
            sT = s_ref[other, h]
            bias = bias_ref[h, pl.ds(brow, 1), :]
            m_new = jnp.maximum(m_old, jnp.max(sT, axis=0, keepdims=True) + bias)
            alpha = jnp.exp(m_old - m_new)
            p = jnp.exp(sT - (m_new - bias))
            new.append((m_new, alpha * l_old + jnp.sum(p, axis=0, keepdims=True)))
            p_ref[other, h] = p.astype(BF16)
            a_ref[other, h] = alpha
        kb = k_ref[0, jnp.minimum(i, nb - 1)]
        for h in range(2):
            s_ref[slot, h] = _dot(kb, qTh[h])
        return tuple(new)

    init = tuple((jnp.full((1, blk), NEG, F32), jnp.zeros((1, blk), F32)) for _ in range(2))
    carry = lax.fori_loop(0, j + 2, body, init)
    oT = jnp.concatenate([acc_ref[h] / carry[h][1] for h in range(2)], axis=0)
    out_ref[0] = oT.T.astype(BF16)


def _moba(qT, k, vT, kmean):
    b, nb, aw, blk = qT.shape
    s = nb * blk
    hp = aw // LANES
    return pl.pallas_call(
        _moba_kernel,
        grid=(b, hp, nb),
        in_specs=[
            pl.BlockSpec((1, 1, LANES, blk), lambda bi, pi, ji: (bi, ji, pi, 0)),
            pl.BlockSpec((1, nb, blk, LANES), lambda bi, pi, ji: (bi, 0, 0, pi)),
            pl.BlockSpec((1, nb, LANES, blk), lambda bi, pi, ji: (bi, 0, pi, 0)),
            pl.BlockSpec((1, nb, LANES), lambda bi, pi, ji: (bi, 0, pi)),
        ],
        out_specs=pl.BlockSpec((1, blk, LANES), lambda bi, pi, ji: (bi, ji, pi)),
        out_shape=jax.ShapeDtypeStruct((b, s, aw), BF16),
        scratch_shapes=[pltpu.VMEM((2, nb + 8, blk), F32),
                        pltpu.VMEM((2, 2, blk, blk), F32),
                        pltpu.VMEM((2, 2, blk, blk), BF16),
                        pltpu.VMEM((2, 2, 1, blk), F32),
                        pltpu.VMEM((2, ATTN_HEAD_DIM, blk), F32)],
        compiler_params=pltpu.CompilerParams(
            dimension_semantics=("parallel", "parallel", "parallel"), vmem_limit_bytes=VMEM_LIMIT),
        name="moba",
    )(qT, k, vT, kmean)


def _split3(x):
    hi = x.astype(BF16)
    r = x - hi.astype(F32)
    mid = r.astype(BF16)
    lo = (r - mid.astype(F32)).astype(BF16)
    return hi, mid, lo


def _mlstm_kernel(q_ref, kT_ref, v_ref, o_ref, gif_ref, gifT_ref, g_ref, out_ref, c_ref, m_ref):
    L = SEQ_TILE
    nh, hd = N_MLSTM_HEADS, MLSTM_HEAD_DIM
    c = pl.program_id(1)

    @pl.when(c == 0)
    def _():
        c_ref[...] = jnp.zeros_like(c_ref)
        m_ref[...] = jnp.zeros_like(m_ref)

    t_idx = lax.broadcasted_iota(jnp.int32, (L, L), 0)
    s_idx = lax.broadcasted_iota(jnp.int32, (L, L), 1)
    causal = s_idx <= t_idx
    tril = jnp.where(causal, 1.0, 0.0).astype(BF16)
    triu = jnp.where(t_idx <= s_idx, 1.0, 0.0).astype(BF16)

    gi_col = gif_ref[0]
    gi_row = gifT_ref[0]
    b_col_all = sum(_dot(tril, part) for part in _split3(gi_col))
    b_row_all = sum(_dot(part, triu) for part in _split3(gi_row))
    ones_col = jnp.where(lax.broadcasted_iota(jnp.int32, (L, hd), 1) == 0, 1.0, 0.0).astype(BF16)
    gn = g_ref[...]

    for h in range(nh):
        a_row = gi_row[h:h + 1, :]
        b_row = b_row_all[nh + h:nh + h + 1, :]
        b_col = b_col_all[:, nh + h:nh + h + 1]
        m_prev = m_ref[h]
        c_old = c_ref[h]

        dmat = jnp.where(causal, b_col - b_row + a_row, NEG)
        inter = b_col + m_prev
        m_t = jnp.maximum(inter, jnp.max(dmat, axis=1, keepdims=True))
        w_intra = jnp.exp(dmat - m_t)
        w_inter = jnp.exp(inter - m_t)

        q = q_ref[0, :, h * hd:(h + 1) * hd]
        kT = kT_ref[0, 0, h * hd:(h + 1) * hd, :]
        v_ext = jnp.concatenate([v_ref[0, :, h * hd:(h + 1) * hd], ones_col], axis=1)
        scores = w_intra * _dot(q, kT)
        num_ext = w_inter * _dot(q, c_old.astype(BF16)) + _dot(scores.astype(BF16), v_ext)
        num = num_ext[:, :hd]
        den = num_ext[:, hd:hd + 1]
        h_out = num / jnp.maximum(jnp.abs(den), jnp.exp(-m_t))

        b_last = b_row[:, L - 1:L]
        dec = b_last - b_row + a_row
        m_new = jnp.maximum(b_last + m_prev, jnp.max(dec, axis=1, keepdims=True))
        wk = jnp.exp(dec - m_new)
        carry_scale = jnp.exp(b_last + m_prev - m_new)
        kw = (kT.astype(F32) * wk).astype(BF16)
        c_ref[h] = carry_scale * c_old + _dot(kw, v_ext)
        m_ref[h] = m_new

        hn = _rms(h_out, gn[:, h * hd:(h + 1) * hd])
        gate_o = _sigmoid(o_ref[0, :, h * hd:(h + 1) * hd])
        out_ref[0, :, h * hd:(h + 1) * hd] = (hn * gate_o).astype(BF16)


def _mlstm(qm, kmT, vm, om, gif, gifT, g):
    b, s, mw = qm.shape
    L = SEQ_TILE
    nc = s // L
    nh2 = 2 * N_MLSTM_HEADS
    row = lambda cdim: pl.BlockSpec((1, L, cdim), lambda bi, ci: (bi, ci, 0))
    return pl.pallas_call(
        _mlstm_kernel,
        grid=(b, nc),
        in_specs=[
            row(mw),
            pl.BlockSpec((1, 1, mw, L), lambda bi, ci: (bi, ci, 0, 0)),
            row(mw), row(mw), row(nh2),
            pl.BlockSpec((1, nh2, L), lambda bi, ci: (bi, 0, ci)),
            _const_spec(g.shape),
        ],
        out_specs=row(mw),
        out_shape=jax.ShapeDtypeStruct((b, s, mw), BF16),
        scratch_shapes=[pltpu.VMEM((N_MLSTM_HEADS, MLSTM_HEAD_DIM, 2 * MLSTM_HEAD_DIM), F32),
                        pltpu.VMEM((N_MLSTM_HEADS, 1, 1), F32)],
        compiler_params=pltpu.CompilerParams(
            dimension_semantics=("parallel", "arbitrary"), vmem_limit_bytes=VMEM_LIMIT),
        name="mlstm",
    )(qm, kmT, vm, om, gif, gifT, g)


def _mix_kernel(x_ref, a_ref, y_ref, g_ref, wg_ref, wa_ref, wm_ref, wo_ref, out_ref):
    d = x_ref.shape[-1]
    x = x_ref[...]
    h = _rms(x, g_ref[...]).astype(BF16)
    gate_a = _sigmoid(_dot(h, wg_ref[:, :d]))
    gate_m = _sigmoid(_dot(h, wg_ref[:, d:]))
    merged = gate_a * _dot(a_ref[...], wa_ref[...]) + gate_m * _dot(y_ref[...], wm_ref[...])
    out_ref[...] = x + _dot(merged.astype(BF16), wo_ref[...])


def _mix(x2d, a2d, y2d, g, wg, wa, wm, wo):
    t, d = x2d.shape
    tm = MIX_TILE
    row = lambda cdim: pl.BlockSpec((tm, cdim), lambda i: (i, 0))
    return pl.pallas_call(
        _mix_kernel,
        grid=(t // tm,),
        in_specs=[row(d), row(a2d.shape[1]), row(y2d.shape[1]), _const_spec(g.shape),
                  _const_spec(wg.shape), _const_spec(wa.shape), _const_spec(wm.shape), _const_spec(wo.shape)],
        out_specs=row(d),
        out_shape=jax.ShapeDtypeStruct((t, d), F32),
        compiler_params=pltpu.CompilerParams(
            dimension_semantics=("parallel",), vmem_limit_bytes=VMEM_LIMIT),
        name="mix",
    )(x2d, a2d, y2d, g, wg, wa, wm, wo)


def _ffn_kernel(x_ref, xh_ref, g_ref, wg_ref, wu_ref, cg_ref, cu_ref, wd_ref, gf_ref, out_ref,
                hext_ref, acc_ref):
    tm = FFN_TILE
    j = pl.program_id(1)
    x = x_ref[0]
    g = g_ref[...]
    hext_ref[HALO:HALO + tm, :] = _rms(x, g).astype(BF16)
    hh = _rms(xh_ref[0], g)
    hext_ref[0:HALO, :] = jnp.where(j == 0, 0.0, hh).astype(BF16)
    acc_ref[...] = jnp.zeros_like(acc_ref)

    def conv(u, cw):
        out = cw[0:1] * u[HALO - 2:HALO - 2 + tm]
        out = out + cw[1:2] * u[HALO - 1:HALO - 1 + tm]
        return out + cw[2:3] * u[HALO:HALO + tm]

    def body(c, carry):
        hx = hext_ref[...]
        gate = conv(_dot(hx, wg_ref[c]), cg_ref[c])
        up = conv(_dot(hx, wu_ref[c]), cu_ref[c])
        act = (gate * _sigmoid(gate) * up).astype(BF16)
        acc_ref[...] += _dot(act, wd_ref[c])
        return carry

    lax.fori_loop(0, wg_ref.shape[0], body, 0)
    out_ref[0] = _rms(x + acc_ref[...], gf_ref[...])


def _ffn(x1, g2, wg, wu, cg, cu, wd, gf):
    b, s, d = x1.shape
    tm = FFN_TILE
    return pl.pallas_call(
        _ffn_kernel,
        grid=(b, s // tm),
        in_specs=[
            pl.BlockSpec((1, tm, d), lambda bi, ji: (bi, ji, 0)),
            pl.BlockSpec((1, HALO, d), lambda bi, ji: (bi, jnp.maximum(ji * (tm // HALO) - 1, 0), 0)),
            _const_spec(g2.shape), _const_spec(wg.shape), _const_spec(wu.shape),
            _const_spec(cg.shape), _const_spec(cu.shape), _const_spec(wd.shape), _const_spec(gf.shape),
        ],
        out_specs=pl.BlockSpec((1, tm, d), lambda bi, ji: (bi, ji, 0)),
        out_shape=jax.ShapeDtypeStruct((b, s, d), F32),
        scratch_shapes=[pltpu.VMEM((HALO + tm, d), BF16), pltpu.VMEM((tm, d), F32)],
        compiler_params=pltpu.CompilerParams(
            dimension_semantics=("parallel", "parallel"), vmem_limit_bytes=VMEM_LIMIT),
        name="ffn",
    )(x1, x1, g2, wg, wu, cg, cu, wd, gf)


def _chunk_cols(w, chunk):
    kdim, n = w.shape
    return w.reshape(kdim, n // chunk, chunk).transpose(1, 0, 2)


def kernel(x, positions, norm_mix_g, w_in, conv_mlstm, i_bias, f_bias, mlstm_norm_g, w_branch_attn,
           w_branch_mlstm, w_out, norm_ffn_g, w_up, conv_ffn, w_down, norm_final_g):
    b, s, d = x.shape
    aw, mw, nh = ATTN_WIDTH, MLSTM_WIDTH, N_MLSTM_HEADS
    assert s % FFN_TILE == 0 and s % SEQ_TILE == 0 and d % LANES == 0
    n_qkv = 3 * aw + 4 * mw
    d_ff = w_down.shape[1]
    assert d_ff % FFN_CHUNK == 0
    half = ATTN_HEAD_DIM // 2
    inv_freq = ROPE_THETA ** (-(jnp.arange(half, dtype=F32) / half))
    invf = jnp.tile(inv_freq, LANES // half).reshape(1, LANES)

    for layer in range(w_in.shape[0]):
        wl = w_in[layer]
        w1 = jnp.concatenate(
            [wl[:, :n_qkv + 2 * nh], jnp.zeros((d, LANES - 2 * nh), wl.dtype)], axis=1).astype(BF16)
        wg = wl[:, n_qkv + 2 * nh:].astype(BF16)
        bias = jnp.concatenate(
            [i_bias[layer], f_bias[layer], jnp.zeros((LANES - 2 * nh,), F32)]).reshape(1, LANES)

        qT, k, kmean, vT, qm, kmT, vm, om, gif, gifT = _in_proj(
            x, positions, invf, norm_mix_g[layer].reshape(1, d), w1, conv_mlstm[layer], bias)
        attn = _moba(qT, k, vT, kmean.reshape(b, s // MOBA_BLOCK, aw))
        y_m = _mlstm(qm, kmT, vm, om, gif, gifT, mlstm_norm_g[layer].reshape(1, mw))
        x1 = _mix(x.reshape(b * s, d), attn.reshape(b * s, aw), y_m.reshape(b * s, mw),
                  norm_mix_g[layer].reshape(1, d), wg, w_branch_attn[layer].astype(BF16),
                  w_branch_mlstm[layer].astype(BF16), w_out[layer].astype(BF16)).reshape(b, s, d)

        wu = w_up[layer].astype(BF16)
        cf = conv_ffn[layer]
        last = layer == w_in.shape[0] - 1
        gfin = norm_final_g.reshape(1, d) if last else None
        assert last, "only the final layer fuses the output norm"
        x = _ffn(x1, norm_ffn_g[layer].reshape(1, d),
                 _chunk_cols(wu[:, :d_ff], FFN_CHUNK), _chunk_cols(wu[:, d_ff:], FFN_CHUNK),
                 _chunk_cols(cf[:, :d_ff], FFN_CHUNK), _chunk_cols(cf[:, d_ff:], FFN_CHUNK),
                 w_down[layer].astype(BF16).reshape(d_ff // FFN_CHUNK, FFN_CHUNK, d), gfin)
    return x
```

```python
import functools
import math

import jax
import jax.numpy as jnp
from jax import lax
from jax.experimental import pallas as pl
from jax.experimental.pallas import tpu as pltpu

F32 = jnp.float32
BF16 = jnp.bfloat16

EPS = 1e-6
NEG = -1e30
ROPE_THETA = 10000.0

N_ATTN_HEADS = 8
ATTN_HEAD_DIM = 64
ATTN_WIDTH = N_ATTN_HEADS * ATTN_HEAD_DIM
MOBA_BLOCK = 256
MOBA_TOPK = 3
N_MLSTM_HEADS = 4
MLSTM_HEAD_DIM = 128
MLSTM_WIDTH = N_MLSTM_HEADS * MLSTM_HEAD_DIM

LANES = 128
HALO = 16
SEQ_TILE = MOBA_BLOCK
MIX_TILE = 512
FFN_TILE = 512
FFN_CHUNK = 256
MOBA_PAIRS_PER_STEP = 2
LOG2_E = math.log2(math.e)
VMEM_LIMIT = 56 * 1024 * 1024


def _dot(a, b):
    return jnp.dot(a, b, preferred_element_type=F32)


def _rms(x, g):
    ms = jnp.mean(x * x, axis=-1, keepdims=True)
    return x * lax.rsqrt(ms + EPS) * g


def _sigmoid(x):
    return 1.0 / (1.0 + jnp.exp(-x))


def _const_spec(shape):
    nd = len(shape)
    return pl.BlockSpec(shape, lambda *_: (0,) * nd, pipeline_mode=pl.Buffered(1))


def _in_proj_kernel(x_ref, xh_ref, pos_ref, invf_ref, g_ref, w_ref, cw_ref, bias_ref,
                    qT_ref, k_ref, kmean_ref, vT_ref, qm_ref, kmT_ref, vm_ref, om_ref, gif_ref, gifT_ref,
                    ext_ref):
    tm = SEQ_TILE
    aw, mw = ATTN_WIDTH, MLSTM_WIDTH
    j = pl.program_id(1)
    g = g_ref[...]
    h = _rms(x_ref[0], g).astype(BF16)

    ang = pos_ref[0].astype(F32) * invf_ref[...]
    cos = jnp.cos(ang)
    sin = jnp.sin(ang)
    lane = lax.broadcasted_iota(jnp.int32, (tm, LANES), 1)
    first_half = (lane % ATTN_HEAD_DIM) < (ATTN_HEAD_DIM // 2)
    sin_signed = jnp.where(first_half, -sin, sin)

    def rope(t):
        outs = []
        for c in range(aw // LANES):
            tg = t[:, c * LANES:(c + 1) * LANES]
            swapped = jnp.where(first_half, pltpu.roll(tg, LANES - 32, 1), pltpu.roll(tg, 32, 1))
            outs.append(tg * cos + swapped * sin_signed)
        return jnp.concatenate(outs, axis=1)

    qk = _dot(h, w_ref[:, 0:2 * aw])
    q = rope(qk[:, :aw]) * (LOG2_E / math.sqrt(ATTN_HEAD_DIM))
    k = rope(qk[:, aw:])
    qT_ref[0, 0] = q.T.astype(BF16)
    k_ref[0, 0] = k.astype(BF16)
    kmean_ref[0] = jnp.mean(k, axis=0, keepdims=True)
    v = _dot(h, w_ref[:, 2 * aw:3 * aw])
    vT_ref[0, 0] = v.T.astype(BF16)

    c0 = 3 * aw
    pm = _dot(h, w_ref[:, c0:c0 + 2 * mw])
    hh = _rms(xh_ref[0], g).astype(BF16)
    ph = _dot(hh, w_ref[:, c0:c0 + 2 * mw])
    ext_ref[0:HALO, :] = jnp.where(j == 0, 0.0, ph)
    ext_ref[HALO:HALO + tm, :] = pm
    cw = cw_ref[...]
    conv = cw[0:1] * ext_ref[HALO - 3:HALO - 3 + tm, :]
    conv = conv + cw[1:2] * ext_ref[HALO - 2:HALO - 2 + tm, :]
    conv = conv + cw[2:3] * ext_ref[HALO - 1:HALO - 1 + tm, :]
    conv = conv + cw[3:4] * pm
    act = conv * _sigmoid(conv)
    qm_ref[0] = (act[:, :mw] * (1.0 / math.sqrt(MLSTM_HEAD_DIM))).astype(BF16)
    kmT_ref[0, 0] = act[:, mw:].T.astype(BF16)

    c1 = c0 + 2 * mw
    vo = _dot(h, w_ref[:, c1:c1 + 2 * mw])
    vm_ref[0] = vo[:, :mw].astype(BF16)
    om_ref[0] = vo[:, mw:]

    c2 = c1 + 2 * mw
    gi = _dot(h, w_ref[:, c2:c2 + LANES]) + bias_ref[...]
    log_sig = jnp.minimum(gi, 0.0) - jnp.log1p(jnp.exp(-jnp.abs(gi)))
    gt = jnp.where(lane < N_MLSTM_HEADS, gi, log_sig)
    gif_ref[0] = gt[:, 0:2 * N_MLSTM_HEADS]
    gifT_ref[0] = gt.T[0:2 * N_MLSTM_HEADS, :]


def _in_proj(x, positions, invf, g, w, cw, bias):
    b, s, d = x.shape
    tm = SEQ_TILE
    nb = s // tm
    aw, mw = ATTN_WIDTH, MLSTM_WIDTH
    nh2 = 2 * N_MLSTM_HEADS
    out_shape = [
        jax.ShapeDtypeStruct((b, nb, aw, tm), BF16),
        jax.ShapeDtypeStruct((b, nb, tm, aw), BF16),
        jax.ShapeDtypeStruct((b * nb, 1, aw), F32),
        jax.ShapeDtypeStruct((b, nb, aw, tm), BF16),
        jax.ShapeDtypeStruct((b, s, mw), BF16),
        jax.ShapeDtypeStruct((b, nb, mw, tm), BF16),
        jax.ShapeDtypeStruct((b, s, mw), BF16),
        jax.ShapeDtypeStruct((b, s, mw), F32),
        jax.ShapeDtypeStruct((b, s, nh2), F32),
        jax.ShapeDtypeStruct((b, nh2, s), F32),
    ]
    blk4 = lambda r, c: pl.BlockSpec((1, 1, r, c), lambda bi, ji: (bi, ji, 0, 0))
    row3 = lambda c: pl.BlockSpec((1, tm, c), lambda bi, ji: (bi, ji, 0))
    out_specs = [
        blk4(aw, tm), blk4(tm, aw),
        pl.BlockSpec((1, 1, aw), lambda bi, ji: (bi * nb + ji, 0, 0)),
        blk4(aw, tm), row3(mw), blk4(mw, tm), row3(mw), row3(mw), row3(nh2),
        pl.BlockSpec((1, nh2, tm), lambda bi, ji: (bi, 0, ji)),
    ]
    in_specs = [
        pl.BlockSpec((1, tm, d), lambda bi, ji: (bi, ji, 0)),
        pl.BlockSpec((1, HALO, d), lambda bi, ji: (bi, jnp.maximum(ji * (tm // HALO) - 1, 0), 0)),
        pl.BlockSpec((1, tm, 1), lambda bi, ji: (bi, ji, 0)),
        _const_spec(invf.shape), _const_spec(g.shape), _const_spec(w.shape),
        _const_spec(cw.shape), _const_spec(bias.shape),
    ]
    return pl.pallas_call(
        _in_proj_kernel,
        grid=(b, nb),
        in_specs=in_specs,
        out_specs=out_specs,
        out_shape=out_shape,
        scratch_shapes=[pltpu.VMEM((HALO + tm, 2 * mw), F32)],
        compiler_params=pltpu.CompilerParams(
            dimension_semantics=("parallel", "parallel"), vmem_limit_bytes=VMEM_LIMIT),
        name="in_proj",
    )(x, x, positions.reshape(b, s, 1), invf, g, w, cw, bias)


def _moba_kernel(qT_ref, k_ref, vT_ref, kmean_ref, out_ref,
                 qm_ref, bias_ref, s_ref, cm_ref, p_ref, a_ref, m_ref, l_ref, acc_ref):
    blk = MOBA_BLOCK
    hd = ATTN_HEAD_DIM
    nb = k_ref.shape[1]
    nh = 2 * (qT_ref.shape[2] // LANES)
    j = pl.program_id(2)
    dim_row = lax.broadcasted_iota(jnp.int32, (LANES, blk), 0)
    blk_id = lax.broadcasted_iota(jnp.int32, (nb, blk), 0)
    key_row = lax.broadcasted_iota(jnp.int32, (blk, blk), 0)
    qry_col = lax.broadcasted_iota(jnp.int32, (blk, blk), 1)

    def lanes(h):
        return slice((h // 2) * LANES, (h // 2 + 1) * LANES)

    def head_rows(h):
        return slice(h * hd, (h + 1) * hd)

    for h in range(nh):
        qT = qT_ref[0, 0, lanes(h), :]
        qh = jnp.where((dim_row >= hd) == (h % 2 == 1), qT, jnp.zeros_like(qT))
        qm_ref[h] = qh
        km = kmean_ref[0, :, lanes(h)]
        km_hi = km.astype(BF16)
        km_lo = (km - km_hi.astype(F32)).astype(BF16)
        gate = _dot(km_hi, qh) + _dot(km_lo, qh)
        gate = jnp.where(blk_id < j, gate, NEG)
        rank = jnp.zeros((nb, blk), jnp.int32)
        for m in range(nb):
            gm = gate[m:m + 1, :]
            ahead = (gm > gate) | ((gm == gate) & (m < blk_id))
            rank = rank + ahead.astype(jnp.int32)
        selected = (rank < MOBA_TOPK) & (blk_id < j)
        bias_ref[h, 0:1, :] = jnp.zeros((1, blk), F32)
        bias_ref[h, 1:nb + 1, :] = jnp.where(selected, 0.0, NEG)
        s_own = jnp.where(key_row <= qry_col, _dot(k_ref[0, j, :, lanes(h)], qh), NEG)
        s_ref[h] = s_own
        cm_ref[h] = jnp.max(s_own, axis=0, keepdims=True)
        p_ref[h] = jnp.zeros((blk, blk), BF16)
        a_ref[h] = jnp.zeros((1, blk), F32)
        m_ref[h] = jnp.full((1, blk), NEG, F32)
        l_ref[h] = jnp.zeros((1, blk), F32)
        acc_ref[h] = jnp.zeros((hd, blk), F32)

    def body(i, carry):
        v_blk = jnp.where(i == 1, j, jnp.clip(i - 2, 0, nb - 1))
        bias_row = jnp.minimum(i, j + 1)
        k_blk = jnp.minimum(i, nb - 1)
        for h in range(nh):
            vTb = vT_ref[0, v_blk, head_rows(h), :]
            acc_ref[h] = a_ref[h] * acc_ref[h] + _dot(vTb, p_ref[h])
        for h in range(nh):
            bias = bias_ref[h, pl.ds(bias_row, 1), :]
            m_old = m_ref[h]
            m_new = jnp.maximum(m_old, cm_ref[h] + bias)
            alpha = jnp.exp2(m_old - m_new)
            p = jnp.exp2(s_ref[h] - (m_new - bias))
            l_ref[h] = alpha * l_ref[h] + jnp.sum(p, axis=0, keepdims=True)
            m_ref[h] = m_new
            a_ref[h] = alpha
            p_ref[h] = p.astype(BF16)
        for h in range(nh):
            sT = _dot(k_ref[0, k_blk, :, lanes(h)], qm_ref[h])
            s_ref[h] = sT
            cm_ref[h] = jnp.max(sT, axis=0, keepdims=True)
        return carry

    lax.fori_loop(0, j + 2, body, 0)
    for pair in range(nh // 2):
        oT = jnp.concatenate([acc_ref[h] / l_ref[h] for h in (2 * pair, 2 * pair + 1)], axis=0)
        out_ref[0, :, pair * LANES:(pair + 1) * LANES] = oT.T.astype(BF16)


def _moba(qT, k, vT, kmean):
    b, nb, aw, blk = qT.shape
    s = nb * blk
    w = MOBA_PAIRS_PER_STEP * LANES
    nh = 2 * MOBA_PAIRS_PER_STEP
    return pl.pallas_call(
        _moba_kernel,
        grid=(b, aw // w, nb),
        in_specs=[
            pl.BlockSpec((1, 1, w, blk), lambda bi, pi, ji: (bi, ji, pi, 0)),
            pl.BlockSpec((1, nb, blk, w), lambda bi, pi, ji: (bi, 0, 0, pi)),
            pl.BlockSpec((1, nb, w, blk), lambda bi, pi, ji: (bi, 0, pi, 0)),
            pl.BlockSpec((1, nb, w), lambda bi, pi, ji: (bi, 0, pi)),
        ],
        out_specs=pl.BlockSpec((1, blk, w), lambda bi, pi, ji: (bi, ji, pi)),
        out_shape=jax.ShapeDtypeStruct((b, s, aw), BF16),
        scratch_shapes=[pltpu.VMEM((nh, LANES, blk), BF16),
                        pltpu.VMEM((nh, nb + 8, blk), F32),
                        pltpu.VMEM((nh, blk, blk), F32),
                        pltpu.VMEM((nh, 1, blk), F32),
                        pltpu.VMEM((nh, blk, blk), BF16),
                        pltpu.VMEM((nh, 1, blk), F32),
                        pltpu.VMEM((nh, 1, blk), F32),
                        pltpu.VMEM((nh, 1, blk), F32),
                        pltpu.VMEM((nh, ATTN_HEAD_DIM, blk), F32)],
        compiler_params=pltpu.CompilerParams(
            dimension_semantics=("parallel", "parallel", "parallel"), vmem_limit_bytes=VMEM_LIMIT),
        name="moba",
    )(qT, k, vT, kmean)


def _split3(x):
    hi = x.astype(BF16)
    r = x - hi.astype(F32)
    mid = r.astype(BF16)
    lo = (r - mid.astype(F32)).astype(BF16)
    return hi, mid, lo


def _mlstm_kernel(q_ref, kT_ref, v_ref, o_ref, gif_ref, gifT_ref, g_ref, out_ref, c_ref, m_ref):
    L = SEQ_TILE
    nh, hd = N_MLSTM_HEADS, MLSTM_HEAD_DIM
    c = pl.program_id(1)

    @pl.when(c == 0)
    def _():
        c_ref[...] = jnp.zeros_like(c_ref)
        m_ref[...] = jnp.zeros_like(m_ref)

    t_idx = lax.broadcasted_iota(jnp.int32, (L, L), 0)
    s_idx = lax.broadcasted_iota(jnp.int32, (L, L), 1)
    causal = s_idx <= t_idx
    tril = jnp.where(causal, 1.0, 0.0).astype(BF16)
    triu = jnp.where(t_idx <= s_idx, 1.0, 0.0).astype(BF16)

    gi_col = gif_ref[0]
    gi_row = gifT_ref[0]
    b_col_all = sum(_dot(tril, part) for part in _split3(gi_col))
    b_row_all = sum(_dot(part, triu) for part in _split3(gi_row))
    ones_col = jnp.where(lax.broadcasted_iota(jnp.int32, (L, hd), 1) == 0, 1.0, 0.0).astype(BF16)
    gn = g_ref[...]

    for h in range(nh):
        a_row = gi_row[h:h + 1, :]
        b_row = b_row_all[nh + h:nh + h + 1, :]
        b_col = b_col_all[:, nh + h:nh + h + 1]
        m_prev = m_ref[h]
        c_old = c_ref[h]

        dmat = jnp.where(causal, b_col - b_row + a_row, NEG)
        inter = b_col + m_prev
        m_t = jnp.maximum(inter, jnp.max(dmat, axis=1, keepdims=True))
        w_intra = jnp.exp(dmat - m_t)
        w_inter = jnp.exp(inter - m_t)

        q = q_ref[0, :, h * hd:(h + 1) * hd]
        kT = kT_ref[0, 0, h * hd:(h + 1) * hd, :]
        v_ext = jnp.concatenate([v_ref[0, :, h * hd:(h + 1) * hd], ones_col], axis=1)
        scores = w_intra * _dot(q, kT)
        num_ext = w_inter * _dot(q, c_old.astype(BF16)) + _dot(scores.astype(BF16), v_ext)
        num = num_ext[:, :hd]
        den = num_ext[:, hd:hd + 1]
        h_out = num / jnp.maximum(jnp.abs(den), jnp.exp(-m_t))

        b_last = b_row[:, L - 1:L]
        dec = b_last - b_row + a_row
        m_new = jnp.maximum(b_last + m_prev, jnp.max(dec, axis=1, keepdims=True))
        wk = jnp.exp(dec - m_new)
        carry_scale = jnp.exp(b_last + m_prev - m_new)
        kw = (kT.astype(F32) * wk).astype(BF16)
        c_ref[h] = carry_scale * c_old + _dot(kw, v_ext)
        m_ref[h] = m_new

        hn = _rms(h_out, gn[:, h * hd:(h + 1) * hd])
        gate_o = _sigmoid(o_ref[0, :, h * hd:(h + 1) * hd])
        out_ref[0, :, h * hd:(h + 1) * hd] = (hn * gate_o).astype(BF16)


def _mlstm(qm, kmT, vm, om, gif, gifT, g):
    b, s, mw = qm.shape
    L = SEQ_TILE
    nc = s // L
    nh2 = 2 * N_MLSTM_HEADS
    row = lambda cdim: pl.BlockSpec((1, L, cdim), lambda bi, ci: (bi, ci, 0))
    return pl.pallas_call(
        _mlstm_kernel,
        grid=(b, nc),
        in_specs=[
            row(mw),
            pl.BlockSpec((1, 1, mw, L), lambda bi, ci: (bi, ci, 0, 0)),
            row(mw), row(mw), row(nh2),
            pl.BlockSpec((1, nh2, L), lambda bi, ci: (bi, 0, ci)),
            _const_spec(g.shape),
        ],
        out_specs=row(mw),
        out_shape=jax.ShapeDtypeStruct((b, s, mw), BF16),
        scratch_shapes=[pltpu.VMEM((N_MLSTM_HEADS, MLSTM_HEAD_DIM, 2 * MLSTM_HEAD_DIM), F32),
                        pltpu.VMEM((N_MLSTM_HEADS, 1, 1), F32)],
        compiler_params=pltpu.CompilerParams(
            dimension_semantics=("parallel", "arbitrary"), vmem_limit_bytes=VMEM_LIMIT),
        name="mlstm",
    )(qm, kmT, vm, om, gif, gifT, g)


def _mix_kernel(x_ref, a_ref, y_ref, g_ref, wg_ref, wa_ref, wm_ref, wo_ref, out_ref):
    d = x_ref.shape[-1]
    x = x_ref[...]
    h = _rms(x, g_ref[...]).astype(BF16)
    gate_a = _sigmoid(_dot(h, wg_ref[:, :d]))
    gate_m = _sigmoid(_dot(h, wg_ref[:, d:]))
    merged = gate_a * _dot(a_ref[...], wa_ref[...]) + gate_m * _dot(y_ref[...], wm_ref[...])
    out_ref[...] = x + _dot(merged.astype(BF16), wo_ref[...])


def _mix(x2d, a2d, y2d, g, wg, wa, wm, wo):
    t, d = x2d.shape
    tm = MIX_TILE
    row = lambda cdim: pl.BlockSpec((tm, cdim), lambda i: (i, 0))
    return pl.pallas_call(
        _mix_kernel,
        grid=(t // tm,),
        in_specs=[row(d), row(a2d.shape[1]), row(y2d.shape[1]), _const_spec(g.shape),
                  _const_spec(wg.shape), _const_spec(wa.shape), _const_spec(wm.shape), _const_spec(wo.shape)],
        out_specs=row(d),
        out_shape=jax.ShapeDtypeStruct((t, d), F32),
        compiler_params=pltpu.CompilerParams(
            dimension_semantics=("parallel",), vmem_limit_bytes=VMEM_LIMIT),
        name="mix",
    )(x2d, a2d, y2d, g, wg, wa, wm, wo)


def _ffn_kernel(x_ref, xh_ref, g_ref, wg_ref, wu_ref, cg_ref, cu_ref, wd_ref, gf_ref, out_ref,
                hext_ref, acc_ref):
    tm = FFN_TILE
    j = pl.program_id(1)
    x = x_ref[0]
    g = g_ref[...]
    hext_ref[HALO:HALO + tm, :] = _rms(x, g).astype(BF16)
    hh = _rms(xh_ref[0], g)
    hext_ref[0:HALO, :] = jnp.where(j == 0, 0.0, hh).astype(BF16)
    acc_ref[...] = jnp.zeros_like(acc_ref)

    def conv(u, cw):
        out = cw[0:1] * u[HALO - 2:HALO - 2 + tm]
        out = out + cw[1:2] * u[HALO - 1:HALO - 1 + tm]
        return out + cw[2:3] * u[HALO:HALO + tm]

    def body(c, carry):
        hx = hext_ref[...]
        gate = conv(_dot(hx, wg_ref[c]), cg_ref[c])
        up = conv(_dot(hx, wu_ref[c]), cu_ref[c])
        act = (gate * _sigmoid(gate) * up).astype(BF16)
        acc_ref[...] += _dot(act, wd_ref[c])
        return carry

    lax.fori_loop(0, wg_ref.shape[0], body, 0)
    out_ref[0] = _rms(x + acc_ref[...], gf_ref[...])


def _ffn(x1, g2, wg, wu, cg, cu, wd, gf):
    b, s, d = x1.shape
    tm = FFN_TILE
    return pl.pallas_call(
        _ffn_kernel,
        grid=(b, s // tm),
        in_specs=[
            pl.BlockSpec((1, tm, d), lambda bi, ji: (bi, ji, 0)),
            pl.BlockSpec((1, HALO, d), lambda bi, ji: (bi, jnp.maximum(ji * (tm // HALO) - 1, 0), 0)),
            _const_spec(g2.shape), _const_spec(wg.shape), _const_spec(wu.shape),
            _const_spec(cg.shape), _const_spec(cu.shape), _const_spec(wd.shape), _const_spec(gf.shape),
        ],
        out_specs=pl.BlockSpec((1, tm, d), lambda bi, ji: (bi, ji, 0)),
        out_shape=jax.ShapeDtypeStruct((b, s, d), F32),
        scratch_shapes=[pltpu.VMEM((HALO + tm, d), BF16), pltpu.VMEM((tm, d), F32)],
        compiler_params=pltpu.CompilerParams(
            dimension_semantics=("parallel", "parallel"), vmem_limit_bytes=VMEM_LIMIT),
        name="ffn",
    )(x1, x1, g2, wg, wu, cg, cu, wd, gf)


def _chunk_cols(w, chunk):
    kdim, n = w.shape
    return w.reshape(kdim, n // chunk, chunk).transpose(1, 0, 2)


def kernel(x, positions, norm_mix_g, w_in, conv_mlstm, i_bias, f_bias, mlstm_norm_g, w_branch_attn,
           w_branch_mlstm, w_out, norm_ffn_g, w_up, conv_ffn, w_down, norm_final_g):
    b, s, d = x.shape
    aw, mw, nh = ATTN_WIDTH, MLSTM_WIDTH, N_MLSTM_HEADS
    assert s % FFN_TILE == 0 and s % SEQ_TILE == 0 and d % LANES == 0
    n_qkv = 3 * aw + 4 * mw
    d_ff = w_down.shape[1]
    assert d_ff % FFN_CHUNK == 0
    half = ATTN_HEAD_DIM // 2
    inv_freq = ROPE_THETA ** (-(jnp.arange(half, dtype=F32) / half))
    invf = jnp.tile(inv_freq, LANES // half).reshape(1, LANES)

    for layer in range(w_in.shape[0]):
        wl = w_in[layer]
        w1 = jnp.concatenate(
            [wl[:, :n_qkv + 2 * nh], jnp.zeros((d, LANES - 2 * nh), wl.dtype)], axis=1).astype(BF16)
        wg = wl[:, n_qkv + 2 * nh:].astype(BF16)
        bias = jnp.concatenate(
            [i_bias[layer], f_bias[layer], jnp.zeros((LANES - 2 * nh,), F32)]).reshape(1, LANES)

        qT, k, kmean, vT, qm, kmT, vm, om, gif, gifT = _in_proj(
            x, positions, invf, norm_mix_g[layer].reshape(1, d), w1, conv_mlstm[layer], bias)
        attn = _moba(qT, k, vT, kmean.reshape(b, s // MOBA_BLOCK, aw))
        y_m = _mlstm(qm, kmT, vm, om, gif, gifT, mlstm_norm_g[layer].reshape(1, mw))
        x1 = _mix(x.reshape(b * s, d), attn.reshape(b * s, aw), y_m.reshape(b * s, mw),
                  norm_mix_g[layer].reshape(1, d), wg, w_branch_attn[layer].astype(BF16),
                  w_branch_mlstm[layer].astype(BF16), w_out[layer].astype(BF16)).reshape(b, s, d)

        wu = w_up[layer].astype(BF16)
        cf = conv_ffn[layer]
        last = layer == w_in.shape[0] - 1
        gfin = norm_final_g.reshape(1, d) if last else None
        assert last, "only the final layer fuses the output norm"
        x = _ffn(x1, norm_ffn_g[layer].reshape(1, d),
                 _chunk_cols(wu[:, :d_ff], FFN_CHUNK), _chunk_cols(wu[:, d_ff:], FFN_CHUNK),
                 _chunk_cols(cf[:, :d_ff], FFN_CHUNK), _chunk_cols(cf[:, d_ff:], FFN_CHUNK),
                 w_down[layer].astype(BF16).reshape(d_ff // FFN_CHUNK, FFN_CHUNK, d), gfin)
    return x
```

```python
import functools
import math

import jax
import jax.numpy as jnp
from jax import lax
from jax.experimental import pallas as pl
from jax.experimental.pallas import tpu as pltpu

F32 = jnp.float32
BF16 = jnp.bfloat16

EPS = 1e-6
NEG = -1e30
ROPE_THETA = 10000.0

N_ATTN_HEADS = 8
ATTN_HEAD_DIM = 64
ATTN_WIDTH = N_ATTN_HEADS * ATTN_HEAD_DIM
MOBA_BLOCK = 256
MOBA_TOPK = 3
N_MLSTM_HEADS = 4
MLSTM_HEAD_DIM = 128
MLSTM_WIDTH = N_MLSTM_HEADS * MLSTM_HEAD_DIM

LANES = 128
HALO = 16
SEQ_TILE = MOBA_BLOCK
MIX_TILE = 512
FFN_TILE = 512
FFN_CHUNK = 256
FFN_ROWS = 64
MOBA_PAIRS_PER_STEP = 2
LOG2_E = math.log2(math.e)
VMEM_LIMIT = 56 * 1024 * 1024


def _dot(a, b):
    return jnp.dot(a, b, preferred_element_type=F32)


def _rms(x, g):
    ms = jnp.mean(x * x, axis=-1, keepdims=True)
    return x * lax.rsqrt(ms + EPS) * g


def _sigmoid(x):
    return 1.0 / (1.0 + jnp.exp(-x))


def _const_spec(shape):
    nd = len(shape)
    return pl.BlockSpec(shape, lambda *_: (0,) * nd, pipeline_mode=pl.Buffered(1))


def _in_proj_kernel(x_ref, xh_ref, pos_ref, invf_ref, g_ref, w_ref, cw_ref, bias_ref,
                    qT_ref, k_ref, kmean_ref, vT_ref, qm_ref, kmT_ref, vm_ref, om_ref, gif_ref, gifT_ref,
                    ext_ref):
    tm = SEQ_TILE
    aw, mw = ATTN_WIDTH, MLSTM_WIDTH
    j = pl.program_id(1)
    g = g_ref[...]
    h = _rms(x_ref[0], g).astype(BF16)

    ang = pos_ref[0].astype(F32) * invf_ref[...]
    cos = jnp.cos(ang)
    sin = jnp.sin(ang)
    lane = lax.broadcasted_iota(jnp.int32, (tm, LANES), 1)
    first_half = (lane % ATTN_HEAD_DIM) < (ATTN_HEAD_DIM // 2)
    sin_signed = jnp.where(first_half, -sin, sin)

    def rope(t):
        outs = []
        for c in range(aw // LANES):
            tg = t[:, c * LANES:(c + 1) * LANES]
            swapped = jnp.where(first_half, pltpu.roll(tg, LANES - 32, 1), pltpu.roll(tg, 32, 1))
            outs.append(tg * cos + swapped * sin_signed)
        return jnp.concatenate(outs, axis=1)

    qk = _dot(h, w_ref[:, 0:2 * aw])
    q = rope(qk[:, :aw]) * (LOG2_E / math.sqrt(ATTN_HEAD_DIM))
    k = rope(qk[:, aw:])
    qT_ref[0, 0] = q.T.astype(BF16)
    k_ref[0, 0] = k.astype(BF16)
    kmean_ref[0] = jnp.mean(k, axis=0, keepdims=True)
    v = _dot(h, w_ref[:, 2 * aw:3 * aw])
    vT_ref[0, 0] = v.T.astype(BF16)

    c0 = 3 * aw
    pm = _dot(h, w_ref[:, c0:c0 + 2 * mw])
    hh = _rms(xh_ref[0], g).astype(BF16)
    ph = _dot(hh, w_ref[:, c0:c0 + 2 * mw])
    ext_ref[0:HALO, :] = jnp.where(j == 0, 0.0, ph)
    ext_ref[HALO:HALO + tm, :] = pm
    cw = cw_ref[...]
    conv = cw[0:1] * ext_ref[HALO - 3:HALO - 3 + tm, :]
    conv = conv + cw[1:2] * ext_ref[HALO - 2:HALO - 2 + tm, :]
    conv = conv + cw[2:3] * ext_ref[HALO - 1:HALO - 1 + tm, :]
    conv = conv + cw[3:4] * pm
    act = conv * _sigmoid(conv)
    qm_ref[0] = (act[:, :mw] * (1.0 / math.sqrt(MLSTM_HEAD_DIM))).astype(BF16)
    kmT_ref[0, 0] = act[:, mw:].T.astype(BF16)

    c1 = c0 + 2 * mw
    vo = _dot(h, w_ref[:, c1:c1 + 2 * mw])
    vm_ref[0] = vo[:, :mw].astype(BF16)
    om_ref[0] = vo[:, mw:]

    c2 = c1 + 2 * mw
    gi = _dot(h, w_ref[:, c2:c2 + LANES]) + bias_ref[...]
    log_sig = jnp.minimum(gi, 0.0) - jnp.log1p(jnp.exp(-jnp.abs(gi)))
    gt = jnp.where(lane < N_MLSTM_HEADS, gi, log_sig)
    gif_ref[0] = gt[:, 0:2 * N_MLSTM_HEADS]
    gifT_ref[0] = gt.T[0:2 * N_MLSTM_HEADS, :]


def _in_proj(x, positions, invf, g, w, cw, bias):
    b, s, d = x.shape
    tm = SEQ_TILE
    nb = s // tm
    aw, mw = ATTN_WIDTH, MLSTM_WIDTH
    nh2 = 2 * N_MLSTM_HEADS
    out_shape = [
        jax.ShapeDtypeStruct((b, nb, aw, tm), BF16),
        jax.ShapeDtypeStruct((b, nb, tm, aw), BF16),
        jax.ShapeDtypeStruct((b * nb, 1, aw), F32),
        jax.ShapeDtypeStruct((b, nb, aw, tm), BF16),
        jax.ShapeDtypeStruct((b, s, mw), BF16),
        jax.ShapeDtypeStruct((b, nb, mw, tm), BF16),
        jax.ShapeDtypeStruct((b, s, mw), BF16),
        jax.ShapeDtypeStruct((b, s, mw), F32),
        jax.ShapeDtypeStruct((b, s, nh2), F32),
        jax.ShapeDtypeStruct((b, nh2, s), F32),
    ]
    blk4 = lambda r, c: pl.BlockSpec((1, 1, r, c), lambda bi, ji: (bi, ji, 0, 0))
    row3 = lambda c: pl.BlockSpec((1, tm, c), lambda bi, ji: (bi, ji, 0))
    out_specs = [
        blk4(aw, tm), blk4(tm, aw),
        pl.BlockSpec((1, 1, aw), lambda bi, ji: (bi * nb + ji, 0, 0)),
        blk4(aw, tm), row3(mw), blk4(mw, tm), row3(mw), row3(mw), row3(nh2),
        pl.BlockSpec((1, nh2, tm), lambda bi, ji: (bi, 0, ji)),
    ]
    in_specs = [
        pl.BlockSpec((1, tm, d), lambda bi, ji: (bi, ji, 0)),
        pl.BlockSpec((1, HALO, d), lambda bi, ji: (bi, jnp.maximum(ji * (tm // HALO) - 1, 0), 0)),
        pl.BlockSpec((1, tm, 1), lambda bi, ji: (bi, ji, 0)),
        _const_spec(invf.shape), _const_spec(g.shape), _const_spec(w.shape),
        _const_spec(cw.shape), _const_spec(bias.shape),
    ]
    return pl.pallas_call(
        _in_proj_kernel,
        grid=(b, nb),
        in_specs=in_specs,
        out_specs=out_specs,
        out_shape=out_shape,
        scratch_shapes=[pltpu.VMEM((HALO + tm, 2 * mw), F32)],
        compiler_params=pltpu.CompilerParams(
            dimension_semantics=("parallel", "parallel"), vmem_limit_bytes=VMEM_LIMIT),
        name="in_proj",
    )(x, x, positions.reshape(b, s, 1), invf, g, w, cw, bias)


def _moba_kernel(qT_ref, k_ref, vT_ref, kmean_ref, out_ref,
                 qm_ref, bias_ref, s_ref, cm_ref, p_ref, a_ref, m_ref, l_ref, acc_ref):
    blk = MOBA_BLOCK
    hd = ATTN_HEAD_DIM
    nb = k_ref.shape[1]
    nh = 2 * (qT_ref.shape[2] // LANES)
    j = pl.program_id(2)
    dim_row = lax.broadcasted_iota(jnp.int32, (LANES, blk), 0)
    blk_id = lax.broadcasted_iota(jnp.int32, (nb, blk), 0)
    key_row = lax.broadcasted_iota(jnp.int32, (blk, blk), 0)
    qry_col = lax.broadcasted_iota(jnp.int32, (blk, blk), 1)

    def lanes(h):
        return slice((h // 2) * LANES, (h // 2 + 1) * LANES)

    def head_rows(h):
        return slice(h * hd, (h + 1) * hd)

    for h in range(nh):
        qT = qT_ref[0, 0, lanes(h), :]
        qh = jnp.where((dim_row >= hd) == (h % 2 == 1), qT, jnp.zeros_like(qT))
        qm_ref[h] = qh
        km = kmean_ref[0, :, lanes(h)]
        km_hi = km.astype(BF16)
        km_lo = (km - km_hi.astype(F32)).astype(BF16)
        gate = _dot(km_hi, qh) + _dot(km_lo, qh)
        gate = jnp.where(blk_id < j, gate, NEG)
        rank = jnp.zeros((nb, blk), jnp.int32)
        for m in range(nb):
            gm = gate[m:m + 1, :]
            ahead = (gm > gate) | ((gm == gate) & (m < blk_id))
            rank = rank + ahead.astype(jnp.int32)
        selected = (rank < MOBA_TOPK) & (blk_id < j)
        bias_ref[h, 0:1, :] = jnp.zeros((1, blk), F32)
        bias_ref[h, 1:nb + 1, :] = jnp.where(selected, 0.0, NEG)
        s_own = jnp.where(key_row <= qry_col, _dot(k_ref[0, j, :, lanes(h)], qh), NEG)
        s_ref[h] = s_own
        cm_ref[h] = jnp.max(s_own, axis=0, keepdims=True)
        p_ref[h] = jnp.zeros((blk, blk), BF16)
        a_ref[h] = jnp.zeros((1, blk), F32)
        m_ref[h] = jnp.full((1, blk), NEG, F32)
        l_ref[h] = jnp.zeros((1, blk), F32)
        acc_ref[h] = jnp.zeros((hd, blk), F32)

    def body(i, carry):
        v_blk = jnp.where(i == 1, j, jnp.clip(i - 2, 0, nb - 1))
        bias_row = jnp.minimum(i, j + 1)
        k_blk = jnp.minimum(i, nb - 1)
        for h in range(nh):
            vTb = vT_ref[0, v_blk, head_rows(h), :]
            acc_ref[h] = a_ref[h] * acc_ref[h] + _dot(vTb, p_ref[h])
        for h in range(nh):
            bias = bias_ref[h, pl.ds(bias_row, 1), :]
            m_old = m_ref[h]
            m_new = jnp.maximum(m_old, cm_ref[h] + bias)
            alpha = jnp.exp2(m_old - m_new)
            p = jnp.exp2(s_ref[h] - (m_new - bias))
            l_ref[h] = alpha * l_ref[h] + jnp.sum(p, axis=0, keepdims=True)
            m_ref[h] = m_new
            a_ref[h] = alpha
            p_ref[h] = p.astype(BF16)
        for h in range(nh):
            sT = _dot(k_ref[0, k_blk, :, lanes(h)], qm_ref[h])
            s_ref[h] = sT
            cm_ref[h] = jnp.max(sT, axis=0, keepdims=True)
        return carry

    lax.fori_loop(0, j + 2, body, 0)
    for pair in range(nh // 2):
        oT = jnp.concatenate([acc_ref[h] / l_ref[h] for h in (2 * pair, 2 * pair + 1)], axis=0)
        out_ref[0, :, pair * LANES:(pair + 1) * LANES] = oT.T.astype(BF16)


def _moba(qT, k, vT, kmean):
    b, nb, aw, blk = qT.shape
    s = nb * blk
    w = MOBA_PAIRS_PER_STEP * LANES
    nh = 2 * MOBA_PAIRS_PER_STEP
    return pl.pallas_call(
        _moba_kernel,
        grid=(b, aw // w, nb),
        in_specs=[
            pl.BlockSpec((1, 1, w, blk), lambda bi, pi, ji: (bi, ji, pi, 0)),
            pl.BlockSpec((1, nb, blk, w), lambda bi, pi, ji: (bi, 0, 0, pi)),
            pl.BlockSpec((1, nb, w, blk), lambda bi, pi, ji: (bi, 0, pi, 0)),
            pl.BlockSpec((1, nb, w), lambda bi, pi, ji: (bi, 0, pi)),
        ],
        out_specs=pl.BlockSpec((1, blk, w), lambda bi, pi, ji: (bi, ji, pi)),
        out_shape=jax.ShapeDtypeStruct((b, s, aw), BF16),
        scratch_shapes=[pltpu.VMEM((nh, LANES, blk), BF16),
                        pltpu.VMEM((nh, nb + 8, blk), F32),
                        pltpu.VMEM((nh, blk, blk), F32),
                        pltpu.VMEM((nh, 1, blk), F32),
                        pltpu.VMEM((nh, blk, blk), BF16),
                        pltpu.VMEM((nh, 1, blk), F32),
                        pltpu.VMEM((nh, 1, blk), F32),
                        pltpu.VMEM((nh, 1, blk), F32),
                        pltpu.VMEM((nh, ATTN_HEAD_DIM, blk), F32)],
        compiler_params=pltpu.CompilerParams(
            dimension_semantics=("parallel", "parallel", "parallel"), vmem_limit_bytes=VMEM_LIMIT),
        name="moba",
    )(qT, k, vT, kmean)


def _split3(x):
    hi = x.astype(BF16)
    r = x - hi.astype(F32)
    mid = r.astype(BF16)
    lo = (r - mid.astype(F32)).astype(BF16)
    return hi, mid, lo


def _mlstm_kernel(q_ref, kT_ref, v_ref, o_ref, gif_ref, gifT_ref, g_ref, out_ref, c_ref, m_ref):
    L = SEQ_TILE
    nh, hd = N_MLSTM_HEADS, MLSTM_HEAD_DIM
    c = pl.program_id(1)

    @pl.when(c == 0)
    def _():
        c_ref[...] = jnp.zeros_like(c_ref)
        m_ref[...] = jnp.zeros_like(m_ref)

    t_idx = lax.broadcasted_iota(jnp.int32, (L, L), 0)
    s_idx = lax.broadcasted_iota(jnp.int32, (L, L), 1)
    causal = s_idx <= t_idx
    tril = jnp.where(causal, 1.0, 0.0).astype(BF16)
    triu = jnp.where(t_idx <= s_idx, 1.0, 0.0).astype(BF16)

    gi_col = gif_ref[0]
    gi_row = gifT_ref[0]
    b_col_all = sum(_dot(tril, part) for part in _split3(gi_col))
    b_row_all = sum(_dot(part, triu) for part in _split3(gi_row))
    ones_col = jnp.where(lax.broadcasted_iota(jnp.int32, (L, hd), 1) == 0, 1.0, 0.0).astype(BF16)
    gn = g_ref[...]

    for h in range(nh):
        a_row = gi_row[h:h + 1, :]
        b_row = b_row_all[nh + h:nh + h + 1, :]
        b_col = b_col_all[:, nh + h:nh + h + 1]
        m_prev = m_ref[h]
        c_old = c_ref[h]

        dmat = jnp.where(causal, b_col - b_row + a_row, NEG)
        inter = b_col + m_prev
        m_t = jnp.maximum(inter, jnp.max(dmat, axis=1, keepdims=True))
        w_intra = jnp.exp(dmat - m_t)
        w_inter = jnp.exp(inter - m_t)

        q = q_ref[0, :, h * hd:(h + 1) * hd]
        kT = kT_ref[0, 0, h * hd:(h + 1) * hd, :]
        v_ext = jnp.concatenate([v_ref[0, :, h * hd:(h + 1) * hd], ones_col], axis=1)
        scores = w_intra * _dot(q, kT)
        num_ext = w_inter * _dot(q, c_old.astype(BF16)) + _dot(scores.astype(BF16), v_ext)
        num = num_ext[:, :hd]
        den = num_ext[:, hd:hd + 1]
        h_out = num / jnp.maximum(jnp.abs(den), jnp.exp(-m_t))

        b_last = b_row[:, L - 1:L]
        dec = b_last - b_row + a_row
        m_new = jnp.maximum(b_last + m_prev, jnp.max(dec, axis=1, keepdims=True))
        wk = jnp.exp(dec - m_new)
        carry_scale = jnp.exp(b_last + m_prev - m_new)
        kw = (kT.astype(F32) * wk).astype(BF16)
        c_ref[h] = carry_scale * c_old + _dot(kw, v_ext)
        m_ref[h] = m_new

        hn = _rms(h_out, gn[:, h * hd:(h + 1) * hd])
        gate_o = _sigmoid(o_ref[0, :, h * hd:(h + 1) * hd])
        out_ref[0, :, h * hd:(h + 1) * hd] = (hn * gate_o).astype(BF16)


def _mlstm(qm, kmT, vm, om, gif, gifT, g):
    b, s, mw = qm.shape
    L = SEQ_TILE
    nc = s // L
    nh2 = 2 * N_MLSTM_HEADS
    row = lambda cdim: pl.BlockSpec((1, L, cdim), lambda bi, ci: (bi, ci, 0))
    return pl.pallas_call(
        _mlstm_kernel,
        grid=(b, nc),
        in_specs=[
            row(mw),
            pl.BlockSpec((1, 1, mw, L), lambda bi, ci: (bi, ci, 0, 0)),
            row(mw), row(mw), row(nh2),
            pl.BlockSpec((1, nh2, L), lambda bi, ci: (bi, 0, ci)),
            _const_spec(g.shape),
        ],
        out_specs=row(mw),
        out_shape=jax.ShapeDtypeStruct((b, s, mw), BF16),
        scratch_shapes=[pltpu.VMEM((N_MLSTM_HEADS, MLSTM_HEAD_DIM, 2 * MLSTM_HEAD_DIM), F32),
                        pltpu.VMEM((N_MLSTM_HEADS, 1, 1), F32)],
        compiler_params=pltpu.CompilerParams(
            dimension_semantics=("parallel", "arbitrary"), vmem_limit_bytes=VMEM_LIMIT),
        name="mlstm",
    )(qm, kmT, vm, om, gif, gifT, g)


def _mix_kernel(x_ref, a_ref, y_ref, g_ref, wg_ref, wa_ref, wm_ref, wo_ref, out_ref):
    d = x_ref.shape[-1]
    x = x_ref[...]
    h = _rms(x, g_ref[...]).astype(BF16)
    gate_a = _sigmoid(_dot(h, wg_ref[:, :d]))
    gate_m = _sigmoid(_dot(h, wg_ref[:, d:]))
    merged = gate_a * _dot(a_ref[...], wa_ref[...]) + gate_m * _dot(y_ref[...], wm_ref[...])
    out_ref[...] = x + _dot(merged.astype(BF16), wo_ref[...])


def _mix(x2d, a2d, y2d, g, wg, wa, wm, wo):
    t, d = x2d.shape
    tm = MIX_TILE
    row = lambda cdim: pl.BlockSpec((tm, cdim), lambda i: (i, 0))
    return pl.pallas_call(
        _mix_kernel,
        grid=(t // tm,),
        in_specs=[row(d), row(a2d.shape[1]), row(y2d.shape[1]), _const_spec(g.shape),
                  _const_spec(wg.shape), _const_spec(wa.shape), _const_spec(wm.shape), _const_spec(wo.shape)],
        out_specs=row(d),
        out_shape=jax.ShapeDtypeStruct((t, d), F32),
        compiler_params=pltpu.CompilerParams(
            dimension_semantics=("parallel",), vmem_limit_bytes=VMEM_LIMIT),
        name="mix",
    )(x2d, a2d, y2d, g, wg, wa, wm, wo)


def _ffn_kernel(x_ref, xh_ref, g_ref, wg_ref, wu_ref, cg_ref, cu_ref, wd_ref, gf_ref, out_ref,
                hext_ref, ug_ref, uu_ref, act_ref, acc_ref):
    tm = FFN_TILE
    nchunk = wg_ref.shape[0]
    j = pl.program_id(1)
    x = x_ref[0]
    g = g_ref[...]
    hext_ref[HALO:HALO + tm, :] = _rms(x, g).astype(BF16)
    hh = _rms(xh_ref[0], g)
    hext_ref[0:HALO, :] = jnp.where(j == 0, 0.0, hh).astype(BF16)
    acc_ref[...] = x

    def stage_a(c):
        ug_ref[...] = _dot(hext_ref[...], wg_ref[c])
        uu_ref[...] = _dot(hext_ref[...], wu_ref[c])

    def conv(u_ref, cw, r0, rows):
        out = cw[0:1] * u_ref[HALO - 2 + r0:HALO - 2 + r0 + rows, :]
        out = out + cw[1:2] * u_ref[HALO - 1 + r0:HALO - 1 + r0 + rows, :]
        return out + cw[2:3] * u_ref[HALO + r0:HALO + r0 + rows, :]

    def stage_b(c):
        cg = cg_ref[c]
        cu = cu_ref[c]
        for r0 in range(0, tm, FFN_ROWS):
            gate = conv(ug_ref, cg, r0, FFN_ROWS)
            up = conv(uu_ref, cu, r0, FFN_ROWS)
            act_ref[r0:r0 + FFN_ROWS, :] = (gate * _sigmoid(gate) * up).astype(BF16)

    def stage_c(c):
        acc_ref[...] += _dot(act_ref[...], wd_ref[c])

    stage_a(0)
    stage_b(0)
    stage_a(1)

    def body(c, carry):
        stage_c(c - 2)
        stage_b(c - 1)
        stage_a(c)
        return carry

    lax.fori_loop(2, nchunk, body, 0)
    stage_c(nchunk - 2)
    stage_b(nchunk - 1)
    stage_c(nchunk - 1)
    out_ref[0] = _rms(acc_ref[...], gf_ref[...])


def _ffn(x1, g2, wg, wu, cg, cu, wd, gf):
    b, s, d = x1.shape
    tm = FFN_TILE
    return pl.pallas_call(
        _ffn_kernel,
        grid=(b, s // tm),
        in_specs=[
            pl.BlockSpec((1, tm, d), lambda bi, ji: (bi, ji, 0)),
            pl.BlockSpec((1, HALO, d), lambda bi, ji: (bi, jnp.maximum(ji * (tm // HALO) - 1, 0), 0)),
            _const_spec(g2.shape), _const_spec(wg.shape), _const_spec(wu.shape),
            _const_spec(cg.shape), _const_spec(cu.shape), _const_spec(wd.shape), _const_spec(gf.shape),
        ],
        out_specs=pl.BlockSpec((1, tm, d), lambda bi, ji: (bi, ji, 0)),
        out_shape=jax.ShapeDtypeStruct((b, s, d), F32),
        scratch_shapes=[pltpu.VMEM((HALO + tm, d), BF16),
                        pltpu.VMEM((HALO + tm, FFN_CHUNK), F32),
                        pltpu.VMEM((HALO + tm, FFN_CHUNK), F32),
                        pltpu.VMEM((tm, FFN_CHUNK), BF16),
                        pltpu.VMEM((tm, d), F32)],
        compiler_params=pltpu.CompilerParams(
            dimension_semantics=("parallel", "parallel"), vmem_limit_bytes=VMEM_LIMIT),
        name="ffn",
    )(x1, x1, g2, wg, wu, cg, cu, wd, gf)


def _chunk_cols(w, chunk):
    kdim, n = w.shape
    return w.reshape(kdim, n // chunk, chunk).transpose(1, 0, 2)


def kernel(x, positions, norm_mix_g, w_in, conv_mlstm, i_bias, f_bias, mlstm_norm_g, w_branch_attn,
           w_branch_mlstm, w_out, norm_ffn_g, w_up, conv_ffn, w_down, norm_final_g):
    b, s, d = x.shape
    aw, mw, nh = ATTN_WIDTH, MLSTM_WIDTH, N_MLSTM_HEADS
    assert s % FFN_TILE == 0 and s % SEQ_TILE == 0 and d % LANES == 0
    n_qkv = 3 * aw + 4 * mw
    d_ff = w_down.shape[1]
    assert d_ff % FFN_CHUNK == 0
    half = ATTN_HEAD_DIM // 2
    inv_freq = ROPE_THETA ** (-(jnp.arange(half, dtype=F32) / half))
    invf = jnp.tile(inv_freq, LANES // half).reshape(1, LANES)

    for layer in range(w_in.shape[0]):
        wl = w_in[layer]
        w1 = jnp.concatenate(
            [wl[:, :n_qkv + 2 * nh], jnp.zeros((d, LANES - 2 * nh), wl.dtype)], axis=1).astype(BF16)
        wg = wl[:, n_qkv + 2 * nh:].astype(BF16)
        bias = jnp.concatenate(
            [i_bias[layer], f_bias[layer], jnp.zeros((LANES - 2 * nh,), F32)]).reshape(1, LANES)

        qT, k, kmean, vT, qm, kmT, vm, om, gif, gifT = _in_proj(
            x, positions, invf, norm_mix_g[layer].reshape(1, d), w1, conv_mlstm[layer], bias)
        attn = _moba(qT, k, vT, kmean.reshape(b, s // MOBA_BLOCK, aw))
        y_m = _mlstm(qm, kmT, vm, om, gif, gifT, mlstm_norm_g[layer].reshape(1, mw))
        x1 = _mix(x.reshape(b * s, d), attn.reshape(b * s, aw), y_m.reshape(b * s, mw),
                  norm_mix_g[layer].reshape(1, d), wg, w_branch_attn[layer].astype(BF16),
                  w_branch_mlstm[layer].astype(BF16), w_out[layer].astype(BF16)).reshape(b, s, d)

        wu = w_up[layer].astype(BF16)
        cf = conv_ffn[layer]
        last = layer == w_in.shape[0] - 1
        gfin = norm_final_g.reshape(1, d) if last else None
        assert last, "only the final layer fuses the output norm"
        x = _ffn(x1, norm_ffn_g[layer].reshape(1, d),
                 _chunk_cols(wu[:, :d_ff], FFN_CHUNK), _chunk_cols(wu[:, d_ff:], FFN_CHUNK),
                 _chunk_cols(cf[:, :d_ff], FFN_CHUNK), _chunk_cols(cf[:, d_ff:], FFN_CHUNK),
                 w_down[layer].astype(BF16).reshape(d_ff // FFN_CHUNK, FFN_CHUNK, d), gfin)
    return x
```

```python
import functools
import math

import jax
import jax.numpy as jnp
from jax import lax
from jax.experimental import pallas as pl
from jax.experimental.pallas import tpu as pltpu

F32 = jnp.float32
BF16 = jnp.bfloat16

EPS = 1e-6
NEG = -1e30
ROPE_THETA = 10000.0

N_ATTN_HEADS = 8
ATTN_HEAD_DIM = 64
ATTN_WIDTH = N_ATTN_HEADS * ATTN_HEAD_DIM
MOBA_BLOCK = 256
MOBA_TOPK = 3
N_MLSTM_HEADS = 4
MLSTM_HEAD_DIM = 128
MLSTM_WIDTH = N_MLSTM_HEADS * MLSTM_HEAD_DIM

LANES = 128
HALO = 16
SEQ_TILE = MOBA_BLOCK
MIX_TILE = 512
FFN_TILE = 512
FFN_CHUNK = 256
FFN_ROWS = 64
MOBA_PAIRS_PER_STEP = 2
LOG2_E = math.log2(math.e)
VMEM_LIMIT = 56 * 1024 * 1024


def _dot(a, b):
    return jnp.dot(a, b, preferred_element_type=F32)


def _rms(x, g):
    ms = jnp.mean(x * x, axis=-1, keepdims=True)
    return x * lax.rsqrt(ms + EPS) * g


def _sigmoid(x):
    return 1.0 / (1.0 + jnp.exp(-x))


def _const_spec(shape):
    nd = len(shape)
    return pl.BlockSpec(shape, lambda *_: (0,) * nd, pipeline_mode=pl.Buffered(1))


def _in_proj_kernel(x_ref, xh_ref, pos_ref, invf_ref, g_ref, w_ref, cw_ref, bias_ref,
                    qT_ref, k_ref, kmean_ref, vT_ref, qm_ref, kmT_ref, vm_ref, om_ref, gif_ref, gifT_ref,
                    ext_ref):
    tm = SEQ_TILE
    aw, mw = ATTN_WIDTH, MLSTM_WIDTH
    j = pl.program_id(1)
    g = g_ref[...]
    h = _rms(x_ref[0], g).astype(BF16)

    ang = pos_ref[0].astype(F32) * invf_ref[...]
    cos = jnp.cos(ang)
    sin = jnp.sin(ang)
    lane = lax.broadcasted_iota(jnp.int32, (tm, LANES), 1)
    first_half = (lane % ATTN_HEAD_DIM) < (ATTN_HEAD_DIM // 2)
    sin_signed = jnp.where(first_half, -sin, sin)

    def rope(t):
        outs = []
        for c in range(aw // LANES):
            tg = t[:, c * LANES:(c + 1) * LANES]
            swapped = jnp.where(first_half, pltpu.roll(tg, LANES - 32, 1), pltpu.roll(tg, 32, 1))
            outs.append(tg * cos + swapped * sin_signed)
        return jnp.concatenate(outs, axis=1)

    qk = _dot(h, w_ref[:, 0:2 * aw])
    q = rope(qk[:, :aw]) * (LOG2_E / math.sqrt(ATTN_HEAD_DIM))
    k = rope(qk[:, aw:])
    qT_ref[0, 0] = q.T.astype(BF16)
    k_ref[0, 0] = k.astype(BF16)
    kmean_ref[0] = jnp.mean(k, axis=0, keepdims=True)
    v = _dot(h, w_ref[:, 2 * aw:3 * aw])
    vT_ref[0, 0] = v.T.astype(BF16)

    c0 = 3 * aw
    pm = _dot(h, w_ref[:, c0:c0 + 2 * mw])
    hh = _rms(xh_ref[0], g).astype(BF16)
    ph = _dot(hh, w_ref[:, c0:c0 + 2 * mw])
    ext_ref[0:HALO, :] = jnp.where(j == 0, 0.0, ph)
    ext_ref[HALO:HALO + tm, :] = pm
    cw = cw_ref[...]
    conv = cw[0:1] * ext_ref[HALO - 3:HALO - 3 + tm, :]
    conv = conv + cw[1:2] * ext_ref[HALO - 2:HALO - 2 + tm, :]
    conv = conv + cw[2:3] * ext_ref[HALO - 1:HALO - 1 + tm, :]
    conv = conv + cw[3:4] * pm
    act = conv * _sigmoid(conv)
    qm_ref[0] = (act[:, :mw] * (1.0 / math.sqrt(MLSTM_HEAD_DIM))).astype(BF16)
    kmT_ref[0, 0] = act[:, mw:].T.astype(BF16)

    c1 = c0 + 2 * mw
    vo = _dot(h, w_ref[:, c1:c1 + 2 * mw])
    vm_ref[0] = vo[:, :mw].astype(BF16)
    om_ref[0] = vo[:, mw:]

    c2 = c1 + 2 * mw
    gi = _dot(h, w_ref[:, c2:c2 + LANES]) + bias_ref[...]
    log_sig = jnp.minimum(gi, 0.0) - jnp.log1p(jnp.exp(-jnp.abs(gi)))
    gt = jnp.where(lane < N_MLSTM_HEADS, gi, log_sig)
    gif_ref[0] = gt[:, 0:2 * N_MLSTM_HEADS]
    gifT_ref[0] = gt.T[0:2 * N_MLSTM_HEADS, :]


def _in_proj(x, positions, invf, g, w, cw, bias):
    b, s, d = x.shape
    tm = SEQ_TILE
    nb = s // tm
    aw, mw = ATTN_WIDTH, MLSTM_WIDTH
    nh2 = 2 * N_MLSTM_HEADS
    out_shape = [
        jax.ShapeDtypeStruct((b, nb, aw, tm), BF16),
        jax.ShapeDtypeStruct((b, nb, tm, aw), BF16),
        jax.ShapeDtypeStruct((b * nb, 1, aw), F32),
        jax.ShapeDtypeStruct((b, nb, aw, tm), BF16),
        jax.ShapeDtypeStruct((b, s, mw), BF16),
        jax.ShapeDtypeStruct((b, nb, mw, tm), BF16),
        jax.ShapeDtypeStruct((b, s, mw), BF16),
        jax.ShapeDtypeStruct((b, s, mw), F32),
        jax.ShapeDtypeStruct((b, s, nh2), F32),
        jax.ShapeDtypeStruct((b, nh2, s), F32),
    ]
    blk4 = lambda r, c: pl.BlockSpec((1, 1, r, c), lambda bi, ji: (bi, ji, 0, 0))
    row3 = lambda c: pl.BlockSpec((1, tm, c), lambda bi, ji: (bi, ji, 0))
    out_specs = [
        blk4(aw, tm), blk4(tm, aw),
        pl.BlockSpec((1, 1, aw), lambda bi, ji: (bi * nb + ji, 0, 0)),
        blk4(aw, tm), row3(mw), blk4(mw, tm), row3(mw), row3(mw), row3(nh2),
        pl.BlockSpec((1, nh2, tm), lambda bi, ji: (bi, 0, ji)),
    ]
    in_specs = [
        pl.BlockSpec((1, tm, d), lambda bi, ji: (bi, ji, 0)),
        pl.BlockSpec((1, HALO, d), lambda bi, ji: (bi, jnp.maximum(ji * (tm // HALO) - 1, 0), 0)),
        pl.BlockSpec((1, tm, 1), lambda bi, ji: (bi, ji, 0)),
        _const_spec(invf.shape), _const_spec(g.shape), _const_spec(w.shape),
        _const_spec(cw.shape), _const_spec(bias.shape),
    ]
    return pl.pallas_call(
        _in_proj_kernel,
        grid=(b, nb),
        in_specs=in_specs,
        out_specs=out_specs,
        out_shape=out_shape,
        scratch_shapes=[pltpu.VMEM((HALO + tm, 2 * mw), F32)],
        compiler_params=pltpu.CompilerParams(
            dimension_semantics=("parallel", "parallel"), vmem_limit_bytes=VMEM_LIMIT),
        name="in_proj",
    )(x, x, positions.reshape(b, s, 1), invf, g, w, cw, bias)


def _moba_kernel(jtab_ref, ntab_ref, qT_ref, k_ref, vT_ref, kmean_ref, out_ref,
                 qm_ref, bias_ref, m_ref, l_ref, acc_ref, s_ref, cm_ref, p_ref, a_ref):
    blk = MOBA_BLOCK
    hd = ATTN_HEAD_DIM
    nb = k_ref.shape[1]
    nh = 2 * (qT_ref.shape[2] // LANES)
    n_units = nb * (nb - 1) // 2
    dim_row = lax.broadcasted_iota(jnp.int32, (LANES, blk), 0)
    blk_id = lax.broadcasted_iota(jnp.int32, (nb, blk), 0)
    key_row = lax.broadcasted_iota(jnp.int32, (blk, blk), 0)
    qry_col = lax.broadcasted_iota(jnp.int32, (blk, blk), 1)

    def lanes(h):
        return slice((h // 2) * LANES, (h // 2 + 1) * LANES)

    def head_rows(h):
        return slice(h * hd, (h + 1) * hd)

    def phase_a(j, carry):
        for h in range(nh):
            qT = qT_ref[0, j, lanes(h), :]
            qh = jnp.where((dim_row >= hd) == (h % 2 == 1), qT, jnp.zeros_like(qT))
            qm_ref[h, j] = qh
            km = kmean_ref[0, :, lanes(h)]
            km_hi = km.astype(BF16)
            km_lo = (km - km_hi.astype(F32)).astype(BF16)
            gate = _dot(km_hi, qh) + _dot(km_lo, qh)
            gate = jnp.where(blk_id < j, gate, NEG)
            rank = jnp.zeros((nb, blk), jnp.int32)
            for m in range(nb):
                gm = gate[m:m + 1, :]
                ahead = (gm > gate) | ((gm == gate) & (m < blk_id))
                rank = rank + ahead.astype(jnp.int32)
            selected = (rank < MOBA_TOPK) & (blk_id < j)
            bias_ref[h, j, 0:nb, :] = jnp.where(selected, 0.0, NEG)
            bias_ref[h, j, nb:nb + 1, :] = jnp.full((1, blk), NEG, F32)
            s_own = jnp.where(key_row <= qry_col, _dot(k_ref[0, j, :, lanes(h)], qh), NEG)
            m0 = jnp.max(s_own, axis=0, keepdims=True)
            p0 = jnp.exp2(s_own - m0)
            m_ref[h, j] = m0
            l_ref[h, j] = jnp.sum(p0, axis=0, keepdims=True)
            acc_ref[h, j] = _dot(vT_ref[0, j, head_rows(h), :], p0.astype(BF16))
        return carry

    lax.fori_loop(0, nb, phase_a, 0)

    for h in range(nh):
        s_ref[h] = jnp.zeros((blk, blk), F32)
        cm_ref[h] = jnp.zeros((1, blk), F32)
        p_ref[h] = jnp.zeros((blk, blk), BF16)
        a_ref[h] = jnp.ones((1, blk), F32)

    def phase_b(i, carry):
        j1, n1 = jtab_ref[i + 2], ntab_ref[i + 2]
        j2, n2 = jtab_ref[i + 1], ntab_ref[i + 1]
        j3, n3 = jtab_ref[i], ntab_ref[i]
        k_blk = jnp.minimum(n1, nb - 1)
        v_blk = jnp.minimum(n3, nb - 1)
        for h in range(nh):
            vTb = vT_ref[0, v_blk, head_rows(h), :]
            acc_ref[h, j3] = a_ref[h] * acc_ref[h, j3] + _dot(vTb, p_ref[h])
        for h in range(nh):
            bias = bias_ref[h, j2, pl.ds(n2, 1), :]
            m_old = m_ref[h, j2]
            m_new = jnp.maximum(m_old, cm_ref[h] + bias)
            alpha = jnp.exp2(m_old - m_new)
            p = jnp.exp2(s_ref[h] - (m_new - bias))
            l_ref[h, j2] = alpha * l_ref[h, j2] + jnp.sum(p, axis=0, keepdims=True)
            m_ref[h, j2] = m_new
            a_ref[h] = alpha
            p_ref[h] = p.astype(BF16)
        for h in range(nh):
            sT = _dot(k_ref[0, k_blk, :, lanes(h)], qm_ref[h, j1])
            s_ref[h] = sT
            cm_ref[h] = jnp.max(sT, axis=0, keepdims=True)
        return carry

    lax.fori_loop(0, n_units + 2, phase_b, 0)

    def phase_c(j, carry):
        row0 = pl.multiple_of(j * blk, blk)
        for pair in range(nh // 2):
            oT = jnp.concatenate([acc_ref[h, j] / l_ref[h, j] for h in (2 * pair, 2 * pair + 1)], axis=0)
            out_ref[0, pl.ds(row0, blk), pair * LANES:(pair + 1) * LANES] = oT.T.astype(BF16)
        return carry

    lax.fori_loop(0, nb, phase_c, 0)


def _moba_units(nb):
    pad = (0, nb)
    units = [pad, pad] + [(j, n) for j in range(nb) for n in range(j)] + [pad, pad]
    jtab = jnp.asarray([u[0] for u in units], jnp.int32)
    ntab = jnp.asarray([u[1] for u in units], jnp.int32)
    return jtab, ntab


def _moba(qT, k, vT, kmean):
    b, nb, aw, blk = qT.shape
    s = nb * blk
    w = MOBA_PAIRS_PER_STEP * LANES
    nh = 2 * MOBA_PAIRS_PER_STEP
    jtab, ntab = _moba_units(nb)
    grid_spec = pltpu.PrefetchScalarGridSpec(
        num_scalar_prefetch=2,
        grid=(b, aw // w),
        in_specs=[
            pl.BlockSpec((1, nb, w, blk), lambda bi, pi, *_: (bi, 0, pi, 0)),
            pl.BlockSpec((1, nb, blk, w), lambda bi, pi, *_: (bi, 0, 0, pi)),
            pl.BlockSpec((1, nb, w, blk), lambda bi, pi, *_: (bi, 0, pi, 0)),
            pl.BlockSpec((1, nb, w), lambda bi, pi, *_: (bi, 0, pi)),
        ],
        out_specs=pl.BlockSpec((1, s, w), lambda bi, pi, *_: (bi, 0, pi)),
        scratch_shapes=[pltpu.VMEM((nh, nb, LANES, blk), BF16),
                        pltpu.VMEM((nh, nb, nb + 8, blk), F32),
                        pltpu.VMEM((nh, nb, 1, blk), F32),
                        pltpu.VMEM((nh, nb, 1, blk), F32),
                        pltpu.VMEM((nh, nb, ATTN_HEAD_DIM, blk), F32),
                        pltpu.VMEM((nh, blk, blk), F32),
                        pltpu.VMEM((nh, 1, blk), F32),
                        pltpu.VMEM((nh, blk, blk), BF16),
                        pltpu.VMEM((nh, 1, blk), F32)],
    )
    return pl.pallas_call(
        _moba_kernel,
        grid_spec=grid_spec,
        out_shape=jax.ShapeDtypeStruct((b, s, aw), BF16),
        compiler_params=pltpu.CompilerParams(
            dimension_semantics=("parallel", "parallel"), vmem_limit_bytes=VMEM_LIMIT),
        name="moba",
    )(jtab, ntab, qT, k, vT, kmean)


def _split3(x):
    hi = x.astype(BF16)
    r = x - hi.astype(F32)
    mid = r.astype(BF16)
    lo = (r - mid.astype(F32)).astype(BF16)
    return hi, mid, lo


def _mlstm_kernel(q_ref, kT_ref, v_ref, o_ref, gif_ref, gifT_ref, g_ref, out_ref, c_ref, m_ref):
    L = SEQ_TILE
    nh, hd = N_MLSTM_HEADS, MLSTM_HEAD_DIM
    c = pl.program_id(1)

    @pl.when(c == 0)
    def _():
        c_ref[...] = jnp.zeros_like(c_ref)
        m_ref[...] = jnp.zeros_like(m_ref)

    t_idx = lax.broadcasted_iota(jnp.int32, (L, L), 0)
    s_idx = lax.broadcasted_iota(jnp.int32, (L, L), 1)
    causal = s_idx <= t_idx
    tril = jnp.where(causal, 1.0, 0.0).astype(BF16)
    triu = jnp.where(t_idx <= s_idx, 1.0, 0.0).astype(BF16)

    gi_col = gif_ref[0]
    gi_row = gifT_ref[0]
    b_col_all = sum(_dot(tril, part) for part in _split3(gi_col))
    b_row_all = sum(_dot(part, triu) for part in _split3(gi_row))
    ones_col = jnp.where(lax.broadcasted_iota(jnp.int32, (L, hd), 1) == 0, 1.0, 0.0).astype(BF16)
    gn = g_ref[...]

    for h in range(nh):
        a_row = gi_row[h:h + 1, :]
        b_row = b_row_all[nh + h:nh + h + 1, :]
        b_col = b_col_all[:, nh + h:nh + h + 1]
        q = q_ref[0, :, h * hd:(h + 1) * hd]
        kT = kT_ref[0, 0, h * hd:(h + 1) * hd, :]
        v_ext = jnp.concatenate([v_ref[0, :, h * hd:(h + 1) * hd], ones_col], axis=1)

        dmat = jnp.where(causal, b_col - b_row + a_row, NEG)
        mi = jnp.max(dmat, axis=1, keepdims=True)
        intra = _dot((jnp.exp(dmat - mi) * _dot(q, kT)).astype(BF16), v_ext)
        b_last = b_row[:, L - 1:L]
        dec = b_last - b_row + a_row
        md = jnp.max(dec, axis=1, keepdims=True)
        kv = _dot((kT.astype(F32) * jnp.exp(dec - md)).astype(BF16), v_ext)

        m_prev = m_ref[h]
        c_old = c_ref[h]
        inter = b_col + m_prev
        m_t = jnp.maximum(inter, mi)
        num_ext = jnp.exp(inter - m_t) * _dot(q, c_old.astype(BF16)) + jnp.exp(mi - m_t) * intra
        num = num_ext[:, :hd]
        den = num_ext[:, hd:hd + 1]
        h_out = num / jnp.maximum(jnp.abs(den), jnp.exp(-m_t))

        m_new = jnp.maximum(b_last + m_prev, md)
        c_ref[h] = jnp.exp(b_last + m_prev - m_new) * c_old + jnp.exp(md - m_new) * kv
        m_ref[h] = m_new

        hn = _rms(h_out, gn[:, h * hd:(h + 1) * hd])
        gate_o = _sigmoid(o_ref[0, :, h * hd:(h + 1) * hd])
        out_ref[0, :, h * hd:(h + 1) * hd] = (hn * gate_o).astype(BF16)


def _mlstm(qm, kmT, vm, om, gif, gifT, g):
    b, s, mw = qm.shape
    L = SEQ_TILE
    nc = s // L
    nh2 = 2 * N_MLSTM_HEADS
    row = lambda cdim: pl.BlockSpec((1, L, cdim), lambda bi, ci: (bi, ci, 0))
    return pl.pallas_call(
        _mlstm_kernel,
        grid=(b, nc),
        in_specs=[
            row(mw),
            pl.BlockSpec((1, 1, mw, L), lambda bi, ci: (bi, ci, 0, 0)),
            row(mw), row(mw), row(nh2),
            pl.BlockSpec((1, nh2, L), lambda bi, ci: (bi, 0, ci)),
            _const_spec(g.shape),
        ],
        out_specs=row(mw),
        out_shape=jax.ShapeDtypeStruct((b, s, mw), BF16),
        scratch_shapes=[pltpu.VMEM((N_MLSTM_HEADS, MLSTM_HEAD_DIM, 2 * MLSTM_HEAD_DIM), F32),
                        pltpu.VMEM((N_MLSTM_HEADS, 1, 1), F32)],
        compiler_params=pltpu.CompilerParams(
            dimension_semantics=("parallel", "arbitrary"), vmem_limit_bytes=VMEM_LIMIT),
        name="mlstm",
    )(qm, kmT, vm, om, gif, gifT, g)


def _mix_kernel(x_ref, a_ref, y_ref, g_ref, wg_ref, wa_ref, wm_ref, wo_ref, out_ref):
    d = x_ref.shape[-1]
    x = x_ref[...]
    h = _rms(x, g_ref[...]).astype(BF16)
    gate_a = _sigmoid(_dot(h, wg_ref[:, :d]))
    gate_m = _sigmoid(_dot(h, wg_ref[:, d:]))
    merged = gate_a * _dot(a_ref[...], wa_ref[...]) + gate_m * _dot(y_ref[...], wm_ref[...])
    out_ref[...] = x + _dot(merged.astype(BF16), wo_ref[...])


def _mix(x2d, a2d, y2d, g, wg, wa, wm, wo):
    t, d = x2d.shape
    tm = MIX_TILE
    row = lambda cdim: pl.BlockSpec((tm, cdim), lambda i: (i, 0))
    return pl.pallas_call(
        _mix_kernel,
        grid=(t // tm,),
        in_specs=[row(d), row(a2d.shape[1]), row(y2d.shape[1]), _const_spec(g.shape),
                  _const_spec(wg.shape), _const_spec(wa.shape), _const_spec(wm.shape), _const_spec(wo.shape)],
        out_specs=row(d),
        out_shape=jax.ShapeDtypeStruct((t, d), F32),
        compiler_params=pltpu.CompilerParams(
            dimension_semantics=("parallel",), vmem_limit_bytes=VMEM_LIMIT),
        name="mix",
    )(x2d, a2d, y2d, g, wg, wa, wm, wo)


def _ffn_kernel(x_ref, xh_ref, g_ref, wg_ref, wu_ref, cg_ref, cu_ref, wd_ref, gf_ref, out_ref,
                hext_ref, ug_ref, uu_ref, act_ref, acc_ref):
    tm = FFN_TILE
    nchunk = wg_ref.shape[0]
    j = pl.program_id(1)
    x = x_ref[0]
    g = g_ref[...]
    hext_ref[HALO:HALO + tm, :] = _rms(x, g).astype(BF16)
    hh = _rms(xh_ref[0], g)
    hext_ref[0:HALO, :] = jnp.where(j == 0, 0.0, hh).astype(BF16)
    acc_ref[...] = x

    def stage_a(c):
        ug_ref[...] = _dot(hext_ref[...], wg_ref[c])
        uu_ref[...] = _dot(hext_ref[...], wu_ref[c])

    def conv(u_ref, cw, r0, rows):
        out = cw[0:1] * u_ref[HALO - 2 + r0:HALO - 2 + r0 + rows, :]
        out = out + cw[1:2] * u_ref[HALO - 1 + r0:HALO - 1 + r0 + rows, :]
        return out + cw[2:3] * u_ref[HALO + r0:HALO + r0 + rows, :]

    def stage_b(c):
        cg = cg_ref[c]
        cu = cu_ref[c]
        for r0 in range(0, tm, FFN_ROWS):
            gate = conv(ug_ref, cg, r0, FFN_ROWS)
            up = conv(uu_ref, cu, r0, FFN_ROWS)
            act_ref[r0:r0 + FFN_ROWS, :] = (gate * _sigmoid(gate) * up).astype(BF16)

    def stage_c(c):
        acc_ref[...] += _dot(act_ref[...], wd_ref[c])

    stage_a(0)
    stage_b(0)
    stage_a(1)

    def body(c, carry):
        stage_c(c - 2)
        stage_b(c - 1)
        stage_a(c)
        return carry

    lax.fori_loop(2, nchunk, body, 0)
    stage_c(nchunk - 2)
    stage_b(nchunk - 1)
    stage_c(nchunk - 1)
    out_ref[0] = _rms(acc_ref[...], gf_ref[...])


def _ffn(x1, g2, wg, wu, cg, cu, wd, gf):
    b, s, d = x1.shape
    tm = FFN_TILE
    return pl.pallas_call(
        _ffn_kernel,
        grid=(b, s // tm),
        in_specs=[
            pl.BlockSpec((1, tm, d), lambda bi, ji: (bi, ji, 0)),
            pl.BlockSpec((1, HALO, d), lambda bi, ji: (bi, jnp.maximum(ji * (tm // HALO) - 1, 0), 0)),
            _const_spec(g2.shape), _const_spec(wg.shape), _const_spec(wu.shape),
            _const_spec(cg.shape), _const_spec(cu.shape), _const_spec(wd.shape), _const_spec(gf.shape),
        ],
        out_specs=pl.BlockSpec((1, tm, d), lambda bi, ji: (bi, ji, 0)),
        out_shape=jax.ShapeDtypeStruct((b, s, d), F32),
        scratch_shapes=[pltpu.VMEM((HALO + tm, d), BF16),
                        pltpu.VMEM((HALO + tm, FFN_CHUNK), F32),
                        pltpu.VMEM((HALO + tm, FFN_CHUNK), F32),
                        pltpu.VMEM((tm, FFN_CHUNK), BF16),
                        pltpu.VMEM((tm, d), F32)],
        compiler_params=pltpu.CompilerParams(
            dimension_semantics=("parallel", "parallel"), vmem_limit_bytes=VMEM_LIMIT),
        name="ffn",
    )(x1, x1, g2, wg, wu, cg, cu, wd, gf)


def _chunk_cols(w, chunk):
    kdim, n = w.shape
    return w.reshape(kdim, n // chunk, chunk).transpose(1, 0, 2)


def kernel(x, positions, norm_mix_g, w_in, conv_mlstm, i_bias, f_bias, mlstm_norm_g, w_branch_attn,
           w_branch_mlstm, w_out, norm_ffn_g, w_up, conv_ffn, w_down, norm_final_g):
    b, s, d = x.shape
    aw, mw, nh = ATTN_WIDTH, MLSTM_WIDTH, N_MLSTM_HEADS
    assert s % FFN_TILE == 0 and s % SEQ_TILE == 0 and d % LANES == 0
    n_qkv = 3 * aw + 4 * mw
    d_ff = w_down.shape[1]
    assert d_ff % FFN_CHUNK == 0
    half = ATTN_HEAD_DIM // 2
    inv_freq = ROPE_THETA ** (-(jnp.arange(half, dtype=F32) / half))
    invf = jnp.tile(inv_freq, LANES // half).reshape(1, LANES)

    for layer in range(w_in.shape[0]):
        wl = w_in[layer]
        w1 = jnp.concatenate(
            [wl[:, :n_qkv + 2 * nh], jnp.zeros((d, LANES - 2 * nh), wl.dtype)], axis=1).astype(BF16)
        wg = wl[:, n_qkv + 2 * nh:].astype(BF16)
        bias = jnp.concatenate(
            [i_bias[layer], f_bias[layer], jnp.zeros((LANES - 2 * nh,), F32)]).reshape(1, LANES)

        qT, k, kmean, vT, qm, kmT, vm, om, gif, gifT = _in_proj(
            x, positions, invf, norm_mix_g[layer].reshape(1, d), w1, conv_mlstm[layer], bias)
        attn = _moba(qT, k, vT, kmean.reshape(b, s // MOBA_BLOCK, aw))
        y_m = _mlstm(qm, kmT, vm, om, gif, gifT, mlstm_norm_g[layer].reshape(1, mw))
        x1 = _mix(x.reshape(b * s, d), attn.reshape(b * s, aw), y_m.reshape(b * s, mw),
                  norm_mix_g[layer].reshape(1, d), wg, w_branch_attn[layer].astype(BF16),
                  w_branch_mlstm[layer].astype(BF16), w_out[layer].astype(BF16)).reshape(b, s, d)

        wu = w_up[layer].astype(BF16)
        cf = conv_ffn[layer]
        last = layer == w_in.shape[0] - 1
        gfin = norm_final_g.reshape(1, d) if last else None
        assert last, "only the final layer fuses the output norm"
        x = _ffn(x1, norm_ffn_g[layer].reshape(1, d),
                 _chunk_cols(wu[:, :d_ff], FFN_CHUNK), _chunk_cols(wu[:, d_ff:], FFN_CHUNK),
                 _chunk_cols(cf[:, :d_ff], FFN_CHUNK), _chunk_cols(cf[:, d_ff:], FFN_CHUNK),
                 w_down[layer].astype(BF16).reshape(d_ff // FFN_CHUNK, FFN_CHUNK, d), gfin)
    return x
```

```python
import functools
import math

import jax
import jax.numpy as jnp
from jax import lax
from jax.experimental import pallas as pl
from jax.experimental.pallas import tpu as pltpu

F32 = jnp.float32
BF16 = jnp.bfloat16

EPS = 1e-6
NEG = -1e30
ROPE_THETA = 10000.0

N_ATTN_HEADS = 8
ATTN_HEAD_DIM = 64
ATTN_WIDTH = N_ATTN_HEADS * ATTN_HEAD_DIM
MOBA_BLOCK = 256
MOBA_TOPK = 3
N_MLSTM_HEADS = 4
MLSTM_HEAD_DIM = 128
MLSTM_WIDTH = N_MLSTM_HEADS * MLSTM_HEAD_DIM

LANES = 128
HALO = 16
SEQ_TILE = MOBA_BLOCK
MIX_TILE = 512
FFN_TILE = 512
FFN_CHUNK = 256
FFN_ROWS = 64
MOBA_PAIRS_PER_STEP = 4
LOG2_E = math.log2(math.e)
VMEM_LIMIT = 56 * 1024 * 1024


def _dot(a, b):
    return jnp.dot(a, b, preferred_element_type=F32)


def _rms(x, g):
    ms = jnp.mean(x * x, axis=-1, keepdims=True)
    return x * lax.rsqrt(ms + EPS) * g


def _sigmoid(x):
    return 1.0 / (1.0 + jnp.exp(-x))


def _const_spec(shape):
    nd = len(shape)
    return pl.BlockSpec(shape, lambda *_: (0,) * nd, pipeline_mode=pl.Buffered(1))


def _in_proj_kernel(x_ref, xh_ref, pos_ref, invf_ref, g_ref, w_ref, cw_ref, bias_ref,
                    qT_ref, k_ref, kmean_ref, vT_ref, qm_ref, kmT_ref, vm_ref, om_ref, gif_ref, gifT_ref,
                    ext_ref):
    tm = SEQ_TILE
    aw, mw = ATTN_WIDTH, MLSTM_WIDTH
    j = pl.program_id(1)
    g = g_ref[...]
    h = _rms(x_ref[0], g).astype(BF16)

    ang = pos_ref[0].astype(F32) * invf_ref[...]
    cos = jnp.cos(ang)
    sin = jnp.sin(ang)
    lane = lax.broadcasted_iota(jnp.int32, (tm, LANES), 1)
    first_half = (lane % ATTN_HEAD_DIM) < (ATTN_HEAD_DIM // 2)
    sin_signed = jnp.where(first_half, -sin, sin)

    def rope(t):
        outs = []
        for c in range(aw // LANES):
            tg = t[:, c * LANES:(c + 1) * LANES]
            swapped = jnp.where(first_half, pltpu.roll(tg, LANES - 32, 1), pltpu.roll(tg, 32, 1))
            outs.append(tg * cos + swapped * sin_signed)
        return jnp.concatenate(outs, axis=1)

    qk = _dot(h, w_ref[:, 0:2 * aw])
    q = rope(qk[:, :aw]) * (LOG2_E / math.sqrt(ATTN_HEAD_DIM))
    k = rope(qk[:, aw:])
    qT_ref[0, 0] = q.T.astype(BF16)
    k_ref[0, 0] = k.astype(BF16)
    kmean_ref[0] = jnp.mean(k, axis=0, keepdims=True)
    v = _dot(h, w_ref[:, 2 * aw:3 * aw])
    vT_ref[0, 0] = v.T.astype(BF16)

    c0 = 3 * aw
    pm = _dot(h, w_ref[:, c0:c0 + 2 * mw])
    hh = _rms(xh_ref[0], g).astype(BF16)
    ph = _dot(hh, w_ref[:, c0:c0 + 2 * mw])
    ext_ref[0:HALO, :] = jnp.where(j == 0, 0.0, ph)
    ext_ref[HALO:HALO + tm, :] = pm
    cw = cw_ref[...]
    conv = cw[0:1] * ext_ref[HALO - 3:HALO - 3 + tm, :]
    conv = conv + cw[1:2] * ext_ref[HALO - 2:HALO - 2 + tm, :]
    conv = conv + cw[2:3] * ext_ref[HALO - 1:HALO - 1 + tm, :]
    conv = conv + cw[3:4] * pm
    act = conv * _sigmoid(conv)
    qm_ref[0] = (act[:, :mw] * (1.0 / math.sqrt(MLSTM_HEAD_DIM))).astype(BF16)
    kmT_ref[0, 0] = act[:, mw:].T.astype(BF16)

    c1 = c0 + 2 * mw
    vo = _dot(h, w_ref[:, c1:c1 + 2 * mw])
    vm_ref[0] = vo[:, :mw].astype(BF16)
    om_ref[0] = vo[:, mw:]

    c2 = c1 + 2 * mw
    gi = _dot(h, w_ref[:, c2:c2 + LANES]) + bias_ref[...]
    log_sig = jnp.minimum(gi, 0.0) - jnp.log1p(jnp.exp(-jnp.abs(gi)))
    gt = jnp.where(lane < N_MLSTM_HEADS, gi, log_sig)
    gif_ref[0] = gt[:, 0:2 * N_MLSTM_HEADS]
    gifT_ref[0] = gt.T[0:2 * N_MLSTM_HEADS, :]


def _in_proj(x, positions, invf, g, w, cw, bias):
    b, s, d = x.shape
    tm = SEQ_TILE
    nb = s // tm
    aw, mw = ATTN_WIDTH, MLSTM_WIDTH
    nh2 = 2 * N_MLSTM_HEADS
    out_shape = [
        jax.ShapeDtypeStruct((b, nb, aw, tm), BF16),
        jax.ShapeDtypeStruct((b, nb, tm, aw), BF16),
        jax.ShapeDtypeStruct((b * nb, 1, aw), F32),
        jax.ShapeDtypeStruct((b, nb, aw, tm), BF16),
        jax.ShapeDtypeStruct((b, s, mw), BF16),
        jax.ShapeDtypeStruct((b, nb, mw, tm), BF16),
        jax.ShapeDtypeStruct((b, s, mw), BF16),
        jax.ShapeDtypeStruct((b, s, mw), F32),
        jax.ShapeDtypeStruct((b, s, nh2), F32),
        jax.ShapeDtypeStruct((b, nh2, s), F32),
    ]
    blk4 = lambda r, c: pl.BlockSpec((1, 1, r, c), lambda bi, ji: (bi, ji, 0, 0))
    row3 = lambda c: pl.BlockSpec((1, tm, c), lambda bi, ji: (bi, ji, 0))
    out_specs = [
        blk4(aw, tm), blk4(tm, aw),
        pl.BlockSpec((1, 1, aw), lambda bi, ji: (bi * nb + ji, 0, 0)),
        blk4(aw, tm), row3(mw), blk4(mw, tm), row3(mw), row3(mw), row3(nh2),
        pl.BlockSpec((1, nh2, tm), lambda bi, ji: (bi, 0, ji)),
    ]
    in_specs = [
        pl.BlockSpec((1, tm, d), lambda bi, ji: (bi, ji, 0)),
        pl.BlockSpec((1, HALO, d), lambda bi, ji: (bi, jnp.maximum(ji * (tm // HALO) - 1, 0), 0)),
        pl.BlockSpec((1, tm, 1), lambda bi, ji: (bi, ji, 0)),
        _const_spec(invf.shape), _const_spec(g.shape), _const_spec(w.shape),
        _const_spec(cw.shape), _const_spec(bias.shape),
    ]
    return pl.pallas_call(
        _in_proj_kernel,
        grid=(b, nb),
        in_specs=in_specs,
        out_specs=out_specs,
        out_shape=out_shape,
        scratch_shapes=[pltpu.VMEM((HALO + tm, 2 * mw), F32)],
        compiler_params=pltpu.CompilerParams(
            dimension_semantics=("parallel", "parallel"), vmem_limit_bytes=VMEM_LIMIT),
        name="in_proj",
    )(x, x, positions.reshape(b, s, 1), invf, g, w, cw, bias)


def _moba_kernel(jtab_ref, ntab_ref, qT_ref, k_ref, vT_ref, kmean_ref, out_ref,
                 qm_ref, bias_ref, m_ref, l_ref, acc_ref, s_ref, cm_ref, p_ref, a_ref):
    blk = MOBA_BLOCK
    hd = ATTN_HEAD_DIM
    nb = k_ref.shape[1]
    nh = 2 * (qT_ref.shape[2] // LANES)
    n_units = nb * (nb - 1) // 2
    dim_row = lax.broadcasted_iota(jnp.int32, (LANES, blk), 0)
    blk_id = lax.broadcasted_iota(jnp.int32, (nb, blk), 0)
    key_row = lax.broadcasted_iota(jnp.int32, (blk, blk), 0)
    qry_col = lax.broadcasted_iota(jnp.int32, (blk, blk), 1)

    def lanes(h):
        return slice((h // 2) * LANES, (h // 2 + 1) * LANES)

    def head_rows(h):
        return slice(h * hd, (h + 1) * hd)

    def phase_a(j, carry):
        for h in range(nh):
            qT = qT_ref[0, j, lanes(h), :]
            qh = jnp.where((dim_row >= hd) == (h % 2 == 1), qT, jnp.zeros_like(qT))
            qm_ref[h, j] = qh
            km = kmean_ref[0, :, lanes(h)]
            km_hi = km.astype(BF16)
            km_lo = (km - km_hi.astype(F32)).astype(BF16)
            gate = _dot(km_hi, qh) + _dot(km_lo, qh)
            gate = jnp.where(blk_id < j, gate, NEG)
            rank = jnp.zeros((nb, blk), jnp.int32)
            for m in range(nb):
                gm = gate[m:m + 1, :]
                ahead = (gm > gate) | ((gm == gate) & (m < blk_id))
                rank = rank + ahead.astype(jnp.int32)
            selected = (rank < MOBA_TOPK) & (blk_id < j)
            bias_ref[h, j, 0:nb, :] = jnp.where(selected, 0.0, NEG)
            bias_ref[h, j, nb:nb + 1, :] = jnp.full((1, blk), NEG, F32)
            s_own = jnp.where(key_row <= qry_col, _dot(k_ref[0, j, :, lanes(h)], qh), NEG)
            m0 = jnp.max(s_own, axis=0, keepdims=True)
            p0 = jnp.exp2(s_own - m0)
            m_ref[h, j] = m0
            l_ref[h, j] = jnp.sum(p0, axis=0, keepdims=True)
            acc_ref[h, j] = _dot(vT_ref[0, j, head_rows(h), :], p0.astype(BF16))
        return carry

    lax.fori_loop(0, nb, phase_a, 0)

    for h in range(nh):
        s_ref[h] = jnp.zeros((blk, blk), F32)
        cm_ref[h] = jnp.zeros((1, blk), F32)
        p_ref[h] = jnp.zeros((blk, blk), BF16)
        a_ref[h] = jnp.ones((1, blk), F32)

    def phase_b(i, carry):
        j1, n1 = jtab_ref[i + 2], ntab_ref[i + 2]
        j2, n2 = jtab_ref[i + 1], ntab_ref[i + 1]
        j3, n3 = jtab_ref[i], ntab_ref[i]
        k_blk = jnp.minimum(n1, nb - 1)
        v_blk = jnp.minimum(n3, nb - 1)
        for h in range(nh):
            vTb = vT_ref[0, v_blk, head_rows(h), :]
            acc_ref[h, j3] = a_ref[h] * acc_ref[h, j3] + _dot(vTb, p_ref[h])
        for h in range(nh):
            bias = bias_ref[h, j2, pl.ds(n2, 1), :]
            m_old = m_ref[h, j2]
            m_new = jnp.maximum(m_old, cm_ref[h] + bias)
            alpha = jnp.exp2(m_old - m_new)
            p = jnp.exp2(s_ref[h] - (m_new - bias))
            l_ref[h, j2] = alpha * l_ref[h, j2] + jnp.sum(p, axis=0, keepdims=True)
            m_ref[h, j2] = m_new
            a_ref[h] = alpha
            p_ref[h] = p.astype(BF16)
        for h in range(nh):
            sT = _dot(k_ref[0, k_blk, :, lanes(h)], qm_ref[h, j1])
            s_ref[h] = sT
            cm_ref[h] = jnp.max(sT, axis=0, keepdims=True)
        return carry

    lax.fori_loop(0, n_units + 2, phase_b, 0)

    def phase_c(j, carry):
        row0 = pl.multiple_of(j * blk, blk)
        for pair in range(nh // 2):
            oT = jnp.concatenate([acc_ref[h, j] / l_ref[h, j] for h in (2 * pair, 2 * pair + 1)], axis=0)
            out_ref[0, pl.ds(row0, blk), pair * LANES:(pair + 1) * LANES] = oT.T.astype(BF16)
        return carry

    lax.fori_loop(0, nb, phase_c, 0)


def _moba_units(nb):
    pad = (0, nb)
    units = [pad, pad] + [(j, n) for j in range(nb) for n in range(j)] + [pad, pad]
    jtab = jnp.asarray([u[0] for u in units], jnp.int32)
    ntab = jnp.asarray([u[1] for u in units], jnp.int32)
    return jtab, ntab


def _moba(qT, k, vT, kmean):
    b, nb, aw, blk = qT.shape
    s = nb * blk
    w = MOBA_PAIRS_PER_STEP * LANES
    nh = 2 * MOBA_PAIRS_PER_STEP
    jtab, ntab = _moba_units(nb)
    grid_spec = pltpu.PrefetchScalarGridSpec(
        num_scalar_prefetch=2,
        grid=(b, aw // w),
        in_specs=[
            pl.BlockSpec((1, nb, w, blk), lambda bi, pi, *_: (bi, 0, pi, 0)),
            pl.BlockSpec((1, nb, blk, w), lambda bi, pi, *_: (bi, 0, 0, pi)),
            pl.BlockSpec((1, nb, w, blk), lambda bi, pi, *_: (bi, 0, pi, 0)),
            pl.BlockSpec((1, nb, w), lambda bi, pi, *_: (bi, 0, pi)),
        ],
        out_specs=pl.BlockSpec((1, s, w), lambda bi, pi, *_: (bi, 0, pi)),
        scratch_shapes=[pltpu.VMEM((nh, nb, LANES, blk), BF16),
                        pltpu.VMEM((nh, nb, nb + 8, blk), F32),
                        pltpu.VMEM((nh, nb, 1, blk), F32),
                        pltpu.VMEM((nh, nb, 1, blk), F32),
                        pltpu.VMEM((nh, nb, ATTN_HEAD_DIM, blk), F32),
                        pltpu.VMEM((nh, blk, blk), F32),
                        pltpu.VMEM((nh, 1, blk), F32),
                        pltpu.VMEM((nh, blk, blk), BF16),
                        pltpu.VMEM((nh, 1, blk), F32)],
    )
    return pl.pallas_call(
        _moba_kernel,
        grid_spec=grid_spec,
        out_shape=jax.ShapeDtypeStruct((b, s, aw), BF16),
        compiler_params=pltpu.CompilerParams(
            dimension_semantics=("parallel", "parallel"), vmem_limit_bytes=VMEM_LIMIT),
        name="moba",
    )(jtab, ntab, qT, k, vT, kmean)


def _split3(x):
    hi = x.astype(BF16)
    r = x - hi.astype(F32)
    mid = r.astype(BF16)
    lo = (r - mid.astype(F32)).astype(BF16)
    return hi, mid, lo


def _mlstm_kernel(q_ref, kT_ref, v_ref, o_ref, gif_ref, gifT_ref, g_ref, out_ref, c_ref, m_ref):
    L = SEQ_TILE
    nh, hd = N_MLSTM_HEADS, MLSTM_HEAD_DIM
    c = pl.program_id(1)

    @pl.when(c == 0)
    def _():
        c_ref[...] = jnp.zeros_like(c_ref)
        m_ref[...] = jnp.zeros_like(m_ref)

    t_idx = lax.broadcasted_iota(jnp.int32, (L, L), 0)
    s_idx = lax.broadcasted_iota(jnp.int32, (L, L), 1)
    causal = s_idx <= t_idx
    tril = jnp.where(causal, 1.0, 0.0).astype(BF16)
    triu = jnp.where(t_idx <= s_idx, 1.0, 0.0).astype(BF16)

    gi_col = gif_ref[0]
    gi_row = gifT_ref[0]
    b_col_all = sum(_dot(tril, part) for part in _split3(gi_col))
    b_row_all = sum(_dot(part, triu) for part in _split3(gi_row))
    ones_col = jnp.where(lax.broadcasted_iota(jnp.int32, (L, hd), 1) == 0, 1.0, 0.0).astype(BF16)
    gn = g_ref[...]

    for h in range(nh):
        a_row = gi_row[h:h + 1, :]
        b_row = b_row_all[nh + h:nh + h + 1, :]
        b_col = b_col_all[:, nh + h:nh + h + 1]
        q = q_ref[0, :, h * hd:(h + 1) * hd]
        kT = kT_ref[0, 0, h * hd:(h + 1) * hd, :]
        v_ext = jnp.concatenate([v_ref[0, :, h * hd:(h + 1) * hd], ones_col], axis=1)

        dmat = jnp.where(causal, b_col - b_row + a_row, NEG)
        mi = jnp.max(dmat, axis=1, keepdims=True)
        intra = _dot((jnp.exp(dmat - mi) * _dot(q, kT)).astype(BF16), v_ext)
        b_last = b_row[:, L - 1:L]
        dec = b_last - b_row + a_row
        md = jnp.max(dec, axis=1, keepdims=True)
        kv = _dot((kT.astype(F32) * jnp.exp(dec - md)).astype(BF16), v_ext)

        m_prev = m_ref[h]
        c_old = c_ref[h]
        inter = b_col + m_prev
        m_t = jnp.maximum(inter, mi)
        num_ext = jnp.exp(inter - m_t) * _dot(q, c_old.astype(BF16)) + jnp.exp(mi - m_t) * intra
        num = num_ext[:, :hd]
        den = num_ext[:, hd:hd + 1]
        h_out = num / jnp.maximum(jnp.abs(den), jnp.exp(-m_t))

        m_new = jnp.maximum(b_last + m_prev, md)
        c_ref[h] = jnp.exp(b_last + m_prev - m_new) * c_old + jnp.exp(md - m_new) * kv
        m_ref[h] = m_new

        hn = _rms(h_out, gn[:, h * hd:(h + 1) * hd])
        gate_o = _sigmoid(o_ref[0, :, h * hd:(h + 1) * hd])
        out_ref[0, :, h * hd:(h + 1) * hd] = (hn * gate_o).astype(BF16)


def _mlstm(qm, kmT, vm, om, gif, gifT, g):
    b, s, mw = qm.shape
    L = SEQ_TILE
    nc = s // L
    nh2 = 2 * N_MLSTM_HEADS
    row = lambda cdim: pl.BlockSpec((1, L, cdim), lambda bi, ci: (bi, ci, 0))
    return pl.pallas_call(
        _mlstm_kernel,
        grid=(b, nc),
        in_specs=[
            row(mw),
            pl.BlockSpec((1, 1, mw, L), lambda bi, ci: (bi, ci, 0, 0)),
            row(mw), row(mw), row(nh2),
            pl.BlockSpec((1, nh2, L), lambda bi, ci: (bi, 0, ci)),
            _const_spec(g.shape),
        ],
        out_specs=row(mw),
        out_shape=jax.ShapeDtypeStruct((b, s, mw), BF16),
        scratch_shapes=[pltpu.VMEM((N_MLSTM_HEADS, MLSTM_HEAD_DIM, 2 * MLSTM_HEAD_DIM), F32),
                        pltpu.VMEM((N_MLSTM_HEADS, 1, 1), F32)],
        compiler_params=pltpu.CompilerParams(
            dimension_semantics=("parallel", "arbitrary"), vmem_limit_bytes=VMEM_LIMIT),
        name="mlstm",
    )(qm, kmT, vm, om, gif, gifT, g)


def _mix_kernel(x_ref, a_ref, y_ref, g_ref, wg_ref, wa_ref, wm_ref, wo_ref, out_ref):
    d = x_ref.shape[-1]
    x = x_ref[...]
    h = _rms(x, g_ref[...]).astype(BF16)
    gate_a = _sigmoid(_dot(h, wg_ref[:, :d]))
    gate_m = _sigmoid(_dot(h, wg_ref[:, d:]))
    merged = gate_a * _dot(a_ref[...], wa_ref[...]) + gate_m * _dot(y_ref[...], wm_ref[...])
    out_ref[...] = x + _dot(merged.astype(BF16), wo_ref[...])


def _mix(x2d, a2d, y2d, g, wg, wa, wm, wo):
    t, d = x2d.shape
    tm = MIX_TILE
    row = lambda cdim: pl.BlockSpec((tm, cdim), lambda i: (i, 0))
    return pl.pallas_call(
        _mix_kernel,
        grid=(t // tm,),
        in_specs=[row(d), row(a2d.shape[1]), row(y2d.shape[1]), _const_spec(g.shape),
                  _const_spec(wg.shape), _const_spec(wa.shape), _const_spec(wm.shape), _const_spec(wo.shape)],
        out_specs=row(d),
        out_shape=jax.ShapeDtypeStruct((t, d), F32),
        compiler_params=pltpu.CompilerParams(
            dimension_semantics=("parallel",), vmem_limit_bytes=VMEM_LIMIT),
        name="mix",
    )(x2d, a2d, y2d, g, wg, wa, wm, wo)


def _ffn_kernel(x_ref, xh_ref, g_ref, wg_ref, wu_ref, cg_ref, cu_ref, wd_ref, gf_ref, out_ref,
                hext_ref, ug_ref, uu_ref, act_ref, acc_ref):
    tm = FFN_TILE
    nchunk = wg_ref.shape[0]
    j = pl.program_id(1)
    x = x_ref[0]
    g = g_ref[...]
    hext_ref[HALO:HALO + tm, :] = _rms(x, g).astype(BF16)
    hh = _rms(xh_ref[0], g)
    hext_ref[0:HALO, :] = jnp.where(j == 0, 0.0, hh).astype(BF16)
    acc_ref[...] = x

    def stage_a(c):
        ug_ref[...] = _dot(hext_ref[...], wg_ref[c])
        uu_ref[...] = _dot(hext_ref[...], wu_ref[c])

    def conv(u_ref, cw, r0, rows):
        out = cw[0:1] * u_ref[HALO - 2 + r0:HALO - 2 + r0 + rows, :]
        out = out + cw[1:2] * u_ref[HALO - 1 + r0:HALO - 1 + r0 + rows, :]
        return out + cw[2:3] * u_ref[HALO + r0:HALO + r0 + rows, :]

    def stage_b(c):
        cg = cg_ref[c]
        cu = cu_ref[c]
        for r0 in range(0, tm, FFN_ROWS):
            gate = conv(ug_ref, cg, r0, FFN_ROWS)
            up = conv(uu_ref, cu, r0, FFN_ROWS)
            act_ref[r0:r0 + FFN_ROWS, :] = (gate * _sigmoid(gate) * up).astype(BF16)

    def stage_c(c):
        acc_ref[...] += _dot(act_ref[...], wd_ref[c])

    stage_a(0)
    stage_b(0)
    stage_a(1)

    def body(c, carry):
        stage_c(c - 2)
        stage_b(c - 1)
        stage_a(c)
        return carry

    lax.fori_loop(2, nchunk, body, 0)
    stage_c(nchunk - 2)
    stage_b(nchunk - 1)
    stage_c(nchunk - 1)
    out_ref[0] = _rms(acc_ref[...], gf_ref[...])


def _ffn(x1, g2, wg, wu, cg, cu, wd, gf):
    b, s, d = x1.shape
    tm = FFN_TILE
    return pl.pallas_call(
        _ffn_kernel,
        grid=(b, s // tm),
        in_specs=[
            pl.BlockSpec((1, tm, d), lambda bi, ji: (bi, ji, 0)),
            pl.BlockSpec((1, HALO, d), lambda bi, ji: (bi, jnp.maximum(ji * (tm // HALO) - 1, 0), 0)),
            _const_spec(g2.shape), _const_spec(wg.shape), _const_spec(wu.shape),
            _const_spec(cg.shape), _const_spec(cu.shape), _const_spec(wd.shape), _const_spec(gf.shape),
        ],
        out_specs=pl.BlockSpec((1, tm, d), lambda bi, ji: (bi, ji, 0)),
        out_shape=jax.ShapeDtypeStruct((b, s, d), F32),
        scratch_shapes=[pltpu.VMEM((HALO + tm, d), BF16),
                        pltpu.VMEM((HALO + tm, FFN_CHUNK), F32),
                        pltpu.VMEM((HALO + tm, FFN_CHUNK), F32),
                        pltpu.VMEM((tm, FFN_CHUNK), BF16),
                        pltpu.VMEM((tm, d), F32)],
        compiler_params=pltpu.CompilerParams(
            dimension_semantics=("parallel", "parallel"), vmem_limit_bytes=VMEM_LIMIT),
        name="ffn",
    )(x1, x1, g2, wg, wu, cg, cu, wd, gf)


def _chunk_cols(w, chunk):
    kdim, n = w.shape
    return w.reshape(kdim, n // chunk, chunk).transpose(1, 0, 2)


def kernel(x, positions, norm_mix_g, w_in, conv_mlstm, i_bias, f_bias, mlstm_norm_g, w_branch_attn,
           w_branch_mlstm, w_out, norm_ffn_g, w_up, conv_ffn, w_down, norm_final_g):
    b, s, d = x.shape
    aw, mw, nh = ATTN_WIDTH, MLSTM_WIDTH, N_MLSTM_HEADS
    assert s % FFN_TILE == 0 and s % SEQ_TILE == 0 and d % LANES == 0
    n_qkv = 3 * aw + 4 * mw
    d_ff = w_down.shape[1]
    assert d_ff % FFN_CHUNK == 0
    half = ATTN_HEAD_DIM // 2
    inv_freq = ROPE_THETA ** (-(jnp.arange(half, dtype=F32) / half))
    invf = jnp.tile(inv_freq, LANES // half).reshape(1, LANES)

    for layer in range(w_in.shape[0]):
        wl = w_in[layer]
        w1 = jnp.concatenate(
            [wl[:, :n_qkv + 2 * nh], jnp.zeros((d, LANES - 2 * nh), wl.dtype)], axis=1).astype(BF16)
        wg = wl[:, n_qkv + 2 * nh:].astype(BF16)
        bias = jnp.concatenate(
            [i_bias[layer], f_bias[layer], jnp.zeros((LANES - 2 * nh,), F32)]).reshape(1, LANES)

        qT, k, kmean, vT, qm, kmT, vm, om, gif, gifT = _in_proj(
            x, positions, invf, norm_mix_g[layer].reshape(1, d), w1, conv_mlstm[layer], bias)
        attn = _moba(qT, k, vT, kmean.reshape(b, s // MOBA_BLOCK, aw))
        y_m = _mlstm(qm, kmT, vm, om, gif, gifT, mlstm_norm_g[layer].reshape(1, mw))
        x1 = _mix(x.reshape(b * s, d), attn.reshape(b * s, aw), y_m.reshape(b * s, mw),
                  norm_mix_g[layer].reshape(1, d), wg, w_branch_attn[layer].astype(BF16),
                  w_branch_mlstm[layer].astype(BF16), w_out[layer].astype(BF16)).reshape(b, s, d)

        wu = w_up[layer].astype(BF16)
        cf = conv_ffn[layer]
        last = layer == w_in.shape[0] - 1
        gfin = norm_final_g.reshape(1, d) if last else None
        assert last, "only the final layer fuses the output norm"
        x = _ffn(x1, norm_ffn_g[layer].reshape(1, d),
                 _chunk_cols(wu[:, :d_ff], FFN_CHUNK), _chunk_cols(wu[:, d_ff:], FFN_CHUNK),
                 _chunk_cols(cf[:, :d_ff], FFN_CHUNK), _chunk_cols(cf[:, d_ff:], FFN_CHUNK),
                 w_down[layer].astype(BF16).reshape(d_ff // FFN_CHUNK, FFN_CHUNK, d), gfin)
    return x
```

```python
import functools
import math

import jax
import jax.numpy as jnp
from jax import lax
from jax.experimental import pallas as pl
from jax.experimental.pallas import tpu as pltpu

F32 = jnp.float32
BF16 = jnp.bfloat16

EPS = 1e-6
NEG = -1e30
ROPE_THETA = 10000.0

N_ATTN_HEADS = 8
ATTN_HEAD_DIM = 64
ATTN_WIDTH = N_ATTN_HEADS * ATTN_HEAD_DIM
MOBA_BLOCK = 256
MOBA_TOPK = 3
N_MLSTM_HEADS = 4
MLSTM_HEAD_DIM = 128
MLSTM_WIDTH = N_MLSTM_HEADS * MLSTM_HEAD_DIM

LANES = 128
HALO = 16
SEQ_TILE = MOBA_BLOCK
MIX_TILE = 512
FFN_TILE = 512
FFN_CHUNK = 256
CONV_STRIDE = 4
MOBA_PAIRS_PER_STEP = 4
LOG2_E = math.log2(math.e)
VMEM_LIMIT = 56 * 1024 * 1024


def _dot(a, b):
    return jnp.dot(a, b, preferred_element_type=F32)


def _rms(x, g):
    ms = jnp.mean(x * x, axis=-1, keepdims=True)
    return x * lax.rsqrt(ms + EPS) * g


def _sigmoid(x):
    return 1.0 / (1.0 + jnp.exp(-x))


def _const_spec(shape):
    nd = len(shape)
    return pl.BlockSpec(shape, lambda *_: (0,) * nd, pipeline_mode=pl.Buffered(1))


def _in_proj_kernel(x_ref, xh_ref, pos_ref, invf_ref, g_ref, w_ref, cw_ref, bias_ref,
                    qT_ref, k_ref, kmean_ref, vT_ref, qm_ref, kmT_ref, vm_ref, om_ref, gif_ref, gifT_ref,
                    ext_ref):
    tm = SEQ_TILE
    aw, mw = ATTN_WIDTH, MLSTM_WIDTH
    j = pl.program_id(1)
    g = g_ref[...]
    h = _rms(x_ref[0], g).astype(BF16)

    ang = pos_ref[0].astype(F32) * invf_ref[...]
    cos = jnp.cos(ang)
    sin = jnp.sin(ang)
    lane = lax.broadcasted_iota(jnp.int32, (tm, LANES), 1)
    first_half = (lane % ATTN_HEAD_DIM) < (ATTN_HEAD_DIM // 2)
    sin_signed = jnp.where(first_half, -sin, sin)

    def rope(t):
        outs = []
        for c in range(aw // LANES):
            tg = t[:, c * LANES:(c + 1) * LANES]
            swapped = jnp.where(first_half, pltpu.roll(tg, LANES - 32, 1), pltpu.roll(tg, 32, 1))
            outs.append(tg * cos + swapped * sin_signed)
        return jnp.concatenate(outs, axis=1)

    qk = _dot(h, w_ref[:, 0:2 * aw])
    q = rope(qk[:, :aw]) * (LOG2_E / math.sqrt(ATTN_HEAD_DIM))
    k = rope(qk[:, aw:])
    qT_ref[0, 0] = q.T.astype(BF16)
    k_ref[0, 0] = k.astype(BF16)
    kmean_ref[0] = jnp.mean(k, axis=0, keepdims=True)
    v = _dot(h, w_ref[:, 2 * aw:3 * aw])
    vT_ref[0, 0] = v.T.astype(BF16)

    c0 = 3 * aw
    pm = _dot(h, w_ref[:, c0:c0 + 2 * mw])
    hh = _rms(xh_ref[0], g).astype(BF16)
    ph = _dot(hh, w_ref[:, c0:c0 + 2 * mw])
    ext_ref[0:HALO, :] = jnp.where(j == 0, 0.0, ph)
    ext_ref[HALO:HALO + tm, :] = pm
    cw = cw_ref[...]
    conv = cw[0:1] * ext_ref[HALO - 3:HALO - 3 + tm, :]
    conv = conv + cw[1:2] * ext_ref[HALO - 2:HALO - 2 + tm, :]
    conv = conv + cw[2:3] * ext_ref[HALO - 1:HALO - 1 + tm, :]
    conv = conv + cw[3:4] * pm
    act = conv * _sigmoid(conv)
    qm_ref[0] = (act[:, :mw] * (1.0 / math.sqrt(MLSTM_HEAD_DIM))).astype(BF16)
    kmT_ref[0, 0] = act[:, mw:].T.astype(BF16)

    c1 = c0 + 2 * mw
    vo = _dot(h, w_ref[:, c1:c1 + 2 * mw])
    vm_ref[0] = vo[:, :mw].astype(BF16)
    om_ref[0] = vo[:, mw:]

    c2 = c1 + 2 * mw
    gi = _dot(h, w_ref[:, c2:c2 + LANES]) + bias_ref[...]
    log_sig = jnp.minimum(gi, 0.0) - jnp.log1p(jnp.exp(-jnp.abs(gi)))
    gt = jnp.where(lane < N_MLSTM_HEADS, gi, log_sig)
    gif_ref[0] = gt[:, 0:2 * N_MLSTM_HEADS]
    gifT_ref[0] = gt.T[0:2 * N_MLSTM_HEADS, :]


def _in_proj(x, positions, invf, g, w, cw, bias):
    b, s, d = x.shape
    tm = SEQ_TILE
    nb = s // tm
    aw, mw = ATTN_WIDTH, MLSTM_WIDTH
    nh2 = 2 * N_MLSTM_HEADS
    out_shape = [
        jax.ShapeDtypeStruct((b, nb, aw, tm), BF16),
        jax.ShapeDtypeStruct((b, nb, tm, aw), BF16),
        jax.ShapeDtypeStruct((b * nb, 1, aw), F32),
        jax.ShapeDtypeStruct((b, nb, aw, tm), BF16),
        jax.ShapeDtypeStruct((b, s, mw), BF16),
        jax.ShapeDtypeStruct((b, nb, mw, tm), BF16),
        jax.ShapeDtypeStruct((b, s, mw), BF16),
        jax.ShapeDtypeStruct((b, s, mw), F32),
        jax.ShapeDtypeStruct((b, s, nh2), F32),
        jax.ShapeDtypeStruct((b, nh2, s), F32),
    ]
    blk4 = lambda r, c: pl.BlockSpec((1, 1, r, c), lambda bi, ji: (bi, ji, 0, 0))
    row3 = lambda c: pl.BlockSpec((1, tm, c), lambda bi, ji: (bi, ji, 0))
    out_specs = [
        blk4(aw, tm), blk4(tm, aw),
        pl.BlockSpec((1, 1, aw), lambda bi, ji: (bi * nb + ji, 0, 0)),
        blk4(aw, tm), row3(mw), blk4(mw, tm), row3(mw), row3(mw), row3(nh2),
        pl.BlockSpec((1, nh2, tm), lambda bi, ji: (bi, 0, ji)),
    ]
    in_specs = [
        pl.BlockSpec((1, tm, d), lambda bi, ji: (bi, ji, 0)),
        pl.BlockSpec((1, HALO, d), lambda bi, ji: (bi, jnp.maximum(ji * (tm // HALO) - 1, 0), 0)),
        pl.BlockSpec((1, tm, 1), lambda bi, ji: (bi, ji, 0)),
        _const_spec(invf.shape), _const_spec(g.shape), _const_spec(w.shape),
        _const_spec(cw.shape), _const_spec(bias.shape),
    ]
    return pl.pallas_call(
        _in_proj_kernel,
        grid=(b, nb),
        in_specs=in_specs,
        out_specs=out_specs,
        out_shape=out_shape,
        scratch_shapes=[pltpu.VMEM((HALO + tm, 2 * mw), F32)],
        compiler_params=pltpu.CompilerParams(
            dimension_semantics=("parallel", "parallel"), vmem_limit_bytes=VMEM_LIMIT),
        name="in_proj",
    )(x, x, positions.reshape(b, s, 1), invf, g, w, cw, bias)


def _moba_kernel(jtab_ref, ntab_ref, qT_ref, k_ref, vT_ref, kmean_ref, out_ref,
                 qm_ref, bias_ref, m_ref, l_ref, acc_ref, s_ref, cm_ref, p_ref, a_ref):
    blk = MOBA_BLOCK
    hd = ATTN_HEAD_DIM
    nb = k_ref.shape[1]
    nh = 2 * (qT_ref.shape[2] // LANES)
    n_units = nb * (nb - 1) // 2
    dim_row = lax.broadcasted_iota(jnp.int32, (LANES, blk), 0)
    blk_id = lax.broadcasted_iota(jnp.int32, (nb, blk), 0)
    key_row = lax.broadcasted_iota(jnp.int32, (blk, blk), 0)
    qry_col = lax.broadcasted_iota(jnp.int32, (blk, blk), 1)

    def lanes(h):
        return slice((h // 2) * LANES, (h // 2 + 1) * LANES)

    def head_rows(h):
        return slice(h * hd, (h + 1) * hd)

    def phase_a(j, carry):
        for h in range(nh):
            qT = qT_ref[0, j, lanes(h), :]
            qh = jnp.where((dim_row >= hd) == (h % 2 == 1), qT, jnp.zeros_like(qT))
            qm_ref[h, j] = qh
            km = kmean_ref[0, :, lanes(h)]
            km_hi = km.astype(BF16)
            km_lo = (km - km_hi.astype(F32)).astype(BF16)
            gate = _dot(km_hi, qh) + _dot(km_lo, qh)
            gate = jnp.where(blk_id < j, gate, NEG)
            rank = jnp.zeros((nb, blk), jnp.int32)
            for m in range(nb):
                gm = gate[m:m + 1, :]
                ahead = (gm > gate) | ((gm == gate) & (m < blk_id))
                rank = rank + ahead.astype(jnp.int32)
            selected = (rank < MOBA_TOPK) & (blk_id < j)
            bias_ref[h, j, 0:nb, :] = jnp.where(selected, 0.0, NEG)
            bias_ref[h, j, nb:nb + 1, :] = jnp.full((1, blk), NEG, F32)
            s_own = jnp.where(key_row <= qry_col, _dot(k_ref[0, j, :, lanes(h)], qh), NEG)
            m0 = jnp.max(s_own, axis=0, keepdims=True)
            p0 = jnp.exp2(s_own - m0)
            m_ref[h, j] = m0
            l_ref[h, j] = jnp.sum(p0, axis=0, keepdims=True)
            acc_ref[h, j] = _dot(vT_ref[0, j, head_rows(h), :], p0.astype(BF16))
        return carry

    lax.fori_loop(0, nb, phase_a, 0)

    for h in range(nh):
        s_ref[h] = jnp.zeros((blk, blk), F32)
        cm_ref[h] = jnp.zeros((1, blk), F32)
        p_ref[h] = jnp.zeros((blk, blk), BF16)
        a_ref[h] = jnp.ones((1, blk), F32)

    def phase_b(i, carry):
        j1, n1 = jtab_ref[i + 2], ntab_ref[i + 2]
        j2, n2 = jtab_ref[i + 1], ntab_ref[i + 1]
        j3, n3 = jtab_ref[i], ntab_ref[i]
        k_blk = jnp.minimum(n1, nb - 1)
        v_blk = jnp.minimum(n3, nb - 1)
        for h in range(nh):
            vTb = vT_ref[0, v_blk, head_rows(h), :]
            acc_ref[h, j3] = a_ref[h] * acc_ref[h, j3] + _dot(vTb, p_ref[h])
        for h in range(nh):
            bias = bias_ref[h, j2, pl.ds(n2, 1), :]
            m_old = m_ref[h, j2]
            m_new = jnp.maximum(m_old, cm_ref[h] + bias)
            alpha = jnp.exp2(m_old - m_new)
            p = jnp.exp2(s_ref[h] - (m_new - bias))
            l_ref[h, j2] = alpha * l_ref[h, j2] + jnp.sum(p, axis=0, keepdims=True)
            m_ref[h, j2] = m_new
            a_ref[h] = alpha
            p_ref[h] = p.astype(BF16)
        for h in range(nh):
            sT = _dot(k_ref[0, k_blk, :, lanes(h)], qm_ref[h, j1])
            s_ref[h] = sT
            cm_ref[h] = jnp.max(sT, axis=0, keepdims=True)
        return carry

    lax.fori_loop(0, n_units + 2, phase_b, 0)

    def phase_c(j, carry):
        row0 = pl.multiple_of(j * blk, blk)
        for pair in range(nh // 2):
            oT = jnp.concatenate([acc_ref[h, j] / l_ref[h, j] for h in (2 * pair, 2 * pair + 1)], axis=0)
            out_ref[0, pl.ds(row0, blk), pair * LANES:(pair + 1) * LANES] = oT.T.astype(BF16)
        return carry

    lax.fori_loop(0, nb, phase_c, 0)


def _moba_units(nb):
    pad = (0, nb)
    units = [pad, pad] + [(j, n) for j in range(nb) for n in range(j)] + [pad, pad]
    jtab = jnp.asarray([u[0] for u in units], jnp.int32)
    ntab = jnp.asarray([u[1] for u in units], jnp.int32)
    return jtab, ntab


def _moba(qT, k, vT, kmean):
    b, nb, aw, blk = qT.shape
    s = nb * blk
    w = MOBA_PAIRS_PER_STEP * LANES
    nh = 2 * MOBA_PAIRS_PER_STEP
    jtab, ntab = _moba_units(nb)
    grid_spec = pltpu.PrefetchScalarGridSpec(
        num_scalar_prefetch=2,
        grid=(b, aw // w),
        in_specs=[
            pl.BlockSpec((1, nb, w, blk), lambda bi, pi, *_: (bi, 0, pi, 0)),
            pl.BlockSpec((1, nb, blk, w), lambda bi, pi, *_: (bi, 0, 0, pi)),
            pl.BlockSpec((1, nb, w, blk), lambda bi, pi, *_: (bi, 0, pi, 0)),
            pl.BlockSpec((1, nb, w), lambda bi, pi, *_: (bi, 0, pi)),
        ],
        out_specs=pl.BlockSpec((1, s, w), lambda bi, pi, *_: (bi, 0, pi)),
        scratch_shapes=[pltpu.VMEM((nh, nb, LANES, blk), BF16),
                        pltpu.VMEM((nh, nb, nb + 8, blk), F32),
                        pltpu.VMEM((nh, nb, 1, blk), F32),
                        pltpu.VMEM((nh, nb, 1, blk), F32),
                        pltpu.VMEM((nh, nb, ATTN_HEAD_DIM, blk), F32),
                        pltpu.VMEM((nh, blk, blk), F32),
                        pltpu.VMEM((nh, 1, blk), F32),
                        pltpu.VMEM((nh, blk, blk), BF16),
                        pltpu.VMEM((nh, 1, blk), F32)],
    )
    return pl.pallas_call(
        _moba_kernel,
        grid_spec=grid_spec,
        out_shape=jax.ShapeDtypeStruct((b, s, aw), BF16),
        compiler_params=pltpu.CompilerParams(
            dimension_semantics=("parallel", "parallel"), vmem_limit_bytes=VMEM_LIMIT),
        name="moba",
    )(jtab, ntab, qT, k, vT, kmean)


def _split3(x):
    hi = x.astype(BF16)
    r = x - hi.astype(F32)
    mid = r.astype(BF16)
    lo = (r - mid.astype(F32)).astype(BF16)
    return hi, mid, lo


def _mlstm_kernel(q_ref, kT_ref, v_ref, o_ref, gif_ref, gifT_ref, g_ref, out_ref, c_ref, m_ref):
    L = SEQ_TILE
    nh, hd = N_MLSTM_HEADS, MLSTM_HEAD_DIM
    c = pl.program_id(1)

    @pl.when(c == 0)
    def _():
        c_ref[...] = jnp.zeros_like(c_ref)
        m_ref[...] = jnp.zeros_like(m_ref)

    t_idx = lax.broadcasted_iota(jnp.int32, (L, L), 0)
    s_idx = lax.broadcasted_iota(jnp.int32, (L, L), 1)
    causal = s_idx <= t_idx
    tril = jnp.where(causal, 1.0, 0.0).astype(BF16)
    triu = jnp.where(t_idx <= s_idx, 1.0, 0.0).astype(BF16)

    gi_col = gif_ref[0]
    gi_row = gifT_ref[0]
    b_col_all = sum(_dot(tril, part) for part in _split3(gi_col))
    b_row_all = sum(_dot(part, triu) for part in _split3(gi_row))
    ones_col = jnp.where(lax.broadcasted_iota(jnp.int32, (L, hd), 1) == 0, 1.0, 0.0).astype(BF16)
    gn = g_ref[...]

    for h in range(nh):
        a_row = gi_row[h:h + 1, :]
        b_row = b_row_all[nh + h:nh + h + 1, :]
        b_col = b_col_all[:, nh + h:nh + h + 1]
        q = q_ref[0, :, h * hd:(h + 1) * hd]
        kT = kT_ref[0, 0, h * hd:(h + 1) * hd, :]
        v_ext = jnp.concatenate([v_ref[0, :, h * hd:(h + 1) * hd], ones_col], axis=1)

        dmat = jnp.where(causal, b_col - b_row + a_row, NEG)
        mi = jnp.max(dmat, axis=1, keepdims=True)
        intra = _dot((jnp.exp(dmat - mi) * _dot(q, kT)).astype(BF16), v_ext)
        b_last = b_row[:, L - 1:L]
        dec = b_last - b_row + a_row
        md = jnp.max(dec, axis=1, keepdims=True)
        kv = _dot((kT.astype(F32) * jnp.exp(dec - md)).astype(BF16), v_ext)

        m_prev = m_ref[h]
        c_old = c_ref[h]
        inter = b_col + m_prev
        m_t = jnp.maximum(inter, mi)
        num_ext = jnp.exp(inter - m_t) * _dot(q, c_old.astype(BF16)) + jnp.exp(mi - m_t) * intra
        num = num_ext[:, :hd]
        den = num_ext[:, hd:hd + 1]
        h_out = num / jnp.maximum(jnp.abs(den), jnp.exp(-m_t))

        m_new = jnp.maximum(b_last + m_prev, md)
        c_ref[h] = jnp.exp(b_last + m_prev - m_new) * c_old + jnp.exp(md - m_new) * kv
        m_ref[h] = m_new

        hn = _rms(h_out, gn[:, h * hd:(h + 1) * hd])
        gate_o = _sigmoid(o_ref[0, :, h * hd:(h + 1) * hd])
        out_ref[0, :, h * hd:(h + 1) * hd] = (hn * gate_o).astype(BF16)


def _mlstm(qm, kmT, vm, om, gif, gifT, g):
    b, s, mw = qm.shape
    L = SEQ_TILE
    nc = s // L
    nh2 = 2 * N_MLSTM_HEADS
    row = lambda cdim: pl.BlockSpec((1, L, cdim), lambda bi, ci: (bi, ci, 0))
    return pl.pallas_call(
        _mlstm_kernel,
        grid=(b, nc),
        in_specs=[
            row(mw),
            pl.BlockSpec((1, 1, mw, L), lambda bi, ci: (bi, ci, 0, 0)),
            row(mw), row(mw), row(nh2),
            pl.BlockSpec((1, nh2, L), lambda bi, ci: (bi, 0, ci)),
            _const_spec(g.shape),
        ],
        out_specs=row(mw),
        out_shape=jax.ShapeDtypeStruct((b, s, mw), BF16),
        scratch_shapes=[pltpu.VMEM((N_MLSTM_HEADS, MLSTM_HEAD_DIM, 2 * MLSTM_HEAD_DIM), F32),
                        pltpu.VMEM((N_MLSTM_HEADS, 1, 1), F32)],
        compiler_params=pltpu.CompilerParams(
            dimension_semantics=("parallel", "arbitrary"), vmem_limit_bytes=VMEM_LIMIT),
        name="mlstm",
    )(qm, kmT, vm, om, gif, gifT, g)


def _mix_kernel(x_ref, a_ref, y_ref, g_ref, wg_ref, wa_ref, wm_ref, wo_ref, out_ref):
    d = x_ref.shape[-1]
    x = x_ref[...]
    h = _rms(x, g_ref[...]).astype(BF16)
    gate_a = _sigmoid(_dot(h, wg_ref[:, :d]))
    gate_m = _sigmoid(_dot(h, wg_ref[:, d:]))
    merged = gate_a * _dot(a_ref[...], wa_ref[...]) + gate_m * _dot(y_ref[...], wm_ref[...])
    out_ref[...] = x + _dot(merged.astype(BF16), wo_ref[...])


def _mix(x2d, a2d, y2d, g, wg, wa, wm, wo):
    t, d = x2d.shape
    tm = MIX_TILE
    row = lambda cdim: pl.BlockSpec((tm, cdim), lambda i: (i, 0))
    return pl.pallas_call(
        _mix_kernel,
        grid=(t // tm,),
        in_specs=[row(d), row(a2d.shape[1]), row(y2d.shape[1]), _const_spec(g.shape),
                  _const_spec(wg.shape), _const_spec(wa.shape), _const_spec(wm.shape), _const_spec(wo.shape)],
        out_specs=row(d),
        out_shape=jax.ShapeDtypeStruct((t, d), F32),
        compiler_params=pltpu.CompilerParams(
            dimension_semantics=("parallel",), vmem_limit_bytes=VMEM_LIMIT),
        name="mix",
    )(x2d, a2d, y2d, g, wg, wa, wm, wo)


def _ffn_kernel(x_ref, xh_ref, g_ref, wg_ref, wu_ref, cg_ref, cu_ref, wd_ref, gf_ref, out_ref,
                hext_ref, ug_ref, uu_ref, act_ref, acc_ref):
    tm = FFN_TILE
    nchunk = wg_ref.shape[0]
    nslab = ug_ref.shape[0]
    j = pl.program_id(1)
    x = x_ref[0]
    g = g_ref[...]
    hext_ref[HALO:HALO + tm, :] = _rms(x, g).astype(BF16)
    hh = _rms(xh_ref[0], g)
    hext_ref[0:HALO, :] = jnp.where(j == 0, 0.0, hh).astype(BF16)
    acc_ref[...] = x

    def stage_a(c):
        ug = _dot(hext_ref[...], wg_ref[c])
        uu = _dot(hext_ref[...], wu_ref[c])
        for sl in range(nslab):
            ug_ref[sl] = ug[:, sl * LANES:(sl + 1) * LANES]
            uu_ref[sl] = uu[:, sl * LANES:(sl + 1) * LANES]

    def taps(u_ref, sl, row, cw):
        out = cw[0:1] * u_ref[sl, pl.ds(HALO + row - 2, 8, stride=CONV_STRIDE), :]
        out = out + cw[1:2] * u_ref[sl, pl.ds(HALO + row - 1, 8, stride=CONV_STRIDE), :]
        return out + cw[2:3] * u_ref[sl, pl.ds(HALO + row, 8, stride=CONV_STRIDE), :]

    def stage_b(c):
        for sl in range(nslab):
            cg = cg_ref[c][:, sl * LANES:(sl + 1) * LANES]
            cu = cu_ref[c][:, sl * LANES:(sl + 1) * LANES]
            for r0 in range(0, tm, 8 * CONV_STRIDE):
                for k in range(CONV_STRIDE):
                    gate = taps(ug_ref, sl, r0 + k, cg)
                    up = taps(uu_ref, sl, r0 + k, cu)
                    act_ref[sl, pl.ds(r0 + k, 8, stride=CONV_STRIDE), :] = gate * _sigmoid(gate) * up

    def stage_c(c):
        act = jnp.concatenate([act_ref[sl] for sl in range(nslab)], axis=1).astype(BF16)
        acc_ref[...] += _dot(act, wd_ref[c])

    stage_a(0)
    stage_b(0)
    stage_a(1)

    def body(c, carry):
        stage_c(c - 2)
        stage_b(c - 1)
        stage_a(c)
        return carry

    lax.fori_loop(2, nchunk, body, 0)
    stage_c(nchunk - 2)
    stage_b(nchunk - 1)
    stage_c(nchunk - 1)
    out_ref[0] = _rms(acc_ref[...], gf_ref[...])


def _ffn(x1, g2, wg, wu, cg, cu, wd, gf):
    b, s, d = x1.shape
    tm = FFN_TILE
    return pl.pallas_call(
        _ffn_kernel,
        grid=(b, s // tm),
        in_specs=[
            pl.BlockSpec((1, tm, d), lambda bi, ji: (bi, ji, 0)),
            pl.BlockSpec((1, HALO, d), lambda bi, ji: (bi, jnp.maximum(ji * (tm // HALO) - 1, 0), 0)),
            _const_spec(g2.shape), _const_spec(wg.shape), _const_spec(wu.shape),
            _const_spec(cg.shape), _const_spec(cu.shape), _const_spec(wd.shape), _const_spec(gf.shape),
        ],
        out_specs=pl.BlockSpec((1, tm, d), lambda bi, ji: (bi, ji, 0)),
        out_shape=jax.ShapeDtypeStruct((b, s, d), F32),
        scratch_shapes=[pltpu.VMEM((HALO + tm, d), BF16),
                        pltpu.VMEM((FFN_CHUNK // LANES, HALO + tm, LANES), F32),
                        pltpu.VMEM((FFN_CHUNK // LANES, HALO + tm, LANES), F32),
                        pltpu.VMEM((FFN_CHUNK // LANES, tm, LANES), F32),
                        pltpu.VMEM((tm, d), F32)],
        compiler_params=pltpu.CompilerParams(
            dimension_semantics=("parallel", "parallel"), vmem_limit_bytes=VMEM_LIMIT),
        name="ffn",
    )(x1, x1, g2, wg, wu, cg, cu, wd, gf)


def _chunk_cols(w, chunk):
    kdim, n = w.shape
    return w.reshape(kdim, n // chunk, chunk).transpose(1, 0, 2)


def kernel(x, positions, norm_mix_g, w_in, conv_mlstm, i_bias, f_bias, mlstm_norm_g, w_branch_attn,
           w_branch_mlstm, w_out, norm_ffn_g, w_up, conv_ffn, w_down, norm_final_g):
    b, s, d = x.shape
    aw, mw, nh = ATTN_WIDTH, MLSTM_WIDTH, N_MLSTM_HEADS
    assert s % FFN_TILE == 0 and s % SEQ_TILE == 0 and d % LANES == 0
    n_qkv = 3 * aw + 4 * mw
    d_ff = w_down.shape[1]
    assert d_ff % FFN_CHUNK == 0
    half = ATTN_HEAD_DIM // 2
    inv_freq = ROPE_THETA ** (-(jnp.arange(half, dtype=F32) / half))
    invf = jnp.tile(inv_freq, LANES // half).reshape(1, LANES)

    for layer in range(w_in.shape[0]):
        wl = w_in[layer]
        w1 = jnp.concatenate(
            [wl[:, :n_qkv + 2 * nh], jnp.zeros((d, LANES - 2 * nh), wl.dtype)], axis=1).astype(BF16)
        wg = wl[:, n_qkv + 2 * nh:].astype(BF16)
        bias = jnp.concatenate(
            [i_bias[layer], f_bias[layer], jnp.zeros((LANES - 2 * nh,), F32)]).reshape(1, LANES)

        qT, k, kmean, vT, qm, kmT, vm, om, gif, gifT = _in_proj(
            x, positions, invf, norm_mix_g[layer].reshape(1, d), w1, conv_mlstm[layer], bias)
        attn = _moba(qT, k, vT, kmean.reshape(b, s // MOBA_BLOCK, aw))
        y_m = _mlstm(qm, kmT, vm, om, gif, gifT, mlstm_norm_g[layer].reshape(1, mw))
        x1 = _mix(x.reshape(b * s, d), attn.reshape(b * s, aw), y_m.reshape(b * s, mw),
                  norm_mix_g[layer].reshape(1, d), wg, w_branch_attn[layer].astype(BF16),
                  w_branch_mlstm[layer].astype(BF16), w_out[layer].astype(BF16)).reshape(b, s, d)

        wu = w_up[layer].astype(BF16)
        cf = conv_ffn[layer]
        last = layer == w_in.shape[0] - 1
        gfin = norm_final_g.reshape(1, d) if last else None
        assert last, "only the final layer fuses the output norm"
        x = _ffn(x1, norm_ffn_g[layer].reshape(1, d),
                 _chunk_cols(wu[:, :d_ff], FFN_CHUNK), _chunk_cols(wu[:, d_ff:], FFN_CHUNK),
                 _chunk_cols(cf[:, :d_ff], FFN_CHUNK), _chunk_cols(cf[:, d_ff:], FFN_CHUNK),
                 w_down[layer].astype(BF16).reshape(d_ff // FFN_CHUNK, FFN_CHUNK, d), gfin)
    return x
```

```python
import functools
import math

import jax
import jax.numpy as jnp
from jax import lax
from jax.experimental import pallas as pl
from jax.experimental.pallas import tpu as pltpu

F32 = jnp.float32
BF16 = jnp.bfloat16

EPS = 1e-6
NEG = -1e30
ROPE_THETA = 10000.0

N_ATTN_HEADS = 8
ATTN_HEAD_DIM = 64
ATTN_WIDTH = N_ATTN_HEADS * ATTN_HEAD_DIM
MOBA_BLOCK = 256
MOBA_TOPK = 3
N_MLSTM_HEADS = 4
MLSTM_HEAD_DIM = 128
MLSTM_WIDTH = N_MLSTM_HEADS * MLSTM_HEAD_DIM

LANES = 128
HALO = 16
SEQ_TILE = MOBA_BLOCK
MIX_TILE = 512
FFN_TILE = 512
FFN_CHUNK = 256
CONV_STRIDE = 4
MOBA_PAIRS_PER_STEP = 4
LOG2_E = math.log2(math.e)
VMEM_LIMIT = 56 * 1024 * 1024


def _dot(a, b):
    return jnp.dot(a, b, preferred_element_type=F32)


def _rms(x, g):
    ms = jnp.mean(x * x, axis=-1, keepdims=True)
    return x * lax.rsqrt(ms + EPS) * g


def _sigmoid(x):
    return 1.0 / (1.0 + jnp.exp(-x))


def _const_spec(shape):
    nd = len(shape)
    return pl.BlockSpec(shape, lambda *_: (0,) * nd, pipeline_mode=pl.Buffered(1))


def _rope_table_kernel(pos_ref, invf_ref, cos_ref, sin_ref):
    ang = pos_ref[...].astype(F32) * invf_ref[...]
    cos_ref[...] = jnp.cos(ang)
    sin_ref[...] = jnp.sin(ang)


def _rope_tables(positions, inv_freq):
    b, s = positions.shape
    half = inv_freq.shape[0]
    per_row = LANES // half
    rows = b * s // per_row
    tile = min(rows, 512)
    pos_rep = jnp.repeat(positions.reshape(rows, per_row), half, axis=1)
    invf = jnp.tile(inv_freq, per_row).reshape(1, LANES)
    spec = pl.BlockSpec((tile, LANES), lambda i: (i, 0))
    cos, sin = pl.pallas_call(
        _rope_table_kernel,
        grid=(rows // tile,),
        in_specs=[spec, _const_spec(invf.shape)],
        out_specs=[spec, spec],
        out_shape=[jax.ShapeDtypeStruct((rows, LANES), F32)] * 2,
        compiler_params=pltpu.CompilerParams(dimension_semantics=("parallel",)),
        name="rope_tables",
    )(pos_rep, invf)
    expand = lambda t: jnp.tile(t.reshape(b, s, half), (1, 1, per_row))
    return expand(cos), expand(sin)


def _in_proj_kernel(x_ref, xh_ref, cos_ref, sin_ref, g_ref, w_ref, cw_ref, bias_ref,
                    qT_ref, k_ref, kmean_ref, vT_ref, qm_ref, kmT_ref, vm_ref, om_ref, gif_ref, gifT_ref,
                    ext_ref, act_ref):
    tm = SEQ_TILE
    aw, mw = ATTN_WIDTH, MLSTM_WIDTH
    j = pl.program_id(1)
    g = g_ref[...]
    h = _rms(x_ref[0], g).astype(BF16)

    cos = cos_ref[0]
    sin = sin_ref[0]
    lane = lax.broadcasted_iota(jnp.int32, (tm, LANES), 1)
    first_half = (lane % ATTN_HEAD_DIM) < (ATTN_HEAD_DIM // 2)
    sin_signed = jnp.where(first_half, -sin, sin)

    def rope(t):
        outs = []
        for c in range(aw // LANES):
            tg = t[:, c * LANES:(c + 1) * LANES]
            swapped = jnp.where(first_half, pltpu.roll(tg, LANES - 32, 1), pltpu.roll(tg, 32, 1))
            outs.append(tg * cos + swapped * sin_signed)
        return jnp.concatenate(outs, axis=1)

    qk = _dot(h, w_ref[:, 0:2 * aw])
    q = rope(qk[:, :aw]) * (LOG2_E / math.sqrt(ATTN_HEAD_DIM))
    k = rope(qk[:, aw:])
    qT_ref[0, 0] = q.T.astype(BF16)
    k_ref[0, 0] = k.astype(BF16)
    kmean_ref[0] = jnp.mean(k, axis=0, keepdims=True)
    v = _dot(h, w_ref[:, 2 * aw:3 * aw])
    vT_ref[0, 0] = v.T.astype(BF16)

    c0 = 3 * aw
    pm = _dot(h, w_ref[:, c0:c0 + 2 * mw])
    hh = _rms(xh_ref[0], g).astype(BF16)
    ph = jnp.where(j == 0, 0.0, _dot(hh, w_ref[:, c0:c0 + 2 * mw]))
    cw = cw_ref[...]
    nslab = 2 * mw // LANES
    for sl in range(nslab):
        cols = slice(sl * LANES, (sl + 1) * LANES)
        ext_ref[sl, 0:HALO, :] = ph[:, cols]
        ext_ref[sl, HALO:HALO + tm, :] = pm[:, cols]
    for sl in range(nslab):
        cws = cw[:, sl * LANES:(sl + 1) * LANES]
        for r0 in range(0, tm, 8 * CONV_STRIDE):
            for kk in range(CONV_STRIDE):
                row = HALO + r0 + kk
                conv = cws[0:1] * ext_ref[sl, pl.ds(row - 3, 8, stride=CONV_STRIDE), :]
                conv = conv + cws[1:2] * ext_ref[sl, pl.ds(row - 2, 8, stride=CONV_STRIDE), :]
                conv = conv + cws[2:3] * ext_ref[sl, pl.ds(row - 1, 8, stride=CONV_STRIDE), :]
                conv = conv + cws[3:4] * ext_ref[sl, pl.ds(row, 8, stride=CONV_STRIDE), :]
                act_ref[sl, pl.ds(r0 + kk, 8, stride=CONV_STRIDE), :] = conv * _sigmoid(conv)
    q_act = jnp.concatenate([act_ref[sl] for sl in range(nslab // 2)], axis=1)
    k_act = jnp.concatenate([act_ref[sl] for sl in range(nslab // 2, nslab)], axis=1)
    qm_ref[0] = (q_act * (1.0 / math.sqrt(MLSTM_HEAD_DIM))).astype(BF16)
    kmT_ref[0, 0] = k_act.T.astype(BF16)

    c1 = c0 + 2 * mw
    vo = _dot(h, w_ref[:, c1:c1 + 2 * mw])
    vm_ref[0] = vo[:, :mw].astype(BF16)
    om_ref[0] = vo[:, mw:]

    c2 = c1 + 2 * mw
    gi = _dot(h, w_ref[:, c2:c2 + LANES]) + bias_ref[...]
    log_sig = jnp.minimum(gi, 0.0) - jnp.log1p(jnp.exp(-jnp.abs(gi)))
    gt = jnp.where(lane < N_MLSTM_HEADS, gi, log_sig)
    gif_ref[0] = gt[:, 0:2 * N_MLSTM_HEADS]
    gifT_ref[0] = gt.T[0:2 * N_MLSTM_HEADS, :]


def _in_proj(x, cos, sin, g, w, cw, bias):
    b, s, d = x.shape
    tm = SEQ_TILE
    nb = s // tm
    aw, mw = ATTN_WIDTH, MLSTM_WIDTH
    nh2 = 2 * N_MLSTM_HEADS
    out_shape = [
        jax.ShapeDtypeStruct((b, nb, aw, tm), BF16),
        jax.ShapeDtypeStruct((b, nb, tm, aw), BF16),
        jax.ShapeDtypeStruct((b * nb, 1, aw), F32),
        jax.ShapeDtypeStruct((b, nb, aw, tm), BF16),
        jax.ShapeDtypeStruct((b, s, mw), BF16),
        jax.ShapeDtypeStruct((b, nb, mw, tm), BF16),
        jax.ShapeDtypeStruct((b, s, mw), BF16),
        jax.ShapeDtypeStruct((b, s, mw), F32),
        jax.ShapeDtypeStruct((b, s, nh2), F32),
        jax.ShapeDtypeStruct((b, nh2, s), F32),
    ]
    blk4 = lambda r, c: pl.BlockSpec((1, 1, r, c), lambda bi, ji: (bi, ji, 0, 0))
    row3 = lambda c: pl.BlockSpec((1, tm, c), lambda bi, ji: (bi, ji, 0))
    out_specs = [
        blk4(aw, tm), blk4(tm, aw),
        pl.BlockSpec((1, 1, aw), lambda bi, ji: (bi * nb + ji, 0, 0)),
        blk4(aw, tm), row3(mw), blk4(mw, tm), row3(mw), row3(mw), row3(nh2),
        pl.BlockSpec((1, nh2, tm), lambda bi, ji: (bi, 0, ji)),
    ]
    in_specs = [
        pl.BlockSpec((1, tm, d), lambda bi, ji: (bi, ji, 0)),
        pl.BlockSpec((1, HALO, d), lambda bi, ji: (bi, jnp.maximum(ji * (tm // HALO) - 1, 0), 0)),
        row3(LANES), row3(LANES),
        _const_spec(g.shape), _const_spec(w.shape), _const_spec(cw.shape), _const_spec(bias.shape),
    ]
    return pl.pallas_call(
        _in_proj_kernel,
        grid=(b, nb),
        in_specs=in_specs,
        out_specs=out_specs,
        out_shape=out_shape,
        scratch_shapes=[pltpu.VMEM((2 * mw // LANES, HALO + tm, LANES), F32),
                        pltpu.VMEM((2 * mw // LANES, tm, LANES), F32)],
        compiler_params=pltpu.CompilerParams(
            dimension_semantics=("parallel", "parallel"), vmem_limit_bytes=VMEM_LIMIT),
        name="in_proj",
    )(x, x, cos, sin, g, w, cw, bias)


def _moba_kernel(jtab_ref, ntab_ref, qT_ref, k_ref, vT_ref, kmean_ref, out_ref,
                 qm_ref, bias_ref, m_ref, l_ref, acc_ref, s_ref, cm_ref, p_ref, a_ref):
    blk = MOBA_BLOCK
    hd = ATTN_HEAD_DIM
    nb = k_ref.shape[1]
    nh = 2 * (qT_ref.shape[2] // LANES)
    n_units = nb * (nb - 1) // 2
    dim_row = lax.broadcasted_iota(jnp.int32, (LANES, blk), 0)
    blk_id = lax.broadcasted_iota(jnp.int32, (nb, blk), 0)
    key_row = lax.broadcasted_iota(jnp.int32, (blk, blk), 0)
    qry_col = lax.broadcasted_iota(jnp.int32, (blk, blk), 1)

    def lanes(h):
        return slice((h // 2) * LANES, (h // 2 + 1) * LANES)

    def head_rows(h):
        return slice(h * hd, (h + 1) * hd)

    def phase_a(j, carry):
        for h in range(nh):
            qT = qT_ref[0, j, lanes(h), :]
            qh = jnp.where((dim_row >= hd) == (h % 2 == 1), qT, jnp.zeros_like(qT))
            qm_ref[h, j] = qh
            km = kmean_ref[0, :, lanes(h)]
            km_hi = km.astype(BF16)
            km_lo = (km - km_hi.astype(F32)).astype(BF16)
            gate = _dot(km_hi, qh) + _dot(km_lo, qh)
            gate = jnp.where(blk_id < j, gate, NEG)
            rank = jnp.zeros((nb, blk), jnp.int32)
            for m in range(nb):
                gm = gate[m:m + 1, :]
                ahead = (gm > gate) | ((gm == gate) & (m < blk_id))
                rank = rank + ahead.astype(jnp.int32)
            selected = (rank < MOBA_TOPK) & (blk_id < j)
            bias_ref[h, j, 0:nb, :] = jnp.where(selected, 0.0, NEG)
            bias_ref[h, j, nb:nb + 1, :] = jnp.full((1, blk), NEG, F32)
            s_own = jnp.where(key_row <= qry_col, _dot(k_ref[0, j, :, lanes(h)], qh), NEG)
            m0 = jnp.max(s_own, axis=0, keepdims=True)
            p0 = jnp.exp2(s_own - m0)
            m_ref[h, j] = m0
            l_ref[h, j] = jnp.sum(p0, axis=0, keepdims=True)
            acc_ref[h, j] = _dot(vT_ref[0, j, head_rows(h), :], p0.astype(BF16))
        return carry

    lax.fori_loop(0, nb, phase_a, 0)

    for h in range(nh):
        s_ref[h] = jnp.zeros((blk, blk), F32)
        cm_ref[h] = jnp.zeros((1, blk), F32)
        p_ref[h] = jnp.zeros((blk, blk), BF16)
        a_ref[h] = jnp.ones((1, blk), F32)

    def phase_b(i, carry):
        j1, n1 = jtab_ref[i + 2], ntab_ref[i + 2]
        j2, n2 = jtab_ref[i + 1], ntab_ref[i + 1]
        j3, n3 = jtab_ref[i], ntab_ref[i]
        k_blk = jnp.minimum(n1, nb - 1)
        v_blk = jnp.minimum(n3, nb - 1)
        for h in range(nh):
            vTb = vT_ref[0, v_blk, head_rows(h), :]
            acc_ref[h, j3] = a_ref[h] * acc_ref[h, j3] + _dot(vTb, p_ref[h])
        for h in range(nh):
            bias = bias_ref[h, j2, pl.ds(n2, 1), :]
            m_old = m_ref[h, j2]
            m_new = jnp.maximum(m_old, cm_ref[h] + bias)
            alpha = jnp.exp2(m_old - m_new)
            p = jnp.exp2(s_ref[h] - (m_new - bias))
            l_ref[h, j2] = alpha * l_ref[h, j2] + jnp.sum(p, axis=0, keepdims=True)
            m_ref[h, j2] = m_new
            a_ref[h] = alpha
            p_ref[h] = p.astype(BF16)
        for h in range(nh):
            sT = _dot(k_ref[0, k_blk, :, lanes(h)], qm_ref[h, j1])
            s_ref[h] = sT
            cm_ref[h] = jnp.max(sT, axis=0, keepdims=True)
        return carry

    lax.fori_loop(0, n_units + 2, phase_b, 0)

    def phase_c(j, carry):
        row0 = pl.multiple_of(j * blk, blk)
        for pair in range(nh // 2):
            oT = jnp.concatenate([acc_ref[h, j] / l_ref[h, j] for h in (2 * pair, 2 * pair + 1)], axis=0)
            out_ref[0, pl.ds(row0, blk), pair * LANES:(pair + 1) * LANES] = oT.T.astype(BF16)
        return carry

    lax.fori_loop(0, nb, phase_c, 0)


def _moba_units(nb):
    pad = (0, nb)
    units = [pad, pad] + [(j, n) for j in range(nb) for n in range(j)] + [pad, pad]
    jtab = jnp.asarray([u[0] for u in units], jnp.int32)
    ntab = jnp.asarray([u[1] for u in units], jnp.int32)
    return jtab, ntab


def _moba(qT, k, vT, kmean):
    b, nb, aw, blk = qT.shape
    s = nb * blk
    w = MOBA_PAIRS_PER_STEP * LANES
    nh = 2 * MOBA_PAIRS_PER_STEP
    jtab, ntab = _moba_units(nb)
    grid_spec = pltpu.PrefetchScalarGridSpec(
        num_scalar_prefetch=2,
        grid=(b, aw // w),
        in_specs=[
            pl.BlockSpec((1, nb, w, blk), lambda bi, pi, *_: (bi, 0, pi, 0)),
            pl.BlockSpec((1, nb, blk, w), lambda bi, pi, *_: (bi, 0, 0, pi)),
            pl.BlockSpec((1, nb, w, blk), lambda bi, pi, *_: (bi, 0, pi, 0)),
            pl.BlockSpec((1, nb, w), lambda bi, pi, *_: (bi, 0, pi)),
        ],
        out_specs=pl.BlockSpec((1, s, w), lambda bi, pi, *_: (bi, 0, pi)),
        scratch_shapes=[pltpu.VMEM((nh, nb, LANES, blk), BF16),
                        pltpu.VMEM((nh, nb, nb + 8, blk), F32),
                        pltpu.VMEM((nh, nb, 1, blk), F32),
                        pltpu.VMEM((nh, nb, 1, blk), F32),
                        pltpu.VMEM((nh, nb, ATTN_HEAD_DIM, blk), F32),
                        pltpu.VMEM((nh, blk, blk), F32),
                        pltpu.VMEM((nh, 1, blk), F32),
                        pltpu.VMEM((nh, blk, blk), BF16),
                        pltpu.VMEM((nh, 1, blk), F32)],
    )
    return pl.pallas_call(
        _moba_kernel,
        grid_spec=grid_spec,
        out_shape=jax.ShapeDtypeStruct((b, s, aw), BF16),
        compiler_params=pltpu.CompilerParams(
            dimension_semantics=("parallel", "parallel"), vmem_limit_bytes=VMEM_LIMIT),
        name="moba",
    )(jtab, ntab, qT, k, vT, kmean)


def _split3(x):
    hi = x.astype(BF16)
    r = x - hi.astype(F32)
    mid = r.astype(BF16)
    lo = (r - mid.astype(F32)).astype(BF16)
    return hi, mid, lo


def _mlstm_kernel(q_ref, kT_ref, v_ref, o_ref, gif_ref, gifT_ref, g_ref, out_ref, c_ref, m_ref):
    L = SEQ_TILE
    nh, hd = N_MLSTM_HEADS, MLSTM_HEAD_DIM
    c = pl.program_id(1)

    @pl.when(c == 0)
    def _():
        c_ref[...] = jnp.zeros_like(c_ref)
        m_ref[...] = jnp.zeros_like(m_ref)

    t_idx = lax.broadcasted_iota(jnp.int32, (L, L), 0)
    s_idx = lax.broadcasted_iota(jnp.int32, (L, L), 1)
    causal = s_idx <= t_idx
    tril = jnp.where(causal, 1.0, 0.0).astype(BF16)
    triu = jnp.where(t_idx <= s_idx, 1.0, 0.0).astype(BF16)

    gi_col = gif_ref[0]
    gi_row = gifT_ref[0]
    b_col_all = sum(_dot(tril, part) for part in _split3(gi_col))
    b_row_all = sum(_dot(part, triu) for part in _split3(gi_row))
    ones_col = jnp.where(lax.broadcasted_iota(jnp.int32, (L, hd), 1) == 0, 1.0, 0.0).astype(BF16)
    gn = g_ref[...]

    for h in range(nh):
        a_row = gi_row[h:h + 1, :]
        b_row = b_row_all[nh + h:nh + h + 1, :]
        b_col = b_col_all[:, nh + h:nh + h + 1]
        q = q_ref[0, :, h * hd:(h + 1) * hd]
        kT = kT_ref[0, 0, h * hd:(h + 1) * hd, :]
        v_ext = jnp.concatenate([v_ref[0, :, h * hd:(h + 1) * hd], ones_col], axis=1)

        dmat = jnp.where(causal, b_col - b_row + a_row, NEG)
        mi = jnp.max(dmat, axis=1, keepdims=True)
        intra = _dot((jnp.exp(dmat - mi) * _dot(q, kT)).astype(BF16), v_ext)
        b_last = b_row[:, L - 1:L]
        dec = b_last - b_row + a_row
        md = jnp.max(dec, axis=1, keepdims=True)
        kv = _dot((kT.astype(F32) * jnp.exp(dec - md)).astype(BF16), v_ext)

        m_prev = m_ref[h]
        c_old = c_ref[h]
        inter = b_col + m_prev
        m_t = jnp.maximum(inter, mi)
        num_ext = jnp.exp(inter - m_t) * _dot(q, c_old.astype(BF16)) + jnp.exp(mi - m_t) * intra
        num = num_ext[:, :hd]
        den = num_ext[:, hd:hd + 1]
        h_out = num / jnp.maximum(jnp.abs(den), jnp.exp(-m_t))

        m_new = jnp.maximum(b_last + m_prev, md)
        c_ref[h] = jnp.exp(b_last + m_prev - m_new) * c_old + jnp.exp(md - m_new) * kv
        m_ref[h] = m_new

        hn = _rms(h_out, gn[:, h * hd:(h + 1) * hd])
        gate_o = _sigmoid(o_ref[0, :, h * hd:(h + 1) * hd])
        out_ref[0, :, h * hd:(h + 1) * hd] = (hn * gate_o).astype(BF16)


def _mlstm(qm, kmT, vm, om, gif, gifT, g):
    b, s, mw = qm.shape
    L = SEQ_TILE
    nc = s // L
    nh2 = 2 * N_MLSTM_HEADS
    row = lambda cdim: pl.BlockSpec((1, L, cdim), lambda bi, ci: (bi, ci, 0))
    return pl.pallas_call(
        _mlstm_kernel,
        grid=(b, nc),
        in_specs=[
            row(mw),
            pl.BlockSpec((1, 1, mw, L), lambda bi, ci: (bi, ci, 0, 0)),
            row(mw), row(mw), row(nh2),
            pl.BlockSpec((1, nh2, L), lambda bi, ci: (bi, 0, ci)),
            _const_spec(g.shape),
        ],
        out_specs=row(mw),
        out_shape=jax.ShapeDtypeStruct((b, s, mw), BF16),
        scratch_shapes=[pltpu.VMEM((N_MLSTM_HEADS, MLSTM_HEAD_DIM, 2 * MLSTM_HEAD_DIM), F32),
                        pltpu.VMEM((N_MLSTM_HEADS, 1, 1), F32)],
        compiler_params=pltpu.CompilerParams(
            dimension_semantics=("parallel", "arbitrary"), vmem_limit_bytes=VMEM_LIMIT),
        name="mlstm",
    )(qm, kmT, vm, om, gif, gifT, g)


def _mix_kernel(x_ref, a_ref, y_ref, g_ref, wg_ref, wa_ref, wm_ref, wo_ref, out_ref):
    d = x_ref.shape[-1]
    x = x_ref[...]
    h = _rms(x, g_ref[...]).astype(BF16)
    gate_a = _sigmoid(_dot(h, wg_ref[:, :d]))
    gate_m = _sigmoid(_dot(h, wg_ref[:, d:]))
    merged = gate_a * _dot(a_ref[...], wa_ref[...]) + gate_m * _dot(y_ref[...], wm_ref[...])
    out_ref[...] = x + _dot(merged.astype(BF16), wo_ref[...])


def _mix(x2d, a2d, y2d, g, wg, wa, wm, wo):
    t, d = x2d.shape
    tm = MIX_TILE
    row = lambda cdim: pl.BlockSpec((tm, cdim), lambda i: (i, 0))
    return pl.pallas_call(
        _mix_kernel,
        grid=(t // tm,),
        in_specs=[row(d), row(a2d.shape[1]), row(y2d.shape[1]), _const_spec(g.shape),
                  _const_spec(wg.shape), _const_spec(wa.shape), _const_spec(wm.shape), _const_spec(wo.shape)],
        out_specs=row(d),
        out_shape=jax.ShapeDtypeStruct((t, d), F32),
        compiler_params=pltpu.CompilerParams(
            dimension_semantics=("parallel",), vmem_limit_bytes=VMEM_LIMIT),
        name="mix",
    )(x2d, a2d, y2d, g, wg, wa, wm, wo)


def _ffn_kernel(x_ref, xh_ref, g_ref, wg_ref, wu_ref, cg_ref, cu_ref, wd_ref, gf_ref, out_ref,
                hext_ref, ug_ref, uu_ref, act_ref, acc_ref):
    tm = FFN_TILE
    nchunk = wd_ref.shape[0]
    nslab = ug_ref.shape[0]
    j = pl.program_id(1)
    x = x_ref[0]
    g = g_ref[...]
    hext_ref[HALO:HALO + tm, :] = _rms(x, g).astype(BF16)
    hh = _rms(xh_ref[0], g)
    hext_ref[0:HALO, :] = jnp.where(j == 0, 0.0, hh).astype(BF16)
    acc_ref[...] = x

    def stage_a(c):
        cols = pl.ds(pl.multiple_of(c * FFN_CHUNK, FFN_CHUNK), FFN_CHUNK)
        ug = _dot(hext_ref[...], wg_ref[:, cols])
        uu = _dot(hext_ref[...], wu_ref[:, cols])
        for sl in range(nslab):
            ug_ref[sl] = ug[:, sl * LANES:(sl + 1) * LANES]
            uu_ref[sl] = uu[:, sl * LANES:(sl + 1) * LANES]

    def taps(u_ref, sl, row, cw):
        out = cw[0:1] * u_ref[sl, pl.ds(HALO + row - 2, 8, stride=CONV_STRIDE), :]
        out = out + cw[1:2] * u_ref[sl, pl.ds(HALO + row - 1, 8, stride=CONV_STRIDE), :]
        return out + cw[2:3] * u_ref[sl, pl.ds(HALO + row, 8, stride=CONV_STRIDE), :]

    def stage_b(c):
        for sl in range(nslab):
            cg = cg_ref[c][:, sl * LANES:(sl + 1) * LANES]
            cu = cu_ref[c][:, sl * LANES:(sl + 1) * LANES]
            for r0 in range(0, tm, 8 * CONV_STRIDE):
                for k in range(CONV_STRIDE):
                    gate = taps(ug_ref, sl, r0 + k, cg)
                    up = taps(uu_ref, sl, r0 + k, cu)
                    act_ref[sl, pl.ds(r0 + k, 8, stride=CONV_STRIDE), :] = gate * _sigmoid(gate) * up

    def stage_c(c):
        act = jnp.concatenate([act_ref[sl] for sl in range(nslab)], axis=1).astype(BF16)
        acc_ref[...] += _dot(act, wd_ref[c])

    stage_a(0)
    stage_b(0)
    stage_a(1)

    def body(c, carry):
        stage_c(c - 2)
        stage_b(c - 1)
        stage_a(c)
        return carry

    lax.fori_loop(2, nchunk, body, 0)
    stage_c(nchunk - 2)
    stage_b(nchunk - 1)
    stage_c(nchunk - 1)
    out_ref[0] = _rms(acc_ref[...], gf_ref[...])


def _ffn(x1, g2, wg, wu, cg, cu, wd, gf):
    b, s, d = x1.shape
    tm = FFN_TILE
    return pl.pallas_call(
        _ffn_kernel,
        grid=(b, s // tm),
        in_specs=[
            pl.BlockSpec((1, tm, d), lambda bi, ji: (bi, ji, 0)),
            pl.BlockSpec((1, HALO, d), lambda bi, ji: (bi, jnp.maximum(ji * (tm // HALO) - 1, 0), 0)),
            _const_spec(g2.shape), _const_spec(wg.shape), _const_spec(wu.shape),
            _const_spec(cg.shape), _const_spec(cu.shape), _const_spec(wd.shape), _const_spec(gf.shape),
        ],
        out_specs=pl.BlockSpec((1, tm, d), lambda bi, ji: (bi, ji, 0)),
        out_shape=jax.ShapeDtypeStruct((b, s, d), F32),
        scratch_shapes=[pltpu.VMEM((HALO + tm, d), BF16),
                        pltpu.VMEM((FFN_CHUNK // LANES, HALO + tm, LANES), F32),
                        pltpu.VMEM((FFN_CHUNK // LANES, HALO + tm, LANES), F32),
                        pltpu.VMEM((FFN_CHUNK // LANES, tm, LANES), F32),
                        pltpu.VMEM((tm, d), F32)],
        compiler_params=pltpu.CompilerParams(
            dimension_semantics=("parallel", "parallel"), vmem_limit_bytes=VMEM_LIMIT),
        name="ffn",
    )(x1, x1, g2, wg, wu, cg, cu, wd, gf)


def _chunk_cols(w, chunk):
    kdim, n = w.shape
    return w.reshape(kdim, n // chunk, chunk).transpose(1, 0, 2)


def kernel(x, positions, norm_mix_g, w_in, conv_mlstm, i_bias, f_bias, mlstm_norm_g, w_branch_attn,
           w_branch_mlstm, w_out, norm_ffn_g, w_up, conv_ffn, w_down, norm_final_g):
    b, s, d = x.shape
    aw, mw, nh = ATTN_WIDTH, MLSTM_WIDTH, N_MLSTM_HEADS
    assert s % FFN_TILE == 0 and s % SEQ_TILE == 0 and d % LANES == 0
    n_qkv = 3 * aw + 4 * mw
    d_ff = w_down.shape[1]
    assert d_ff % FFN_CHUNK == 0
    half = ATTN_HEAD_DIM // 2
    inv_freq = ROPE_THETA ** (-(jnp.arange(half, dtype=F32) / half))
    cos, sin = _rope_tables(positions, inv_freq)

    for layer in range(w_in.shape[0]):
        wl = w_in[layer]
        w1 = jnp.concatenate(
            [wl[:, :n_qkv + 2 * nh], jnp.zeros((d, LANES - 2 * nh), wl.dtype)], axis=1).astype(BF16)
        wg = wl[:, n_qkv + 2 * nh:].astype(BF16)
        bias = jnp.concatenate(
            [i_bias[layer], f_bias[layer], jnp.zeros((LANES - 2 * nh,), F32)]).reshape(1, LANES)

        qT, k, kmean, vT, qm, kmT, vm, om, gif, gifT = _in_proj(
            x, cos, sin, norm_mix_g[layer].reshape(1, d), w1, conv_mlstm[layer], bias)
        attn = _moba(qT, k, vT, kmean.reshape(b, s // MOBA_BLOCK, aw))
        y_m = _mlstm(qm, kmT, vm, om, gif, gifT, mlstm_norm_g[layer].reshape(1, mw))
        x1 = _mix(x.reshape(b * s, d), attn.reshape(b * s, aw), y_m.reshape(b * s, mw),
                  norm_mix_g[layer].reshape(1, d), wg, w_branch_attn[layer].astype(BF16),
                  w_branch_mlstm[layer].astype(BF16), w_out[layer].astype(BF16)).reshape(b, s, d)

        wu = w_up[layer].astype(BF16)
        cf = conv_ffn[layer]
        last = layer == w_in.shape[0] - 1
        gfin = norm_final_g.reshape(1, d) if last else None
        assert last, "only the final layer fuses the output norm"
        x = _ffn(x1, norm_ffn_g[layer].reshape(1, d),
                 wu[:, :d_ff], wu[:, d_ff:],
                 _chunk_cols(cf[:, :d_ff], FFN_CHUNK), _chunk_cols(cf[:, d_ff:], FFN_CHUNK),
                 w_down[layer].astype(BF16).reshape(d_ff // FFN_CHUNK, FFN_CHUNK, d), gfin)
    return x
```

```python
import functools
import math

import jax
import jax.numpy as jnp
from jax import lax
from jax.experimental import pallas as pl
from jax.experimental.pallas import tpu as pltpu

F32 = jnp.float32
BF16 = jnp.bfloat16

EPS = 1e-6
NEG = -1e30
ROPE_THETA = 10000.0

N_ATTN_HEADS = 8
ATTN_HEAD_DIM = 64
ATTN_WIDTH = N_ATTN_HEADS * ATTN_HEAD_DIM
MOBA_BLOCK = 256
MOBA_TOPK = 3
N_MLSTM_HEADS = 4
MLSTM_HEAD_DIM = 128
MLSTM_WIDTH = N_MLSTM_HEADS * MLSTM_HEAD_DIM

LANES = 128
HALO = 16
SEQ_TILE = MOBA_BLOCK
MIX_TILE = 512
FFN_TILE = 512
FFN_CHUNK = 256
CONV_STRIDE = 4
MOBA_PAIRS_PER_STEP = 4
LOG2_E = math.log2(math.e)
VMEM_LIMIT = 56 * 1024 * 1024


def _dot(a, b):
    return jnp.dot(a, b, preferred_element_type=F32)


def _rms(x, g):
    ms = jnp.mean(x * x, axis=-1, keepdims=True)
    return x * lax.rsqrt(ms + EPS) * g


def _sigmoid(x):
    return 1.0 / (1.0 + jnp.exp(-x))


def _const_spec(shape):
    nd = len(shape)
    return pl.BlockSpec(shape, lambda *_: (0,) * nd, pipeline_mode=pl.Buffered(1))


def _rope_table_kernel(pos_ref, invf_ref, cos_ref, sin_ref):
    rows, per_row = pos_ref.shape
    half = LANES // per_row
    lane = lax.broadcasted_iota(jnp.int32, (rows, LANES), 1)
    pos = pos_ref[...]
    pos_rep = pos[:, per_row - 1:per_row]
    for g in range(per_row - 2, -1, -1):
        pos_rep = jnp.where(lane < (g + 1) * half, pos[:, g:g + 1], pos_rep)
    ang = pos_rep.astype(F32) * invf_ref[...]
    for table, out_ref in ((jnp.cos(ang), cos_ref), (jnp.sin(ang), sin_ref)):
        for g in range(per_row):
            base = table if g == 0 else pltpu.roll(table, LANES - g * half, 1)
            width = half
            while width < LANES:
                base = jnp.where(lane < width, base, pltpu.roll(base, width, 1))
                width *= 2
            out_ref[pl.ds(g, rows, stride=per_row), :] = base


def _rope_tables(positions, inv_freq):
    b, s = positions.shape
    half = inv_freq.shape[0]
    per_row = LANES // half
    rows = b * s // per_row
    tile = min(rows, 512)
    invf = jnp.tile(inv_freq, per_row).reshape(1, LANES)
    out_spec = pl.BlockSpec((tile * per_row, LANES), lambda i: (i, 0))
    cos, sin = pl.pallas_call(
        _rope_table_kernel,
        grid=(rows // tile,),
        in_specs=[pl.BlockSpec((tile, per_row), lambda i: (i, 0)), _const_spec(invf.shape)],
        out_specs=[out_spec, out_spec],
        out_shape=[jax.ShapeDtypeStruct((b * s, LANES), F32)] * 2,
        compiler_params=pltpu.CompilerParams(dimension_semantics=("parallel",)),
        name="rope_tables",
    )(positions.reshape(rows, per_row), invf)
    return cos.reshape(b, s, LANES), sin.reshape(b, s, LANES)


def _in_proj_kernel(x_ref, xh_ref, cos_ref, sin_ref, g_ref, w_ref, cw_ref, bias_ref,
                    qT_ref, k_ref, kmean_ref, vT_ref, qm_ref, kmT_ref, vm_ref, om_ref, gif_ref, gifT_ref,
                    ext_ref, act_ref):
    tm = SEQ_TILE
    aw, mw = ATTN_WIDTH, MLSTM_WIDTH
    j = pl.program_id(1)
    g = g_ref[...]
    h = _rms(x_ref[0], g).astype(BF16)

    cos = cos_ref[0]
    sin = sin_ref[0]
    lane = lax.broadcasted_iota(jnp.int32, (tm, LANES), 1)
    first_half = (lane % ATTN_HEAD_DIM) < (ATTN_HEAD_DIM // 2)
    sin_signed = jnp.where(first_half, -sin, sin)

    def rope(t):
        outs = []
        for c in range(aw // LANES):
            tg = t[:, c * LANES:(c + 1) * LANES]
            swapped = jnp.where(first_half, pltpu.roll(tg, LANES - 32, 1), pltpu.roll(tg, 32, 1))
            outs.append(tg * cos + swapped * sin_signed)
        return jnp.concatenate(outs, axis=1)

    qk = _dot(h, w_ref[:, 0:2 * aw])
    q = rope(qk[:, :aw]) * (LOG2_E / math.sqrt(ATTN_HEAD_DIM))
    k = rope(qk[:, aw:])
    qT_ref[0, 0] = q.T.astype(BF16)
    k_ref[0, 0] = k.astype(BF16)
    kmean_ref[0] = jnp.mean(k, axis=0, keepdims=True)
    v = _dot(h, w_ref[:, 2 * aw:3 * aw])
    vT_ref[0, 0] = v.T.astype(BF16)

    c0 = 3 * aw
    pm = _dot(h, w_ref[:, c0:c0 + 2 * mw])
    hh = _rms(xh_ref[0], g).astype(BF16)
    ph = jnp.where(j == 0, 0.0, _dot(hh, w_ref[:, c0:c0 + 2 * mw]))
    cw = cw_ref[...]
    nslab = 2 * mw // LANES
    for sl in range(nslab):
        cols = slice(sl * LANES, (sl + 1) * LANES)
        ext_ref[sl, 0:HALO, :] = ph[:, cols]
        ext_ref[sl, HALO:HALO + tm, :] = pm[:, cols]
    for sl in range(nslab):
        cws = cw[:, sl * LANES:(sl + 1) * LANES]
        for r0 in range(0, tm, 8 * CONV_STRIDE):
            for kk in range(CONV_STRIDE):
                row = HALO + r0 + kk
                conv = cws[0:1] * ext_ref[sl, pl.ds(row - 3, 8, stride=CONV_STRIDE), :]
                conv = conv + cws[1:2] * ext_ref[sl, pl.ds(row - 2, 8, stride=CONV_STRIDE), :]
                conv = conv + cws[2:3] * ext_ref[sl, pl.ds(row - 1, 8, stride=CONV_STRIDE), :]
                conv = conv + cws[3:4] * ext_ref[sl, pl.ds(row, 8, stride=CONV_STRIDE), :]
                act_ref[sl, pl.ds(r0 + kk, 8, stride=CONV_STRIDE), :] = conv * _sigmoid(conv)
    q_act = jnp.concatenate([act_ref[sl] for sl in range(nslab // 2)], axis=1)
    k_act = jnp.concatenate([act_ref[sl] for sl in range(nslab // 2, nslab)], axis=1)
    qm_ref[0] = (q_act * (1.0 / math.sqrt(MLSTM_HEAD_DIM))).astype(BF16)
    kmT_ref[0, 0] = k_act.T.astype(BF16)

    c1 = c0 + 2 * mw
    vo = _dot(h, w_ref[:, c1:c1 + 2 * mw])
    vm_ref[0] = vo[:, :mw].astype(BF16)
    om_ref[0] = vo[:, mw:]

    c2 = c1 + 2 * mw
    gi = _dot(h, w_ref[:, c2:c2 + LANES]) + bias_ref[...]
    log_sig = jnp.minimum(gi, 0.0) - jnp.log1p(jnp.exp(-jnp.abs(gi)))
    gt = jnp.where(lane < N_MLSTM_HEADS, gi, log_sig)
    gif_ref[0] = gt[:, 0:2 * N_MLSTM_HEADS]
    gifT_ref[0] = gt.T[0:2 * N_MLSTM_HEADS, :]


def _in_proj(x, cos, sin, g, w, cw, bias):
    b, s, d = x.shape
    tm = SEQ_TILE
    nb = s // tm
    aw, mw = ATTN_WIDTH, MLSTM_WIDTH
    nh2 = 2 * N_MLSTM_HEADS
    out_shape = [
        jax.ShapeDtypeStruct((b, nb, aw, tm), BF16),
        jax.ShapeDtypeStruct((b, nb, tm, aw), BF16),
        jax.ShapeDtypeStruct((b * nb, 1, aw), F32),
        jax.ShapeDtypeStruct((b, nb, aw, tm), BF16),
        jax.ShapeDtypeStruct((b, s, mw), BF16),
        jax.ShapeDtypeStruct((b, nb, mw, tm), BF16),
        jax.ShapeDtypeStruct((b, s, mw), BF16),
        jax.ShapeDtypeStruct((b, s, mw), F32),
        jax.ShapeDtypeStruct((b, s, nh2), F32),
        jax.ShapeDtypeStruct((b, nh2, s), F32),
    ]
    blk4 = lambda r, c: pl.BlockSpec((1, 1, r, c), lambda bi, ji: (bi, ji, 0, 0))
    row3 = lambda c: pl.BlockSpec((1, tm, c), lambda bi, ji: (bi, ji, 0))
    out_specs = [
        blk4(aw, tm), blk4(tm, aw),
        pl.BlockSpec((1, 1, aw), lambda bi, ji: (bi * nb + ji, 0, 0)),
        blk4(aw, tm), row3(mw), blk4(mw, tm), row3(mw), row3(mw), row3(nh2),
        pl.BlockSpec((1, nh2, tm), lambda bi, ji: (bi, 0, ji)),
    ]
    in_specs = [
        pl.BlockSpec((1, tm, d), lambda bi, ji: (bi, ji, 0)),
        pl.BlockSpec((1, HALO, d), lambda bi, ji: (bi, jnp.maximum(ji * (tm // HALO) - 1, 0), 0)),
        row3(LANES), row3(LANES),
        _const_spec(g.shape), _const_spec(w.shape), _const_spec(cw.shape), _const_spec(bias.shape),
    ]
    return pl.pallas_call(
        _in_proj_kernel,
        grid=(b, nb),
        in_specs=in_specs,
        out_specs=out_specs,
        out_shape=out_shape,
        scratch_shapes=[pltpu.VMEM((2 * mw // LANES, HALO + tm, LANES), F32),
                        pltpu.VMEM((2 * mw // LANES, tm, LANES), F32)],
        compiler_params=pltpu.CompilerParams(
            dimension_semantics=("parallel", "parallel"), vmem_limit_bytes=VMEM_LIMIT),
        name="in_proj",
    )(x, x, cos, sin, g, w, cw, bias)


def _moba_kernel(jtab_ref, ntab_ref, qT_ref, k_ref, vT_ref, kmean_ref, out_ref,
                 qm_ref, bias_ref, m_ref, l_ref, acc_ref, s_ref, cm_ref, p_ref, a_ref):
    blk = MOBA_BLOCK
    hd = ATTN_HEAD_DIM
    nb = k_ref.shape[1]
    nh = 2 * (qT_ref.shape[2] // LANES)
    n_units = nb * (nb - 1) // 2
    dim_row = lax.broadcasted_iota(jnp.int32, (LANES, blk), 0)
    blk_id = lax.broadcasted_iota(jnp.int32, (nb, blk), 0)
    key_row = lax.broadcasted_iota(jnp.int32, (blk, blk), 0)
    qry_col = lax.broadcasted_iota(jnp.int32, (blk, blk), 1)

    def lanes(h):
        return slice((h // 2) * LANES, (h // 2 + 1) * LANES)

    def head_rows(h):
        return slice(h * hd, (h + 1) * hd)

    def phase_a(j, carry):
        for h in range(nh):
            qT = qT_ref[0, j, lanes(h), :]
            qh = jnp.where((dim_row >= hd) == (h % 2 == 1), qT, jnp.zeros_like(qT))
            qm_ref[h, j] = qh
            km = kmean_ref[0, :, lanes(h)]
            km_hi = km.astype(BF16)
            km_lo = (km - km_hi.astype(F32)).astype(BF16)
            gate = _dot(km_hi, qh) + _dot(km_lo, qh)
            gate = jnp.where(blk_id < j, gate, NEG)
            rank = jnp.zeros((nb, blk), jnp.int32)
            for m in range(nb):
                gm = gate[m:m + 1, :]
                ahead = (gm > gate) | ((gm == gate) & (m < blk_id))
                rank = rank + ahead.astype(jnp.int32)
            selected = (rank < MOBA_TOPK) & (blk_id < j)
            bias_ref[h, j, 0:nb, :] = jnp.where(selected, 0.0, NEG)
            bias_ref[h, j, nb:nb + 1, :] = jnp.full((1, blk), NEG, F32)
            s_own = jnp.where(key_row <= qry_col, _dot(k_ref[0, j, :, lanes(h)], qh), NEG)
            m0 = jnp.max(s_own, axis=0, keepdims=True)
            p0 = jnp.exp2(s_own - m0)
            m_ref[h, j] = m0
            l_ref[h, j] = jnp.sum(p0, axis=0, keepdims=True)
            acc_ref[h, j] = _dot(vT_ref[0, j, head_rows(h), :], p0.astype(BF16))
        return carry

    lax.fori_loop(0, nb, phase_a, 0)

    for h in range(nh):
        s_ref[h] = jnp.zeros((blk, blk), F32)
        cm_ref[h] = jnp.zeros((1, blk), F32)
        p_ref[h] = jnp.zeros((blk, blk), BF16)
        a_ref[h] = jnp.ones((1, blk), F32)

    def phase_b(i, carry):
        j1, n1 = jtab_ref[i + 2], ntab_ref[i + 2]
        j2, n2 = jtab_ref[i + 1], ntab_ref[i + 1]
        j3, n3 = jtab_ref[i], ntab_ref[i]
        k_blk = jnp.minimum(n1, nb - 1)
        v_blk = jnp.minimum(n3, nb - 1)
        for h in range(nh):
            vTb = vT_ref[0, v_blk, head_rows(h), :]
            acc_ref[h, j3] = a_ref[h] * acc_ref[h, j3] + _dot(vTb, p_ref[h])
        for h in range(nh):
            bias = bias_ref[h, j2, pl.ds(n2, 1), :]
            m_old = m_ref[h, j2]
            m_new = jnp.maximum(m_old, cm_ref[h] + bias)
            alpha = jnp.exp2(m_old - m_new)
            p = jnp.exp2(s_ref[h] - (m_new - bias))
            l_ref[h, j2] = alpha * l_ref[h, j2] + jnp.sum(p, axis=0, keepdims=True)
            m_ref[h, j2] = m_new
            a_ref[h] = alpha
            p_ref[h] = p.astype(BF16)
        for h in range(nh):
            sT = _dot(k_ref[0, k_blk, :, lanes(h)], qm_ref[h, j1])
            s_ref[h] = sT
            cm_ref[h] = jnp.max(sT, axis=0, keepdims=True)
        return carry

    lax.fori_loop(0, n_units + 2, phase_b, 0)

    def phase_c(j, carry):
        row0 = pl.multiple_of(j * blk, blk)
        for pair in range(nh // 2):
            oT = jnp.concatenate([acc_ref[h, j] / l_ref[h, j] for h in (2 * pair, 2 * pair + 1)], axis=0)
            out_ref[0, pl.ds(row0, blk), pair * LANES:(pair + 1) * LANES] = oT.T.astype(BF16)
        return carry

    lax.fori_loop(0, nb, phase_c, 0)


def _moba_units(nb):
    pad = (0, nb)
    units = [pad, pad] + [(j, n) for j in range(nb) for n in range(j)] + [pad, pad]
    jtab = jnp.asarray([u[0] for u in units], jnp.int32)
    ntab = jnp.asarray([u[1] for u in units], jnp.int32)
    return jtab, ntab


def _moba(qT, k, vT, kmean):
    b, nb, aw, blk = qT.shape
    s = nb * blk
    w = MOBA_PAIRS_PER_STEP * LANES
    nh = 2 * MOBA_PAIRS_PER_STEP
    jtab, ntab = _moba_units(nb)
    grid_spec = pltpu.PrefetchScalarGridSpec(
        num_scalar_prefetch=2,
        grid=(b, aw // w),
        in_specs=[
            pl.BlockSpec((1, nb, w, blk), lambda bi, pi, *_: (bi, 0, pi, 0)),
            pl.BlockSpec((1, nb, blk, w), lambda bi, pi, *_: (bi, 0, 0, pi)),
            pl.BlockSpec((1, nb, w, blk), lambda bi, pi, *_: (bi, 0, pi, 0)),
            pl.BlockSpec((1, nb, w), lambda bi, pi, *_: (bi, 0, pi)),
        ],
        out_specs=pl.BlockSpec((1, s, w), lambda bi, pi, *_: (bi, 0, pi)),
        scratch_shapes=[pltpu.VMEM((nh, nb, LANES, blk), BF16),
                        pltpu.VMEM((nh, nb, nb + 8, blk), F32),
                        pltpu.VMEM((nh, nb, 1, blk), F32),
                        pltpu.VMEM((nh, nb, 1, blk), F32),
                        pltpu.VMEM((nh, nb, ATTN_HEAD_DIM, blk), F32),
                        pltpu.VMEM((nh, blk, blk), F32),
                        pltpu.VMEM((nh, 1, blk), F32),
                        pltpu.VMEM((nh, blk, blk), BF16),
                        pltpu.VMEM((nh, 1, blk), F32)],
    )
    return pl.pallas_call(
        _moba_kernel,
        grid_spec=grid_spec,
        out_shape=jax.ShapeDtypeStruct((b, s, aw), BF16),
        compiler_params=pltpu.CompilerParams(
            dimension_semantics=("parallel", "parallel"), vmem_limit_bytes=VMEM_LIMIT),
        name="moba",
    )(jtab, ntab, qT, k, vT, kmean)


def _split3(x):
    hi = x.astype(BF16)
    r = x - hi.astype(F32)
    mid = r.astype(BF16)
    lo = (r - mid.astype(F32)).astype(BF16)
    return hi, mid, lo


def _mlstm_kernel(q_ref, kT_ref, v_ref, o_ref, gif_ref, gifT_ref, g_ref, out_ref, c_ref, m_ref):
    L = SEQ_TILE
    nh, hd = N_MLSTM_HEADS, MLSTM_HEAD_DIM
    c = pl.program_id(1)

    @pl.when(c == 0)
    def _():
        c_ref[...] = jnp.zeros_like(c_ref)
        m_ref[...] = jnp.zeros_like(m_ref)

    t_idx = lax.broadcasted_iota(jnp.int32, (L, L), 0)
    s_idx = lax.broadcasted_iota(jnp.int32, (L, L), 1)
    causal = s_idx <= t_idx
    tril = jnp.where(causal, 1.0, 0.0).astype(BF16)
    triu = jnp.where(t_idx <= s_idx, 1.0, 0.0).astype(BF16)

    gi_col = gif_ref[0]
    gi_row = gifT_ref[0]
    b_col_all = sum(_dot(tril, part) for part in _split3(gi_col))
    b_row_all = sum(_dot(part, triu) for part in _split3(gi_row))
    ones_col = jnp.where(lax.broadcasted_iota(jnp.int32, (L, hd), 1) == 0, 1.0, 0.0).astype(BF16)
    gn = g_ref[...]

    for h in range(nh):
        a_row = gi_row[h:h + 1, :]
        b_row = b_row_all[nh + h:nh + h + 1, :]
        b_col = b_col_all[:, nh + h:nh + h + 1]
        q = q_ref[0, :, h * hd:(h + 1) * hd]
        kT = kT_ref[0, 0, h * hd:(h + 1) * hd, :]
        v_ext = jnp.concatenate([v_ref[0, :, h * hd:(h + 1) * hd], ones_col], axis=1)

        dmat = jnp.where(causal, b_col - b_row + a_row, NEG)
        mi = jnp.max(dmat, axis=1, keepdims=True)
        intra = _dot((jnp.exp(dmat - mi) * _dot(q, kT)).astype(BF16), v_ext)
        b_last = b_row[:, L - 1:L]
        dec = b_last - b_row + a_row
        md = jnp.max(dec, axis=1, keepdims=True)
        kv = _dot((kT.astype(F32) * jnp.exp(dec - md)).astype(BF16), v_ext)

        m_prev = m_ref[h]
        c_old = c_ref[h]
        inter = b_col + m_prev
        m_t = jnp.maximum(inter, mi)
        num_ext = jnp.exp(inter - m_t) * _dot(q, c_old.astype(BF16)) + jnp.exp(mi - m_t) * intra
        num = num_ext[:, :hd]
        den = num_ext[:, hd:hd + 1]
        h_out = num / jnp.maximum(jnp.abs(den), jnp.exp(-m_t))

        m_new = jnp.maximum(b_last + m_prev, md)
        c_ref[h] = jnp.exp(b_last + m_prev - m_new) * c_old + jnp.exp(md - m_new) * kv
        m_ref[h] = m_new

        hn = _rms(h_out, gn[:, h * hd:(h + 1) * hd])
        gate_o = _sigmoid(o_ref[0, :, h * hd:(h + 1) * hd])
        out_ref[0, :, h * hd:(h + 1) * hd] = (hn * gate_o).astype(BF16)


def _mlstm(qm, kmT, vm, om, gif, gifT, g):
    b, s, mw = qm.shape
    L = SEQ_TILE
    nc = s // L
    nh2 = 2 * N_MLSTM_HEADS
    row = lambda cdim: pl.BlockSpec((1, L, cdim), lambda bi, ci: (bi, ci, 0))
    return pl.pallas_call(
        _mlstm_kernel,
        grid=(b, nc),
        in_specs=[
            row(mw),
            pl.BlockSpec((1, 1, mw, L), lambda bi, ci: (bi, ci, 0, 0)),
            row(mw), row(mw), row(nh2),
            pl.BlockSpec((1, nh2, L), lambda bi, ci: (bi, 0, ci)),
            _const_spec(g.shape),
        ],
        out_specs=row(mw),
        out_shape=jax.ShapeDtypeStruct((b, s, mw), BF16),
        scratch_shapes=[pltpu.VMEM((N_MLSTM_HEADS, MLSTM_HEAD_DIM, 2 * MLSTM_HEAD_DIM), F32),
                        pltpu.VMEM((N_MLSTM_HEADS, 1, 1), F32)],
        compiler_params=pltpu.CompilerParams(
            dimension_semantics=("parallel", "arbitrary"), vmem_limit_bytes=VMEM_LIMIT),
        name="mlstm",
    )(qm, kmT, vm, om, gif, gifT, g)


def _mix_kernel(x_ref, a_ref, y_ref, g_ref, wg_ref, wa_ref, wm_ref, wo_ref, out_ref):
    d = x_ref.shape[-1]
    x = x_ref[...]
    h = _rms(x, g_ref[...]).astype(BF16)
    gate_a = _sigmoid(_dot(h, wg_ref[:, :d]))
    gate_m = _sigmoid(_dot(h, wg_ref[:, d:]))
    merged = gate_a * _dot(a_ref[...], wa_ref[...]) + gate_m * _dot(y_ref[...], wm_ref[...])
    out_ref[...] = x + _dot(merged.astype(BF16), wo_ref[...])


def _mix(x2d, a2d, y2d, g, wg, wa, wm, wo):
    t, d = x2d.shape
    tm = MIX_TILE
    row = lambda cdim: pl.BlockSpec((tm, cdim), lambda i: (i, 0))
    return pl.pallas_call(
        _mix_kernel,
        grid=(t // tm,),
        in_specs=[row(d), row(a2d.shape[1]), row(y2d.shape[1]), _const_spec(g.shape),
                  _const_spec(wg.shape), _const_spec(wa.shape), _const_spec(wm.shape), _const_spec(wo.shape)],
        out_specs=row(d),
        out_shape=jax.ShapeDtypeStruct((t, d), F32),
        compiler_params=pltpu.CompilerParams(
            dimension_semantics=("parallel",), vmem_limit_bytes=VMEM_LIMIT),
        name="mix",
    )(x2d, a2d, y2d, g, wg, wa, wm, wo)


def _ffn_kernel(x_ref, xh_ref, g_ref, wup_ref, cg_ref, cu_ref, wd_ref, gf_ref, out_ref,
                hext_ref, ug_ref, uu_ref, act_ref, acc_ref):
    tm = FFN_TILE
    nchunk = wd_ref.shape[0]
    nslab = ug_ref.shape[0]
    j = pl.program_id(1)
    x = x_ref[0]
    g = g_ref[...]
    hext_ref[HALO:HALO + tm, :] = _rms(x, g).astype(BF16)
    hh = _rms(xh_ref[0], g)
    hext_ref[0:HALO, :] = jnp.where(j == 0, 0.0, hh).astype(BF16)
    acc_ref[...] = x

    def stage_a(c):
        gate_cols = pl.ds(pl.multiple_of(c * FFN_CHUNK, LANES), FFN_CHUNK)
        up_cols = pl.ds(pl.multiple_of(nchunk * FFN_CHUNK + c * FFN_CHUNK, LANES), FFN_CHUNK)
        ug = _dot(hext_ref[...], wup_ref[:, gate_cols])
        uu = _dot(hext_ref[...], wup_ref[:, up_cols])
        for sl in range(nslab):
            ug_ref[sl] = ug[:, sl * LANES:(sl + 1) * LANES]
            uu_ref[sl] = uu[:, sl * LANES:(sl + 1) * LANES]

    def taps(u_ref, sl, row, cw):
        out = cw[0:1] * u_ref[sl, pl.ds(HALO + row - 2, 8, stride=CONV_STRIDE), :]
        out = out + cw[1:2] * u_ref[sl, pl.ds(HALO + row - 1, 8, stride=CONV_STRIDE), :]
        return out + cw[2:3] * u_ref[sl, pl.ds(HALO + row, 8, stride=CONV_STRIDE), :]

    def stage_b(c):
        for sl in range(nslab):
            cg = cg_ref[c][:, sl * LANES:(sl + 1) * LANES]
            cu = cu_ref[c][:, sl * LANES:(sl + 1) * LANES]
            for r0 in range(0, tm, 8 * CONV_STRIDE):
                for k in range(CONV_STRIDE):
                    gate = taps(ug_ref, sl, r0 + k, cg)
                    up = taps(uu_ref, sl, r0 + k, cu)
                    act_ref[sl, pl.ds(r0 + k, 8, stride=CONV_STRIDE), :] = gate * _sigmoid(gate) * up

    def stage_c(c):
        act = jnp.concatenate([act_ref[sl] for sl in range(nslab)], axis=1).astype(BF16)
        acc_ref[...] += _dot(act, wd_ref[c])

    stage_a(0)
    stage_b(0)
    stage_a(1)

    def body(c, carry):
        stage_c(c - 2)
        stage_b(c - 1)
        stage_a(c)
        return carry

    lax.fori_loop(2, nchunk, body, 0)
    stage_c(nchunk - 2)
    stage_b(nchunk - 1)
    stage_c(nchunk - 1)
    out_ref[0] = _rms(acc_ref[...], gf_ref[...])


def _ffn(x1, g2, wup, cg, cu, wd, gf):
    b, s, d = x1.shape
    tm = FFN_TILE
    return pl.pallas_call(
        _ffn_kernel,
        grid=(b, s // tm),
        in_specs=[
            pl.BlockSpec((1, tm, d), lambda bi, ji: (bi, ji, 0)),
            pl.BlockSpec((1, HALO, d), lambda bi, ji: (bi, jnp.maximum(ji * (tm // HALO) - 1, 0), 0)),
            _const_spec(g2.shape), _const_spec(wup.shape),
            _const_spec(cg.shape), _const_spec(cu.shape), _const_spec(wd.shape), _const_spec(gf.shape),
        ],
        out_specs=pl.BlockSpec((1, tm, d), lambda bi, ji: (bi, ji, 0)),
        out_shape=jax.ShapeDtypeStruct((b, s, d), F32),
        scratch_shapes=[pltpu.VMEM((HALO + tm, d), BF16),
                        pltpu.VMEM((FFN_CHUNK // LANES, HALO + tm, LANES), F32),
                        pltpu.VMEM((FFN_CHUNK // LANES, HALO + tm, LANES), F32),
                        pltpu.VMEM((FFN_CHUNK // LANES, tm, LANES), F32),
                        pltpu.VMEM((tm, d), F32)],
        compiler_params=pltpu.CompilerParams(
            dimension_semantics=("parallel", "parallel"), vmem_limit_bytes=VMEM_LIMIT),
        name="ffn",
    )(x1, x1, g2, wup, cg, cu, wd, gf)


def _chunk_cols(w, chunk):
    kdim, n = w.shape
    return w.reshape(kdim, n // chunk, chunk).transpose(1, 0, 2)


def kernel(x, positions, norm_mix_g, w_in, conv_mlstm, i_bias, f_bias, mlstm_norm_g, w_branch_attn,
           w_branch_mlstm, w_out, norm_ffn_g, w_up, conv_ffn, w_down, norm_final_g):
    b, s, d = x.shape
    aw, mw, nh = ATTN_WIDTH, MLSTM_WIDTH, N_MLSTM_HEADS
    assert s % FFN_TILE == 0 and s % SEQ_TILE == 0 and d % LANES == 0
    n_qkv = 3 * aw + 4 * mw
    d_ff = w_down.shape[1]
    assert d_ff % FFN_CHUNK == 0
    half = ATTN_HEAD_DIM // 2
    inv_freq = ROPE_THETA ** (-(jnp.arange(half, dtype=F32) / half))
    cos, sin = _rope_tables(positions, inv_freq)

    for layer in range(w_in.shape[0]):
        wl = w_in[layer]
        w1 = jnp.concatenate(
            [wl[:, :n_qkv].astype(BF16), wl[:, n_qkv:n_qkv + 2 * nh].astype(BF16),
             jnp.zeros((d, LANES - 2 * nh), BF16)], axis=1)
        wg = wl[:, n_qkv + 2 * nh:].astype(BF16)
        bias = jnp.concatenate(
            [i_bias[layer], f_bias[layer], jnp.zeros((LANES - 2 * nh,), F32)]).reshape(1, LANES)

        qT, k, kmean, vT, qm, kmT, vm, om, gif, gifT = _in_proj(
            x, cos, sin, norm_mix_g[layer].reshape(1, d), w1, conv_mlstm[layer], bias)
        attn = _moba(qT, k, vT, kmean.reshape(b, s // MOBA_BLOCK, aw))
        y_m = _mlstm(qm, kmT, vm, om, gif, gifT, mlstm_norm_g[layer].reshape(1, mw))
        x1 = _mix(x.reshape(b * s, d), attn.reshape(b * s, aw), y_m.reshape(b * s, mw),
                  norm_mix_g[layer].reshape(1, d), wg, w_branch_attn[layer].astype(BF16),
                  w_branch_mlstm[layer].astype(BF16), w_out[layer].astype(BF16)).reshape(b, s, d)

        wu = w_up[layer].astype(BF16)
        cf = conv_ffn[layer]
        last = layer == w_in.shape[0] - 1
        gfin = norm_final_g.reshape(1, d) if last else None
        assert last, "only the final layer fuses the output norm"
        x = _ffn(x1, norm_ffn_g[layer].reshape(1, d), wu,
                 _chunk_cols(cf[:, :d_ff], FFN_CHUNK), _chunk_cols(cf[:, d_ff:], FFN_CHUNK),
                 w_down[layer].astype(BF16).reshape(d_ff // FFN_CHUNK, FFN_CHUNK, d), gfin)
    return x
```

```python
import functools
import math

import jax
import jax.numpy as jnp
from jax import lax
from jax.experimental import pallas as pl
from jax.experimental.pallas import tpu as pltpu

F32 = jnp.float32
BF16 = jnp.bfloat16

EPS = 1e-6
NEG = -1e30
ROPE_THETA = 10000.0

N_ATTN_HEADS = 8
ATTN_HEAD_DIM = 64
ATTN_WIDTH = N_ATTN_HEADS * ATTN_HEAD_DIM
MOBA_BLOCK = 256
MOBA_TOPK = 3
N_MLSTM_HEADS = 4
MLSTM_HEAD_DIM = 128
MLSTM_WIDTH = N_MLSTM_HEADS * MLSTM_HEAD_DIM

LANES = 128
HALO = 16
SEQ_TILE = MOBA_BLOCK
MIX_TILE = 512
FFN_TILE = 512
FFN_CHUNK = 256
CONV_STRIDE = 4
MLSTM_ROWS = 1
MOBA_PAIRS_PER_STEP = 4
LOG2_E = math.log2(math.e)
VMEM_LIMIT = 56 * 1024 * 1024


def _dot(a, b):
    return jnp.dot(a, b, preferred_element_type=F32)


def _rms(x, g):
    ms = jnp.mean(x * x, axis=-1, keepdims=True)
    return x * lax.rsqrt(ms + EPS) * g


def _sigmoid(x):
    return 1.0 / (1.0 + jnp.exp(-x))


def _conv_block(u_ref, sl, r0, cw):
    ntap = cw.shape[0]
    rows = {k: u_ref[sl, pl.ds(HALO + r0 + k, 8, stride=CONV_STRIDE), :] for k in range(1 - ntap, CONV_STRIDE)}
    outs = []
    for k in range(CONV_STRIDE):
        out = cw[0:1] * rows[k - ntap + 1]
        for j in range(1, ntap):
            out = out + cw[j:j + 1] * rows[k - ntap + 1 + j]
        outs.append(out)
    return outs


def _const_spec(shape):
    nd = len(shape)
    return pl.BlockSpec(shape, lambda *_: (0,) * nd, pipeline_mode=pl.Buffered(1))


def _rope_table_kernel(pos_ref, invf_ref, cos_ref, sin_ref):
    rows, per_row = pos_ref.shape
    half = LANES // per_row
    lane = lax.broadcasted_iota(jnp.int32, (rows, LANES), 1)
    pos = pos_ref[...]
    pos_rep = pos[:, per_row - 1:per_row]
    for g in range(per_row - 2, -1, -1):
        pos_rep = jnp.where(lane < (g + 1) * half, pos[:, g:g + 1], pos_rep)
    ang = pos_rep.astype(F32) * invf_ref[...]
    for table, out_ref in ((jnp.cos(ang), cos_ref), (jnp.sin(ang), sin_ref)):
        for g in range(per_row):
            base = table if g == 0 else pltpu.roll(table, LANES - g * half, 1)
            width = half
            while width < LANES:
                base = jnp.where(lane < width, base, pltpu.roll(base, width, 1))
                width *= 2
            out_ref[pl.ds(g, rows, stride=per_row), :] = base


def _rope_tables(positions, inv_freq):
    b, s = positions.shape
    half = inv_freq.shape[0]
    per_row = LANES // half
    rows = b * s // per_row
    tile = min(rows, 512)
    invf = jnp.tile(inv_freq, per_row).reshape(1, LANES)
    out_spec = pl.BlockSpec((tile * per_row, LANES), lambda i: (i, 0))
    cos, sin = pl.pallas_call(
        _rope_table_kernel,
        grid=(rows // tile,),
        in_specs=[pl.BlockSpec((tile, per_row), lambda i: (i, 0)), _const_spec(invf.shape)],
        out_specs=[out_spec, out_spec],
        out_shape=[jax.ShapeDtypeStruct((b * s, LANES), F32)] * 2,
        compiler_params=pltpu.CompilerParams(dimension_semantics=("parallel",)),
        name="rope_tables",
    )(positions.reshape(rows, per_row), invf)
    return cos.reshape(b, s, LANES), sin.reshape(b, s, LANES)


def _in_proj_kernel(x_ref, xh_ref, cos_ref, sin_ref, g_ref, w_ref, cw_ref, bias_ref,
                    qT_ref, k_ref, kmean_ref, vT_ref, qm_ref, kmT_ref, vm_ref, om_ref, gif_ref, gifT_ref,
                    ext_ref, act_ref):
    tm = SEQ_TILE
    aw, mw = ATTN_WIDTH, MLSTM_WIDTH
    j = pl.program_id(1)
    g = g_ref[...]
    h = _rms(x_ref[0], g).astype(BF16)

    cos = cos_ref[0]
    sin = sin_ref[0]
    lane = lax.broadcasted_iota(jnp.int32, (tm, LANES), 1)
    first_half = (lane % ATTN_HEAD_DIM) < (ATTN_HEAD_DIM // 2)
    sin_signed = jnp.where(first_half, -sin, sin)

    def rope(t):
        outs = []
        for c in range(aw // LANES):
            tg = t[:, c * LANES:(c + 1) * LANES]
            swapped = jnp.where(first_half, pltpu.roll(tg, LANES - 32, 1), pltpu.roll(tg, 32, 1))
            outs.append(tg * cos + swapped * sin_signed)
        return jnp.concatenate(outs, axis=1)

    qk = _dot(h, w_ref[:, 0:2 * aw])
    q = rope(qk[:, :aw]) * (LOG2_E / math.sqrt(ATTN_HEAD_DIM))
    k = rope(qk[:, aw:])
    qT_ref[0, 0] = q.T.astype(BF16)
    k_ref[0, 0] = k.astype(BF16)
    kmean_ref[0] = jnp.mean(k, axis=0, keepdims=True)
    v = _dot(h, w_ref[:, 2 * aw:3 * aw])
    vT_ref[0, 0] = v.T.astype(BF16)

    c0 = 3 * aw
    pm = _dot(h, w_ref[:, c0:c0 + 2 * mw])
    hh = _rms(xh_ref[0], g).astype(BF16)
    ph = jnp.where(j == 0, 0.0, _dot(hh, w_ref[:, c0:c0 + 2 * mw]))
    cw = cw_ref[...]
    nslab = 2 * mw // LANES
    for sl in range(nslab):
        cols = slice(sl * LANES, (sl + 1) * LANES)
        ext_ref[sl, 0:HALO, :] = ph[:, cols]
        ext_ref[sl, HALO:HALO + tm, :] = pm[:, cols]
    for sl in range(nslab):
        cws = cw[:, sl * LANES:(sl + 1) * LANES]
        for r0 in range(0, tm, 8 * CONV_STRIDE):
            for kk, conv in enumerate(_conv_block(ext_ref, sl, r0, cws)):
                act_ref[sl, pl.ds(r0 + kk, 8, stride=CONV_STRIDE), :] = conv * _sigmoid(conv)
    q_act = jnp.concatenate([act_ref[sl] for sl in range(nslab // 2)], axis=1)
    k_act = jnp.concatenate([act_ref[sl] for sl in range(nslab // 2, nslab)], axis=1)
    qm_ref[0] = (q_act * (1.0 / math.sqrt(MLSTM_HEAD_DIM))).astype(BF16)
    kmT_ref[0, 0] = k_act.T.astype(BF16)

    c1 = c0 + 2 * mw
    vo = _dot(h, w_ref[:, c1:c1 + 2 * mw])
    vm_ref[0] = vo[:, :mw].astype(BF16)
    om_ref[0] = vo[:, mw:]

    c2 = c1 + 2 * mw
    gi = _dot(h, w_ref[:, c2:c2 + LANES]) + bias_ref[...]
    log_sig = jnp.minimum(gi, 0.0) - jnp.log1p(jnp.exp(-jnp.abs(gi)))
    gt = jnp.where(lane < N_MLSTM_HEADS, gi, log_sig)
    gif_ref[0] = gt[:, 0:2 * N_MLSTM_HEADS]
    gifT_ref[0] = gt.T[0:2 * N_MLSTM_HEADS, :]


def _in_proj(x, cos, sin, g, w, cw, bias):
    b, s, d = x.shape
    tm = SEQ_TILE
    nb = s // tm
    aw, mw = ATTN_WIDTH, MLSTM_WIDTH
    nh2 = 2 * N_MLSTM_HEADS
    out_shape = [
        jax.ShapeDtypeStruct((b, nb, aw, tm), BF16),
        jax.ShapeDtypeStruct((b, nb, tm, aw), BF16),
        jax.ShapeDtypeStruct((b * nb, 1, aw), F32),
        jax.ShapeDtypeStruct((b, nb, aw, tm), BF16),
        jax.ShapeDtypeStruct((b, s, mw), BF16),
        jax.ShapeDtypeStruct((b, nb, mw, tm), BF16),
        jax.ShapeDtypeStruct((b, s, mw), BF16),
        jax.ShapeDtypeStruct((b, s, mw), F32),
        jax.ShapeDtypeStruct((b, s, nh2), F32),
        jax.ShapeDtypeStruct((b, nh2, s), F32),
    ]
    blk4 = lambda r, c: pl.BlockSpec((1, 1, r, c), lambda bi, ji: (bi, ji, 0, 0))
    row3 = lambda c: pl.BlockSpec((1, tm, c), lambda bi, ji: (bi, ji, 0))
    out_specs = [
        blk4(aw, tm), blk4(tm, aw),
        pl.BlockSpec((1, 1, aw), lambda bi, ji: (bi * nb + ji, 0, 0)),
        blk4(aw, tm), row3(mw), blk4(mw, tm), row3(mw), row3(mw), row3(nh2),
        pl.BlockSpec((1, nh2, tm), lambda bi, ji: (bi, 0, ji)),
    ]
    in_specs = [
        pl.BlockSpec((1, tm, d), lambda bi, ji: (bi, ji, 0)),
        pl.BlockSpec((1, HALO, d), lambda bi, ji: (bi, jnp.maximum(ji * (tm // HALO) - 1, 0), 0)),
        row3(LANES), row3(LANES),
        _const_spec(g.shape), _const_spec(w.shape), _const_spec(cw.shape), _const_spec(bias.shape),
    ]
    return pl.pallas_call(
        _in_proj_kernel,
        grid=(b, nb),
        in_specs=in_specs,
        out_specs=out_specs,
        out_shape=out_shape,
        scratch_shapes=[pltpu.VMEM((2 * mw // LANES, HALO + tm, LANES), F32),
                        pltpu.VMEM((2 * mw // LANES, tm, LANES), F32)],
        compiler_params=pltpu.CompilerParams(
            dimension_semantics=("parallel", "parallel"), vmem_limit_bytes=VMEM_LIMIT),
        name="in_proj",
    )(x, x, cos, sin, g, w, cw, bias)


def _moba_kernel(jtab_ref, ntab_ref, qT_ref, k_ref, vT_ref, kmean_ref, out_ref,
                 qm_ref, bias_ref, m_ref, l_ref, acc_ref, s_ref, cm_ref, p_ref, a_ref):
    blk = MOBA_BLOCK
    hd = ATTN_HEAD_DIM
    nb = k_ref.shape[1]
    nh = 2 * (qT_ref.shape[2] // LANES)
    n_units = nb * (nb - 1) // 2
    dim_row = lax.broadcasted_iota(jnp.int32, (LANES, blk), 0)
    blk_id = lax.broadcasted_iota(jnp.int32, (nb, blk), 0)
    key_row = lax.broadcasted_iota(jnp.int32, (blk, blk), 0)
    qry_col = lax.broadcasted_iota(jnp.int32, (blk, blk), 1)

    def lanes(h):
        return slice((h // 2) * LANES, (h // 2 + 1) * LANES)

    def head_rows(h):
        return slice(h * hd, (h + 1) * hd)

    def phase_a(j, carry):
        for h in range(nh):
            qT = qT_ref[0, j, lanes(h), :]
            qh = jnp.where((dim_row >= hd) == (h % 2 == 1), qT, jnp.zeros_like(qT))
            qm_ref[h, j] = qh
            km = kmean_ref[0, :, lanes(h)]
            km_hi = km.astype(BF16)
            km_lo = (km - km_hi.astype(F32)).astype(BF16)
            gate = _dot(km_hi, qh) + _dot(km_lo, qh)
            gate = jnp.where(blk_id < j, gate, NEG)
            rank = jnp.zeros((nb, blk), jnp.int32)
            for m in range(nb):
                gm = gate[m:m + 1, :]
                ahead = (gm > gate) | ((gm == gate) & (m < blk_id))
                rank = rank + ahead.astype(jnp.int32)
            selected = (rank < MOBA_TOPK) & (blk_id < j)
            bias_ref[h, j, 0:nb, :] = jnp.where(selected, 0.0, NEG)
            bias_ref[h, j, nb:nb + 1, :] = jnp.full((1, blk), NEG, F32)
            s_own = jnp.where(key_row <= qry_col, _dot(k_ref[0, j, :, lanes(h)], qh), NEG)
            m0 = jnp.max(s_own, axis=0, keepdims=True)
            p0 = jnp.exp2(s_own - m0)
            m_ref[h, j] = m0
            l_ref[h, j] = jnp.sum(p0, axis=0, keepdims=True)
            acc_ref[h, j] = _dot(vT_ref[0, j, head_rows(h), :], p0.astype(BF16))
        return carry

    lax.fori_loop(0, nb, phase_a, 0)

    for h in range(nh):
        s_ref[h] = jnp.zeros((blk, blk), F32)
        cm_ref[h] = jnp.zeros((1, blk), F32)
        p_ref[h] = jnp.zeros((blk, blk), BF16)
        a_ref[h] = jnp.ones((1, blk), F32)

    def phase_b(i, carry):
        j1, n1 = jtab_ref[i + 2], ntab_ref[i + 2]
        j2, n2 = jtab_ref[i + 1], ntab_ref[i + 1]
        j3, n3 = jtab_ref[i], ntab_ref[i]
        k_blk = jnp.minimum(n1, nb - 1)
        v_blk = jnp.minimum(n3, nb - 1)
        for h in range(nh):
            vTb = vT_ref[0, v_blk, head_rows(h), :]
            acc_ref[h, j3] = a_ref[h] * acc_ref[h, j3] + _dot(vTb, p_ref[h])
        for h in range(nh):
            bias = bias_ref[h, j2, pl.ds(n2, 1), :]
            m_old = m_ref[h, j2]
            m_new = jnp.maximum(m_old, cm_ref[h] + bias)
            alpha = jnp.exp2(m_old - m_new)
            p = jnp.exp2(s_ref[h] - (m_new - bias))
            l_ref[h, j2] = alpha * l_ref[h, j2] + jnp.sum(p, axis=0, keepdims=True)
            m_ref[h, j2] = m_new
            a_ref[h] = alpha
            p_ref[h] = p.astype(BF16)
        for h in range(nh):
            sT = _dot(k_ref[0, k_blk, :, lanes(h)], qm_ref[h, j1])
            s_ref[h] = sT
            cm_ref[h] = jnp.max(sT, axis=0, keepdims=True)
        return carry

    lax.fori_loop(0, n_units + 2, phase_b, 0)

    def phase_c(j, carry):
        row0 = pl.multiple_of(j * blk, blk)
        for pair in range(nh // 2):
            oT = jnp.concatenate([acc_ref[h, j] / l_ref[h, j] for h in (2 * pair, 2 * pair + 1)], axis=0)
            out_ref[0, pl.ds(row0, blk), pair * LANES:(pair + 1) * LANES] = oT.T.astype(BF16)
        return carry

    lax.fori_loop(0, nb, phase_c, 0)


def _moba_units(nb):
    pad = (0, nb)
    units = [pad, pad] + [(j, n) for j in range(nb) for n in range(j)] + [pad, pad]
    jtab = jnp.asarray([u[0] for u in units], jnp.int32)
    ntab = jnp.asarray([u[1] for u in units], jnp.int32)
    return jtab, ntab


def _moba(qT, k, vT, kmean):
    b, nb, aw, blk = qT.shape
    s = nb * blk
    w = MOBA_PAIRS_PER_STEP * LANES
    nh = 2 * MOBA_PAIRS_PER_STEP
    jtab, ntab = _moba_units(nb)
    grid_spec = pltpu.PrefetchScalarGridSpec(
        num_scalar_prefetch=2,
        grid=(b, aw // w),
        in_specs=[
            pl.BlockSpec((1, nb, w, blk), lambda bi, pi, *_: (bi, 0, pi, 0)),
            pl.BlockSpec((1, nb, blk, w), lambda bi, pi, *_: (bi, 0, 0, pi)),
            pl.BlockSpec((1, nb, w, blk), lambda bi, pi, *_: (bi, 0, pi, 0)),
            pl.BlockSpec((1, nb, w), lambda bi, pi, *_: (bi, 0, pi)),
        ],
        out_specs=pl.BlockSpec((1, s, w), lambda bi, pi, *_: (bi, 0, pi)),
        scratch_shapes=[pltpu.VMEM((nh, nb, LANES, blk), BF16),
                        pltpu.VMEM((nh, nb, nb + 8, blk), F32),
                        pltpu.VMEM((nh, nb, 1, blk), F32),
                        pltpu.VMEM((nh, nb, 1, blk), F32),
                        pltpu.VMEM((nh, nb, ATTN_HEAD_DIM, blk), F32),
                        pltpu.VMEM((nh, blk, blk), F32),
                        pltpu.VMEM((nh, 1, blk), F32),
                        pltpu.VMEM((nh, blk, blk), BF16),
                        pltpu.VMEM((nh, 1, blk), F32)],
    )
    return pl.pallas_call(
        _moba_kernel,
        grid_spec=grid_spec,
        out_shape=jax.ShapeDtypeStruct((b, s, aw), BF16),
        compiler_params=pltpu.CompilerParams(
            dimension_semantics=("parallel", "parallel"), vmem_limit_bytes=VMEM_LIMIT),
        name="moba",
    )(jtab, ntab, qT, k, vT, kmean)


def _split3(x):
    hi = x.astype(BF16)
    r = x - hi.astype(F32)
    mid = r.astype(BF16)
    lo = (r - mid.astype(F32)).astype(BF16)
    return hi, mid, lo


def _mlstm_kernel(q_ref, kT_ref, v_ref, o_ref, gif_ref, gifT_ref, g_ref, out_ref, c_ref, m_ref):
    L = SEQ_TILE
    nh, hd = N_MLSTM_HEADS, MLSTM_HEAD_DIM
    c = pl.program_id(1)

    @pl.when(c == 0)
    def _():
        c_ref[...] = jnp.zeros_like(c_ref)
        m_ref[...] = jnp.zeros_like(m_ref)

    t_idx = lax.broadcasted_iota(jnp.int32, (L, L), 0)
    s_idx = lax.broadcasted_iota(jnp.int32, (L, L), 1)
    causal = s_idx <= t_idx
    tril = jnp.where(causal, 1.0, 0.0).astype(BF16)
    triu = jnp.where(t_idx <= s_idx, 1.0, 0.0).astype(BF16)

    ones_col = jnp.where(lax.broadcasted_iota(jnp.int32, (L, hd), 1) == 0, 1.0, 0.0).astype(BF16)
    gn = g_ref[...]

    gates = []
    for r in range(q_ref.shape[0]):
        gi_col = gif_ref[r]
        gi_row = gifT_ref[r]
        b_col_all = sum(_dot(tril, part) for part in _split3(gi_col))
        b_row_all = sum(_dot(part, triu) for part in _split3(gi_row))
        gates.append((gi_row, b_col_all, b_row_all))

    for r, h in [(r, h) for r in range(q_ref.shape[0]) for h in range(nh)]:
        gi_row, b_col_all, b_row_all = gates[r]
        a_row = gi_row[h:h + 1, :]
        b_row = b_row_all[nh + h:nh + h + 1, :]
        b_col = b_col_all[:, nh + h:nh + h + 1]
        q = q_ref[r, :, h * hd:(h + 1) * hd]
        kT = kT_ref[r, 0, h * hd:(h + 1) * hd, :]
        v_ext = jnp.concatenate([v_ref[r, :, h * hd:(h + 1) * hd], ones_col], axis=1)

        dmat = jnp.where(causal, b_col - b_row + a_row, NEG)
        mi = jnp.max(dmat, axis=1, keepdims=True)
        intra = _dot((jnp.exp(dmat - mi) * _dot(q, kT)).astype(BF16), v_ext)
        b_last = b_row[:, L - 1:L]
        dec = b_last - b_row + a_row
        md = jnp.max(dec, axis=1, keepdims=True)
        kv = _dot((kT.astype(F32) * jnp.exp(dec - md)).astype(BF16), v_ext)

        m_prev = m_ref[r * nh + h]
        c_old = c_ref[r * nh + h]
        inter = b_col + m_prev
        m_t = jnp.maximum(inter, mi)
        num_ext = jnp.exp(inter - m_t) * _dot(q, c_old.astype(BF16)) + jnp.exp(mi - m_t) * intra
        num = num_ext[:, :hd]
        den = num_ext[:, hd:hd + 1]
        h_out = num / jnp.maximum(jnp.abs(den), jnp.exp(-m_t))

        m_new = jnp.maximum(b_last + m_prev, md)
        c_ref[r * nh + h] = jnp.exp(b_last + m_prev - m_new) * c_old + jnp.exp(md - m_new) * kv
        m_ref[r * nh + h] = m_new

        hn = _rms(h_out, gn[:, h * hd:(h + 1) * hd])
        gate_o = _sigmoid(o_ref[r, :, h * hd:(h + 1) * hd])
        out_ref[r, :, h * hd:(h + 1) * hd] = (hn * gate_o).astype(BF16)


def _mlstm(qm, kmT, vm, om, gif, gifT, g):
    b, s, mw = qm.shape
    L = SEQ_TILE
    nc = s // L
    nh2 = 2 * N_MLSTM_HEADS
    rows = MLSTM_ROWS if b % MLSTM_ROWS == 0 else 1
    row = lambda cdim: pl.BlockSpec((rows, L, cdim), lambda bi, ci: (bi, ci, 0))
    return pl.pallas_call(
        _mlstm_kernel,
        grid=(b // rows, nc),
        in_specs=[
            row(mw),
            pl.BlockSpec((rows, 1, mw, L), lambda bi, ci: (bi, ci, 0, 0)),
            row(mw), row(mw), row(nh2),
            pl.BlockSpec((rows, nh2, L), lambda bi, ci: (bi, 0, ci)),
            _const_spec(g.shape),
        ],
        out_specs=row(mw),
        out_shape=jax.ShapeDtypeStruct((b, s, mw), BF16),
        scratch_shapes=[pltpu.VMEM((rows * N_MLSTM_HEADS, MLSTM_HEAD_DIM, 2 * MLSTM_HEAD_DIM), F32),
                        pltpu.VMEM((rows * N_MLSTM_HEADS, 1, 1), F32)],
        compiler_params=pltpu.CompilerParams(
            dimension_semantics=("parallel", "arbitrary"), vmem_limit_bytes=VMEM_LIMIT),
        name="mlstm",
    )(qm, kmT, vm, om, gif, gifT, g)


def _mix_kernel(x_ref, a_ref, y_ref, g_ref, wg_ref, wa_ref, wm_ref, wo_ref, out_ref):
    d = x_ref.shape[-1]
    x = x_ref[...]
    h = _rms(x, g_ref[...]).astype(BF16)
    gate_a = _sigmoid(_dot(h, wg_ref[:, :d]))
    gate_m = _sigmoid(_dot(h, wg_ref[:, d:]))
    merged = gate_a * _dot(a_ref[...], wa_ref[...]) + gate_m * _dot(y_ref[...], wm_ref[...])
    out_ref[...] = x + _dot(merged.astype(BF16), wo_ref[...])


def _mix(x2d, a2d, y2d, g, wg, wa, wm, wo):
    t, d = x2d.shape
    tm = MIX_TILE
    row = lambda cdim: pl.BlockSpec((tm, cdim), lambda i: (i, 0))
    return pl.pallas_call(
        _mix_kernel,
        grid=(t // tm,),
        in_specs=[row(d), row(a2d.shape[1]), row(y2d.shape[1]), _const_spec(g.shape),
                  _const_spec(wg.shape), _const_spec(wa.shape), _const_spec(wm.shape), _const_spec(wo.shape)],
        out_specs=row(d),
        out_shape=jax.ShapeDtypeStruct((t, d), F32),
        compiler_params=pltpu.CompilerParams(
            dimension_semantics=("parallel",), vmem_limit_bytes=VMEM_LIMIT),
        name="mix",
    )(x2d, a2d, y2d, g, wg, wa, wm, wo)


def _ffn_kernel(x_ref, xh_ref, g_ref, wup_ref, cg_ref, cu_ref, wd_ref, gf_ref, out_ref,
                hext_ref, ug_ref, uu_ref, act_ref, actall_ref):
    tm = FFN_TILE
    nslab = ug_ref.shape[0]
    nchunk = actall_ref.shape[1] // FFN_CHUNK
    j = pl.program_id(1)
    x = x_ref[0]
    g = g_ref[...]
    hext_ref[HALO:HALO + tm, :] = _rms(x, g).astype(BF16)
    hh = _rms(xh_ref[0], g)
    hext_ref[0:HALO, :] = jnp.where(j == 0, 0.0, hh).astype(BF16)

    def stage_a(c):
        gate_cols = pl.ds(pl.multiple_of(c * FFN_CHUNK, LANES), FFN_CHUNK)
        up_cols = pl.ds(pl.multiple_of(nchunk * FFN_CHUNK + c * FFN_CHUNK, LANES), FFN_CHUNK)
        ug = _dot(hext_ref[...], wup_ref[:, gate_cols])
        uu = _dot(hext_ref[...], wup_ref[:, up_cols])
        for sl in range(nslab):
            ug_ref[sl] = ug[:, sl * LANES:(sl + 1) * LANES]
            uu_ref[sl] = uu[:, sl * LANES:(sl + 1) * LANES]

    def stage_b(c):
        for sl in range(nslab):
            cg = cg_ref[c][:, sl * LANES:(sl + 1) * LANES]
            cu = cu_ref[c][:, sl * LANES:(sl + 1) * LANES]
            for r0 in range(0, tm, 8 * CONV_STRIDE):
                gates = _conv_block(ug_ref, sl, r0, cg)
                ups = _conv_block(uu_ref, sl, r0, cu)
                for k in range(CONV_STRIDE):
                    act_ref[sl, pl.ds(r0 + k, 8, stride=CONV_STRIDE), :] = gates[k] * _sigmoid(gates[k]) * ups[k]
        act = jnp.concatenate([act_ref[sl] for sl in range(nslab)], axis=1)
        actall_ref[:, pl.ds(pl.multiple_of(c * FFN_CHUNK, LANES), FFN_CHUNK)] = act.astype(BF16)

    stage_a(0)

    def body(c, carry):
        stage_b(c - 1)
        stage_a(c)
        return carry

    lax.fori_loop(1, nchunk, body, 0)
    stage_b(nchunk - 1)
    out_ref[0] = _rms(x + _dot(actall_ref[...], wd_ref[...]), gf_ref[...])


def _ffn(x1, g2, wup, cg, cu, wd, gf):
    b, s, d = x1.shape
    tm = FFN_TILE
    return pl.pallas_call(
        _ffn_kernel,
        grid=(b, s // tm),
        in_specs=[
            pl.BlockSpec((1, tm, d), lambda bi, ji: (bi, ji, 0)),
            pl.BlockSpec((1, HALO, d), lambda bi, ji: (bi, jnp.maximum(ji * (tm // HALO) - 1, 0), 0)),
            _const_spec(g2.shape), _const_spec(wup.shape),
            _const_spec(cg.shape), _const_spec(cu.shape), _const_spec(wd.shape), _const_spec(gf.shape),
        ],
        out_specs=pl.BlockSpec((1, tm, d), lambda bi, ji: (bi, ji, 0)),
        out_shape=jax.ShapeDtypeStruct((b, s, d), F32),
        scratch_shapes=[pltpu.VMEM((HALO + tm, d), BF16),
                        pltpu.VMEM((FFN_CHUNK // LANES, HALO + tm, LANES), F32),
                        pltpu.VMEM((FFN_CHUNK // LANES, HALO + tm, LANES), F32),
                        pltpu.VMEM((FFN_CHUNK // LANES, tm, LANES), F32),
                        pltpu.VMEM((tm, wd.shape[0]), BF16)],
        compiler_params=pltpu.CompilerParams(
            dimension_semantics=("parallel", "parallel"), vmem_limit_bytes=VMEM_LIMIT),
        name="ffn",
    )(x1, x1, g2, wup, cg, cu, wd, gf)


def _chunk_cols(w, chunk):
    kdim, n = w.shape
    return w.reshape(kdim, n // chunk, chunk).transpose(1, 0, 2)


def kernel(x, positions, norm_mix_g, w_in, conv_mlstm, i_bias, f_bias, mlstm_norm_g, w_branch_attn,
           w_branch_mlstm, w_out, norm_ffn_g, w_up, conv_ffn, w_down, norm_final_g):
    b, s, d = x.shape
    aw, mw, nh = ATTN_WIDTH, MLSTM_WIDTH, N_MLSTM_HEADS
    assert s % FFN_TILE == 0 and s % SEQ_TILE == 0 and d % LANES == 0
    n_qkv = 3 * aw + 4 * mw
    d_ff = w_down.shape[1]
    assert d_ff % FFN_CHUNK == 0
    half = ATTN_HEAD_DIM // 2
    inv_freq = ROPE_THETA ** (-(jnp.arange(half, dtype=F32) / half))
    cos, sin = _rope_tables(positions, inv_freq)

    for layer in range(w_in.shape[0]):
        wl = w_in[layer]
        w1 = jnp.concatenate(
            [wl[:, :n_qkv].astype(BF16), wl[:, n_qkv:n_qkv + 2 * nh].astype(BF16),
             jnp.zeros((d, LANES - 2 * nh), BF16)], axis=1)
        wg = wl[:, n_qkv + 2 * nh:].astype(BF16)
        bias = jnp.concatenate(
            [i_bias[layer], f_bias[layer], jnp.zeros((LANES - 2 * nh,), F32)]).reshape(1, LANES)

        qT, k, kmean, vT, qm, kmT, vm, om, gif, gifT = _in_proj(
            x, cos, sin, norm_mix_g[layer].reshape(1, d), w1, conv_mlstm[layer], bias)
        attn = _moba(qT, k, vT, kmean.reshape(b, s // MOBA_BLOCK, aw))
        y_m = _mlstm(qm, kmT, vm, om, gif, gifT, mlstm_norm_g[layer].reshape(1, mw))
        x1 = _mix(x.reshape(b * s, d), attn.reshape(b * s, aw), y_m.reshape(b * s, mw),
                  norm_mix_g[layer].reshape(1, d), wg, w_branch_attn[layer].astype(BF16),
                  w_branch_mlstm[layer].astype(BF16), w_out[layer].astype(BF16)).reshape(b, s, d)

        wu = w_up[layer].astype(BF16)
        cf = conv_ffn[layer]
        last = layer == w_in.shape[0] - 1
        gfin = norm_final_g.reshape(1, d) if last else None
        assert last, "only the final layer fuses the output norm"
        x = _ffn(x1, norm_ffn_g[layer].reshape(1, d), wu,
                 _chunk_cols(cf[:, :d_ff], FFN_CHUNK), _chunk_cols(cf[:, d_ff:], FFN_CHUNK),
                 w_down[layer].astype(BF16), gfin)
    return x
```

```python
import functools
import math

import jax
import jax.numpy as jnp
from jax import lax
from jax.experimental import pallas as pl
from jax.experimental.pallas import tpu as pltpu

F32 = jnp.float32
BF16 = jnp.bfloat16

EPS = 1e-6
NEG = -1e30
ROPE_THETA = 10000.0

N_ATTN_HEADS = 8
ATTN_HEAD_DIM = 64
ATTN_WIDTH = N_ATTN_HEADS * ATTN_HEAD_DIM
MOBA_BLOCK = 256
MOBA_TOPK = 3
N_MLSTM_HEADS = 4
MLSTM_HEAD_DIM = 128
MLSTM_WIDTH = N_MLSTM_HEADS * MLSTM_HEAD_DIM

LANES = 128
HALO = 16
SEQ_TILE = MOBA_BLOCK
MIX_TILE = 512
FFN_TILE = 1024
FFN_CHUNK = 256
CONV_STRIDE = 4
SUM_ROWS = 16
MLSTM_ROWS = 1
MOBA_PAIRS_PER_STEP = 4
LOG2_E = math.log2(math.e)
VMEM_LIMIT = 56 * 1024 * 1024


def _dot(a, b):
    return jnp.dot(a, b, preferred_element_type=F32)


def _rms(x, g):
    ms = jnp.mean(x * x, axis=-1, keepdims=True)
    return x * lax.rsqrt(ms + EPS) * g


def _sigmoid(x):
    return 1.0 / (1.0 + jnp.exp(-x))


def _conv_block(u_ref, sl, r0, cw):
    ntap = cw.shape[0]
    rows = {k: u_ref[sl, pl.ds(HALO + r0 + k, 8, stride=CONV_STRIDE), :] for k in range(1 - ntap, CONV_STRIDE)}
    outs = []
    for k in range(CONV_STRIDE):
        out = cw[0:1] * rows[k - ntap + 1]
        for j in range(1, ntap):
            out = out + cw[j:j + 1] * rows[k - ntap + 1 + j]
        outs.append(out)
    return outs


def _const_spec(shape):
    nd = len(shape)
    return pl.BlockSpec(shape, lambda *_: (0,) * nd, pipeline_mode=pl.Buffered(1))


def _rope_table_kernel(pos_ref, invf_ref, cos_ref, sin_ref):
    rows, per_row = pos_ref.shape
    half = LANES // per_row
    lane = lax.broadcasted_iota(jnp.int32, (rows, LANES), 1)
    pos = pos_ref[...]
    pos_rep = pos[:, per_row - 1:per_row]
    for g in range(per_row - 2, -1, -1):
        pos_rep = jnp.where(lane < (g + 1) * half, pos[:, g:g + 1], pos_rep)
    ang = pos_rep.astype(F32) * invf_ref[...]
    for table, out_ref in ((jnp.cos(ang), cos_ref), (jnp.sin(ang), sin_ref)):
        for g in range(per_row):
            base = table if g == 0 else pltpu.roll(table, LANES - g * half, 1)
            width = half
            while width < LANES:
                base = jnp.where(lane < width, base, pltpu.roll(base, width, 1))
                width *= 2
            out_ref[pl.ds(g, rows, stride=per_row), :] = base


def _rope_tables(positions, inv_freq):
    b, s = positions.shape
    half = inv_freq.shape[0]
    per_row = LANES // half
    rows = b * s // per_row
    tile = min(rows, 512)
    invf = jnp.tile(inv_freq, per_row).reshape(1, LANES)
    out_spec = pl.BlockSpec((tile * per_row, LANES), lambda i: (i, 0))
    cos, sin = pl.pallas_call(
        _rope_table_kernel,
        grid=(rows // tile,),
        in_specs=[pl.BlockSpec((tile, per_row), lambda i: (i, 0)), _const_spec(invf.shape)],
        out_specs=[out_spec, out_spec],
        out_shape=[jax.ShapeDtypeStruct((b * s, LANES), F32)] * 2,
        compiler_params=pltpu.CompilerParams(dimension_semantics=("parallel",)),
        name="rope_tables",
    )(positions.reshape(rows, per_row), invf)
    return cos.reshape(b, s, LANES), sin.reshape(b, s, LANES)


def _in_proj_kernel(x_ref, xh_ref, cos_ref, sin_ref, g_ref, w_ref, cw_ref, bias_ref,
                    qT_ref, k_ref, kmean_ref, vT_ref, qm_ref, kmT_ref, vm_ref, om_ref, gif_ref, gifT_ref,
                    ext_ref, act_ref):
    tm = SEQ_TILE
    aw, mw = ATTN_WIDTH, MLSTM_WIDTH
    j = pl.program_id(1)
    g = g_ref[...]
    h = _rms(x_ref[0], g).astype(BF16)

    cos = cos_ref[0]
    sin = sin_ref[0]
    lane = lax.broadcasted_iota(jnp.int32, (tm, LANES), 1)
    first_half = (lane % ATTN_HEAD_DIM) < (ATTN_HEAD_DIM // 2)
    sin_signed = jnp.where(first_half, -sin, sin)

    def rope(t):
        outs = []
        for c in range(aw // LANES):
            tg = t[:, c * LANES:(c + 1) * LANES]
            swapped = jnp.where(first_half, pltpu.roll(tg, LANES - 32, 1), pltpu.roll(tg, 32, 1))
            outs.append(tg * cos + swapped * sin_signed)
        return jnp.concatenate(outs, axis=1)

    qk = _dot(h, w_ref[:, 0:2 * aw])
    q = rope(qk[:, :aw]) * (LOG2_E / math.sqrt(ATTN_HEAD_DIM))
    k = rope(qk[:, aw:])
    qT_ref[0, 0] = q.T.astype(BF16)
    k_ref[0, 0] = k.astype(BF16)
    kmean_ref[0] = jnp.mean(k, axis=0, keepdims=True)
    v = _dot(h, w_ref[:, 2 * aw:3 * aw])
    vT_ref[0, 0] = v.T.astype(BF16)

    c0 = 3 * aw
    pm = _dot(h, w_ref[:, c0:c0 + 2 * mw])
    hh = _rms(xh_ref[0], g).astype(BF16)
    ph = jnp.where(j == 0, 0.0, _dot(hh, w_ref[:, c0:c0 + 2 * mw]))
    cw = cw_ref[...]
    nslab = 2 * mw // LANES
    for sl in range(nslab):
        cols = slice(sl * LANES, (sl + 1) * LANES)
        ext_ref[sl, 0:HALO, :] = ph[:, cols]
        ext_ref[sl, HALO:HALO + tm, :] = pm[:, cols]
    for sl in range(nslab):
        cws = cw[:, sl * LANES:(sl + 1) * LANES]
        for r0 in range(0, tm, 8 * CONV_STRIDE):
            for kk, conv in enumerate(_conv_block(ext_ref, sl, r0, cws)):
                act_ref[sl, pl.ds(r0 + kk, 8, stride=CONV_STRIDE), :] = conv * _sigmoid(conv)
    q_act = jnp.concatenate([act_ref[sl] for sl in range(nslab // 2)], axis=1)
    k_act = jnp.concatenate([act_ref[sl] for sl in range(nslab // 2, nslab)], axis=1)
    qm_ref[0] = (q_act * (1.0 / math.sqrt(MLSTM_HEAD_DIM))).astype(BF16)
    kmT_ref[0, 0] = k_act.T.astype(BF16)

    c1 = c0 + 2 * mw
    vo = _dot(h, w_ref[:, c1:c1 + 2 * mw])
    vm_ref[0] = vo[:, :mw].astype(BF16)
    om_ref[0] = vo[:, mw:]

    c2 = c1 + 2 * mw
    gi = _dot(h, w_ref[:, c2:c2 + LANES]) + bias_ref[...]
    log_sig = jnp.minimum(gi, 0.0) - jnp.log1p(jnp.exp(-jnp.abs(gi)))
    gt = jnp.where(lane < N_MLSTM_HEADS, gi, log_sig)
    gif_ref[0] = gt[:, 0:2 * N_MLSTM_HEADS]
    gifT_ref[0] = gt.T[0:2 * N_MLSTM_HEADS, :]


def _in_proj(x, cos, sin, g, w, cw, bias):
    b, s, d = x.shape
    tm = SEQ_TILE
    nb = s // tm
    aw, mw = ATTN_WIDTH, MLSTM_WIDTH
    nh2 = 2 * N_MLSTM_HEADS
    out_shape = [
        jax.ShapeDtypeStruct((b, nb, aw, tm), BF16),
        jax.ShapeDtypeStruct((b, nb, tm, aw), BF16),
        jax.ShapeDtypeStruct((b * nb, 1, aw), F32),
        jax.ShapeDtypeStruct((b, nb, aw, tm), BF16),
        jax.ShapeDtypeStruct((b, s, mw), BF16),
        jax.ShapeDtypeStruct((b, nb, mw, tm), BF16),
        jax.ShapeDtypeStruct((b, s, mw), BF16),
        jax.ShapeDtypeStruct((b, s, mw), F32),
        jax.ShapeDtypeStruct((b, s, nh2), F32),
        jax.ShapeDtypeStruct((b, nh2, s), F32),
    ]
    blk4 = lambda r, c: pl.BlockSpec((1, 1, r, c), lambda bi, ji: (bi, ji, 0, 0))
    row3 = lambda c: pl.BlockSpec((1, tm, c), lambda bi, ji: (bi, ji, 0))
    out_specs = [
        blk4(aw, tm), blk4(tm, aw),
        pl.BlockSpec((1, 1, aw), lambda bi, ji: (bi * nb + ji, 0, 0)),
        blk4(aw, tm), row3(mw), blk4(mw, tm), row3(mw), row3(mw), row3(nh2),
        pl.BlockSpec((1, nh2, tm), lambda bi, ji: (bi, 0, ji)),
    ]
    in_specs = [
        pl.BlockSpec((1, tm, d), lambda bi, ji: (bi, ji, 0)),
        pl.BlockSpec((1, HALO, d), lambda bi, ji: (bi, jnp.maximum(ji * (tm // HALO) - 1, 0), 0)),
        row3(LANES), row3(LANES),
        _const_spec(g.shape), _const_spec(w.shape), _const_spec(cw.shape), _const_spec(bias.shape),
    ]
    return pl.pallas_call(
        _in_proj_kernel,
        grid=(b, nb),
        in_specs=in_specs,
        out_specs=out_specs,
        out_shape=out_shape,
        scratch_shapes=[pltpu.VMEM((2 * mw // LANES, HALO + tm, LANES), F32),
                        pltpu.VMEM((2 * mw // LANES, tm, LANES), F32)],
        compiler_params=pltpu.CompilerParams(
            dimension_semantics=("parallel", "parallel"), vmem_limit_bytes=VMEM_LIMIT),
        name="in_proj",
    )(x, x, cos, sin, g, w, cw, bias)


def _moba_kernel(jtab_ref, ntab_ref, qT_ref, k_ref, vT_ref, kmean_ref, out_ref,
                 bias_ref, m_ref, acc_ref, s_ref, cm_ref, p_ref, a_ref):
    blk = MOBA_BLOCK
    hd = ATTN_HEAD_DIM
    nb = k_ref.shape[1]
    nh = 2 * (qT_ref.shape[2] // LANES)
    n_units = nb * (nb - 1) // 2
    dim_row = lax.broadcasted_iota(jnp.int32, (LANES, blk), 0)
    dim_col = lax.broadcasted_iota(jnp.int32, (blk, LANES), 1)
    blk_id = lax.broadcasted_iota(jnp.int32, (nb, blk), 0)
    key_row = lax.broadcasted_iota(jnp.int32, (blk, blk), 0)
    qry_col = lax.broadcasted_iota(jnp.int32, (blk, blk), 1)

    def lanes(h):
        return slice((h // 2) * LANES, (h // 2 + 1) * LANES)

    ones_rows = jnp.ones((SUM_ROWS, blk), BF16)

    def v_ext(n, h):
        return jnp.concatenate([vT_ref[0, n, h * hd:(h + 1) * hd, :], ones_rows], axis=0)

    def phase_a(j, carry):
        for h in range(nh):
            qT = qT_ref[0, j, lanes(h), :]
            qh = jnp.where((dim_row >= hd) == (h % 2 == 1), qT, jnp.zeros_like(qT))
            km = kmean_ref[0, :, lanes(h)]
            km_hi = km.astype(BF16)
            km_lo = (km - km_hi.astype(F32)).astype(BF16)
            gate = _dot(km_hi, qh) + _dot(km_lo, qh)
            gate = jnp.where(blk_id < j, gate, NEG)
            rank = jnp.zeros((nb, blk), jnp.int32)
            for m in range(nb):
                gm = gate[m:m + 1, :]
                ahead = (gm > gate) | ((gm == gate) & (m < blk_id))
                rank = rank + ahead.astype(jnp.int32)
            selected = (rank < MOBA_TOPK) & (blk_id < j)
            bias_ref[h, j, 0:nb, :] = jnp.where(selected, 0.0, NEG)
            bias_ref[h, j, nb:nb + 1, :] = jnp.full((1, blk), NEG, F32)
            s_own = jnp.where(key_row <= qry_col, _dot(k_ref[0, j, :, lanes(h)], qh), NEG)
            m0 = jnp.max(s_own, axis=0, keepdims=True)
            p0 = jnp.exp2(s_own - m0)
            m_ref[h, j] = m0
            acc_ref[h, j, 0:hd, :] = _dot(vT_ref[0, j, h * hd:(h + 1) * hd, :], p0.astype(BF16))
            acc_ref[h, j, hd:hd + SUM_ROWS, :] = jnp.broadcast_to(jnp.sum(p0, axis=0, keepdims=True), (SUM_ROWS, blk))
        return carry

    lax.fori_loop(0, nb, phase_a, 0)

    for h in range(nh):
        s_ref[h] = jnp.zeros((blk, blk), F32)
        cm_ref[h] = jnp.zeros((1, blk), F32)
        p_ref[h] = jnp.zeros((blk, blk), BF16)
        a_ref[h] = jnp.ones((1, blk), F32)

    def phase_b(i, carry):
        j1, n1 = jtab_ref[i + 2], ntab_ref[i + 2]
        j2, n2 = jtab_ref[i + 1], ntab_ref[i + 1]
        j3, n3 = jtab_ref[i], ntab_ref[i]
        k_blk = jnp.minimum(n1, nb - 1)
        v_blk = jnp.minimum(n3, nb - 1)
        for h in range(nh):
            acc_ref[h, j3] = a_ref[h] * acc_ref[h, j3] + _dot(v_ext(v_blk, h), p_ref[h])
        for h in range(nh):
            bias = bias_ref[h, j2, pl.ds(n2, 1), :]
            m_old = m_ref[h, j2]
            m_new = jnp.maximum(m_old, cm_ref[h] + bias)
            alpha = jnp.exp2(m_old - m_new)
            p = jnp.exp2(s_ref[h] - (m_new - bias))
            m_ref[h, j2] = m_new
            a_ref[h] = alpha
            p_ref[h] = p.astype(BF16)
        for h in range(nh):
            kb = k_ref[0, k_blk, :, lanes(h)]
            kb = jnp.where((dim_col >= hd) == (h % 2 == 1), kb, jnp.zeros_like(kb))
            sT = _dot(kb, qT_ref[0, j1, lanes(h), :])
            s_ref[h] = sT
            cm_ref[h] = jnp.max(sT, axis=0, keepdims=True)
        return carry

    lax.fori_loop(0, n_units + 2, phase_b, 0)

    def phase_c(j, carry):
        row0 = pl.multiple_of(j * blk, blk)
        for pair in range(nh // 2):
            oT = jnp.concatenate([acc_ref[h, j, 0:hd, :] / acc_ref[h, j, hd:hd + 1, :]
                                  for h in (2 * pair, 2 * pair + 1)], axis=0)
            out_ref[0, pl.ds(row0, blk), pair * LANES:(pair + 1) * LANES] = oT.T.astype(BF16)
        return carry

    lax.fori_loop(0, nb, phase_c, 0)


def _moba_units(nb):
    pad = (0, nb)
    units = [pad, pad] + [(j, n) for j in range(nb) for n in range(j)] + [pad, pad]
    jtab = jnp.asarray([u[0] for u in units], jnp.int32)
    ntab = jnp.asarray([u[1] for u in units], jnp.int32)
    return jtab, ntab


def _moba(qT, k, vT, kmean):
    b, nb, aw, blk = qT.shape
    s = nb * blk
    w = MOBA_PAIRS_PER_STEP * LANES
    nh = 2 * MOBA_PAIRS_PER_STEP
    jtab, ntab = _moba_units(nb)
    grid_spec = pltpu.PrefetchScalarGridSpec(
        num_scalar_prefetch=2,
        grid=(b, aw // w),
        in_specs=[
            pl.BlockSpec((1, nb, w, blk), lambda bi, pi, *_: (bi, 0, pi, 0)),
            pl.BlockSpec((1, nb, blk, w), lambda bi, pi, *_: (bi, 0, 0, pi)),
            pl.BlockSpec((1, nb, w, blk), lambda bi, pi, *_: (bi, 0, pi, 0)),
            pl.BlockSpec((1, nb, w), lambda bi, pi, *_: (bi, 0, pi)),
        ],
        out_specs=pl.BlockSpec((1, s, w), lambda bi, pi, *_: (bi, 0, pi)),
        scratch_shapes=[pltpu.VMEM((nh, nb, nb + 8, blk), F32),
                        pltpu.VMEM((nh, nb, 1, blk), F32),
                        pltpu.VMEM((nh, nb, ATTN_HEAD_DIM + SUM_ROWS, blk), F32),
                        pltpu.VMEM((nh, blk, blk), F32),
                        pltpu.VMEM((nh, 1, blk), F32),
                        pltpu.VMEM((nh, blk, blk), BF16),
                        pltpu.VMEM((nh, 1, blk), F32)],
    )
    return pl.pallas_call(
        _moba_kernel,
        grid_spec=grid_spec,
        out_shape=jax.ShapeDtypeStruct((b, s, aw), BF16),
        compiler_params=pltpu.CompilerParams(
            dimension_semantics=("parallel", "parallel"), vmem_limit_bytes=VMEM_LIMIT),
        name="moba",
    )(jtab, ntab, qT, k, vT, kmean)


def _split3(x):
    hi = x.astype(BF16)
    r = x - hi.astype(F32)
    mid = r.astype(BF16)
    lo = (r - mid.astype(F32)).astype(BF16)
    return hi, mid, lo


def _mlstm_kernel(q_ref, kT_ref, v_ref, o_ref, gif_ref, gifT_ref, g_ref, out_ref, c_ref, m_ref):
    L = SEQ_TILE
    nh, hd = N_MLSTM_HEADS, MLSTM_HEAD_DIM
    c = pl.program_id(1)

    @pl.when(c == 0)
    def _():
        c_ref[...] = jnp.zeros_like(c_ref)
        m_ref[...] = jnp.zeros_like(m_ref)

    t_idx = lax.broadcasted_iota(jnp.int32, (L, L), 0)
    s_idx = lax.broadcasted_iota(jnp.int32, (L, L), 1)
    causal = s_idx <= t_idx
    tril = jnp.where(causal, 1.0, 0.0).astype(BF16)
    triu = jnp.where(t_idx <= s_idx, 1.0, 0.0).astype(BF16)

    ones_col = jnp.where(lax.broadcasted_iota(jnp.int32, (L, hd), 1) == 0, 1.0, 0.0).astype(BF16)
    gn = g_ref[...]

    gates = []
    for r in range(q_ref.shape[0]):
        gi_col = gif_ref[r]
        gi_row = gifT_ref[r]
        b_col_all = sum(_dot(tril, part) for part in _split3(gi_col))
        b_row_all = sum(_dot(part, triu) for part in _split3(gi_row))
        gates.append((gi_row, b_col_all, b_row_all))

    for r, h in [(r, h) for r in range(q_ref.shape[0]) for h in range(nh)]:
        gi_row, b_col_all, b_row_all = gates[r]
        a_row = gi_row[h:h + 1, :]
        b_row = b_row_all[nh + h:nh + h + 1, :]
        b_col = b_col_all[:, nh + h:nh + h + 1]
        q = q_ref[r, :, h * hd:(h + 1) * hd]
        kT = kT_ref[r, 0, h * hd:(h + 1) * hd, :]
        v_ext = jnp.concatenate([v_ref[r, :, h * hd:(h + 1) * hd], ones_col], axis=1)

        dmat = jnp.where(causal, b_col - b_row + a_row, NEG)
        mi = jnp.max(dmat, axis=1, keepdims=True)
        intra = _dot((jnp.exp(dmat - mi) * _dot(q, kT)).astype(BF16), v_ext)
        b_last = b_row[:, L - 1:L]
        dec = b_last - b_row + a_row
        md = jnp.max(dec, axis=1, keepdims=True)
        kv = _dot((kT.astype(F32) * jnp.exp(dec - md)).astype(BF16), v_ext)

        m_prev = m_ref[r * nh + h]
        c_old = c_ref[r * nh + h]
        inter = b_col + m_prev
        m_t = jnp.maximum(inter, mi)
        num_ext = jnp.exp(inter - m_t) * _dot(q, c_old.astype(BF16)) + jnp.exp(mi - m_t) * intra
        num = num_ext[:, :hd]
        den = num_ext[:, hd:hd + 1]
        h_out = num / jnp.maximum(jnp.abs(den), jnp.exp(-m_t))

        m_new = jnp.maximum(b_last + m_prev, md)
        c_ref[r * nh + h] = jnp.exp(b_last + m_prev - m_new) * c_old + jnp.exp(md - m_new) * kv
        m_ref[r * nh + h] = m_new

        hn = _rms(h_out, gn[:, h * hd:(h + 1) * hd])
        gate_o = _sigmoid(o_ref[r, :, h * hd:(h + 1) * hd])
        out_ref[r, :, h * hd:(h + 1) * hd] = (hn * gate_o).astype(BF16)


def _mlstm(qm, kmT, vm, om, gif, gifT, g):
    b, s, mw = qm.shape
    L = SEQ_TILE
    nc = s // L
    nh2 = 2 * N_MLSTM_HEADS
    rows = MLSTM_ROWS if b % MLSTM_ROWS == 0 else 1
    row = lambda cdim: pl.BlockSpec((rows, L, cdim), lambda bi, ci: (bi, ci, 0))
    return pl.pallas_call(
        _mlstm_kernel,
        grid=(b // rows, nc),
        in_specs=[
            row(mw),
            pl.BlockSpec((rows, 1, mw, L), lambda bi, ci: (bi, ci, 0, 0)),
            row(mw), row(mw), row(nh2),
            pl.BlockSpec((rows, nh2, L), lambda bi, ci: (bi, 0, ci)),
            _const_spec(g.shape),
        ],
        out_specs=row(mw),
        out_shape=jax.ShapeDtypeStruct((b, s, mw), BF16),
        scratch_shapes=[pltpu.VMEM((rows * N_MLSTM_HEADS, MLSTM_HEAD_DIM, 2 * MLSTM_HEAD_DIM), F32),
                        pltpu.VMEM((rows * N_MLSTM_HEADS, 1, 1), F32)],
        compiler_params=pltpu.CompilerParams(
            dimension_semantics=("parallel", "arbitrary"), vmem_limit_bytes=VMEM_LIMIT),
        name="mlstm",
    )(qm, kmT, vm, om, gif, gifT, g)


def _mix_kernel(x_ref, a_ref, y_ref, g_ref, wg_ref, wa_ref, wm_ref, wo_ref, out_ref):
    d = x_ref.shape[-1]
    x = x_ref[...]
    h = _rms(x, g_ref[...]).astype(BF16)
    gate_a = _sigmoid(_dot(h, wg_ref[:, :d]))
    gate_m = _sigmoid(_dot(h, wg_ref[:, d:]))
    merged = gate_a * _dot(a_ref[...], wa_ref[...]) + gate_m * _dot(y_ref[...], wm_ref[...])
    out_ref[...] = x + _dot(merged.astype(BF16), wo_ref[...])


def _mix(x2d, a2d, y2d, g, wg, wa, wm, wo):
    t, d = x2d.shape
    tm = MIX_TILE
    row = lambda cdim: pl.BlockSpec((tm, cdim), lambda i: (i, 0))
    return pl.pallas_call(
        _mix_kernel,
        grid=(t // tm,),
        in_specs=[row(d), row(a2d.shape[1]), row(y2d.shape[1]), _const_spec(g.shape),
                  _const_spec(wg.shape), _const_spec(wa.shape), _const_spec(wm.shape), _const_spec(wo.shape)],
        out_specs=row(d),
        out_shape=jax.ShapeDtypeStruct((t, d), F32),
        compiler_params=pltpu.CompilerParams(
            dimension_semantics=("parallel",), vmem_limit_bytes=VMEM_LIMIT),
        name="mix",
    )(x2d, a2d, y2d, g, wg, wa, wm, wo)


def _ffn_kernel(x_ref, xh_ref, g_ref, wup_ref, cg_ref, cu_ref, wd_ref, gf_ref, out_ref,
                hext_ref, ug_ref, uu_ref, act_ref, actall_ref):
    tm = FFN_TILE
    nslab = ug_ref.shape[0]
    nchunk = actall_ref.shape[1] // FFN_CHUNK
    j = pl.program_id(1)
    x = x_ref[0]
    g = g_ref[...]
    hext_ref[HALO:HALO + tm, :] = _rms(x, g).astype(BF16)
    hh = _rms(xh_ref[0], g)
    hext_ref[0:HALO, :] = jnp.where(j == 0, 0.0, hh).astype(BF16)

    def stage_a(c):
        gate_cols = pl.ds(pl.multiple_of(c * FFN_CHUNK, LANES), FFN_CHUNK)
        up_cols = pl.ds(pl.multiple_of(nchunk * FFN_CHUNK + c * FFN_CHUNK, LANES), FFN_CHUNK)
        ug = _dot(hext_ref[...], wup_ref[:, gate_cols])
        uu = _dot(hext_ref[...], wup_ref[:, up_cols])
        for sl in range(nslab):
            ug_ref[sl] = ug[:, sl * LANES:(sl + 1) * LANES]
            uu_ref[sl] = uu[:, sl * LANES:(sl + 1) * LANES]

    def stage_b(c):
        for sl in range(nslab):
            cg = cg_ref[c][:, sl * LANES:(sl + 1) * LANES]
            cu = cu_ref[c][:, sl * LANES:(sl + 1) * LANES]
            for r0 in range(0, tm, 8 * CONV_STRIDE):
                gates = _conv_block(ug_ref, sl, r0, cg)
                ups = _conv_block(uu_ref, sl, r0, cu)
                for k in range(CONV_STRIDE):
                    act_ref[sl, pl.ds(r0 + k, 8, stride=CONV_STRIDE), :] = gates[k] * _sigmoid(gates[k]) * ups[k]
        act = jnp.concatenate([act_ref[sl] for sl in range(nslab)], axis=1)
        actall_ref[:, pl.ds(pl.multiple_of(c * FFN_CHUNK, LANES), FFN_CHUNK)] = act.astype(BF16)

    stage_a(0)

    def body(c, carry):
        stage_b(c - 1)
        stage_a(c)
        return carry

    lax.fori_loop(1, nchunk, body, 0)
    stage_b(nchunk - 1)
    out_ref[0] = _rms(x + _dot(actall_ref[...], wd_ref[...]), gf_ref[...])


def _ffn(x1, g2, wup, cg, cu, wd, gf):
    b, s, d = x1.shape
    tm = FFN_TILE
    return pl.pallas_call(
        _ffn_kernel,
        grid=(b, s // tm),
        in_specs=[
            pl.BlockSpec((1, tm, d), lambda bi, ji: (bi, ji, 0)),
            pl.BlockSpec((1, HALO, d), lambda bi, ji: (bi, jnp.maximum(ji * (tm // HALO) - 1, 0), 0)),
            _const_spec(g2.shape), _const_spec(wup.shape),
            _const_spec(cg.shape), _const_spec(cu.shape), _const_spec(wd.shape), _const_spec(gf.shape),
        ],
        out_specs=pl.BlockSpec((1, tm, d), lambda bi, ji: (bi, ji, 0)),
        out_shape=jax.ShapeDtypeStruct((b, s, d), F32),
        scratch_shapes=[pltpu.VMEM((HALO + tm, d), BF16),
                        pltpu.VMEM((FFN_CHUNK // LANES, HALO + tm, LANES), F32),
                        pltpu.VMEM((FFN_CHUNK // LANES, HALO + tm, LANES), F32),
                        pltpu.VMEM((FFN_CHUNK // LANES, tm, LANES), F32),
                        pltpu.VMEM((tm, wd.shape[0]), BF16)],
        compiler_params=pltpu.CompilerParams(
            dimension_semantics=("parallel", "parallel"), vmem_limit_bytes=VMEM_LIMIT),
        name="ffn",
    )(x1, x1, g2, wup, cg, cu, wd, gf)


def _chunk_cols(w, chunk):
    kdim, n = w.shape
    return w.reshape(kdim, n // chunk, chunk).transpose(1, 0, 2)


def kernel(x, positions, norm_mix_g, w_in, conv_mlstm, i_bias, f_bias, mlstm_norm_g, w_branch_attn,
           w_branch_mlstm, w_out, norm_ffn_g, w_up, conv_ffn, w_down, norm_final_g):
    b, s, d = x.shape
    aw, mw, nh = ATTN_WIDTH, MLSTM_WIDTH, N_MLSTM_HEADS
    assert s % FFN_TILE == 0 and s % SEQ_TILE == 0 and d % LANES == 0
    n_qkv = 3 * aw + 4 * mw
    d_ff = w_down.shape[1]
    assert d_ff % FFN_CHUNK == 0
    half = ATTN_HEAD_DIM // 2
    inv_freq = ROPE_THETA ** (-(jnp.arange(half, dtype=F32) / half))
    cos, sin = _rope_tables(positions, inv_freq)

    for layer in range(w_in.shape[0]):
        wl = w_in[layer]
        w1 = jnp.concatenate(
            [wl[:, :n_qkv].astype(BF16), wl[:, n_qkv:n_qkv + 2 * nh].astype(BF16),
             jnp.zeros((d, LANES - 2 * nh), BF16)], axis=1)
        wg = wl[:, n_qkv + 2 * nh:].astype(BF16)
        bias = jnp.concatenate(
            [i_bias[layer], f_bias[layer], jnp.zeros((LANES - 2 * nh,), F32)]).reshape(1, LANES)

        qT, k, kmean, vT, qm, kmT, vm, om, gif, gifT = _in_proj(
            x, cos, sin, norm_mix_g[layer].reshape(1, d), w1, conv_mlstm[layer], bias)
        attn = _moba(qT, k, vT, kmean.reshape(b, s // MOBA_BLOCK, aw))
        y_m = _mlstm(qm, kmT, vm, om, gif, gifT, mlstm_norm_g[layer].reshape(1, mw))
        x1 = _mix(x.reshape(b * s, d), attn.reshape(b * s, aw), y_m.reshape(b * s, mw),
                  norm_mix_g[layer].reshape(1, d), wg, w_branch_attn[layer].astype(BF16),
                  w_branch_mlstm[layer].astype(BF16), w_out[layer].astype(BF16)).reshape(b, s, d)

        wu = w_up[layer].astype(BF16)
        cf = conv_ffn[layer]
        last = layer == w_in.shape[0] - 1
        gfin = norm_final_g.reshape(1, d) if last else None
        assert last, "only the final layer fuses the output norm"
        x = _ffn(x1, norm_ffn_g[layer].reshape(1, d), wu,
                 _chunk_cols(cf[:, :d_ff], FFN_CHUNK), _chunk_cols(cf[:, d_ff:], FFN_CHUNK),
                 w_down[layer].astype(BF16), gfin)
    return x
```

```python
import functools
import math

import jax
import jax.numpy as jnp
from jax import lax
from jax.experimental import pallas as pl
from jax.experimental.pallas import tpu as pltpu

F32 = jnp.float32
BF16 = jnp.bfloat16

EPS = 1e-6
NEG = -1e30
ROPE_THETA = 10000.0

N_ATTN_HEADS = 8
ATTN_HEAD_DIM = 64
ATTN_WIDTH = N_ATTN_HEADS * ATTN_HEAD_DIM
MOBA_BLOCK = 256
MOBA_TOPK = 3
N_MLSTM_HEADS = 4
MLSTM_HEAD_DIM = 128
MLSTM_WIDTH = N_MLSTM_HEADS * MLSTM_HEAD_DIM

LANES = 128
HALO = 16
SEQ_TILE = MOBA_BLOCK
MIX_TILE = 512
FFN_TILE = 1024
FFN_CHUNK = 256
CONV_STRIDE = 4
SUM_ROWS = 16
MLSTM_CHUNK = 128
MLSTM_ROWS = 1
MOBA_PAIRS_PER_STEP = 4
LOG2_E = math.log2(math.e)
VMEM_LIMIT = 56 * 1024 * 1024


def _dot(a, b):
    return jnp.dot(a, b, preferred_element_type=F32)


def _rms(x, g):
    ms = jnp.mean(x * x, axis=-1, keepdims=True)
    return x * lax.rsqrt(ms + EPS) * g


def _sigmoid(x):
    return 1.0 / (1.0 + jnp.exp(-x))


def _conv_block(u_ref, sl, r0, cw):
    ntap = cw.shape[0]
    rows = {k: u_ref[sl, pl.ds(HALO + r0 + k, 8, stride=CONV_STRIDE), :] for k in range(1 - ntap, CONV_STRIDE)}
    outs = []
    for k in range(CONV_STRIDE):
        out = cw[0:1] * rows[k - ntap + 1]
        for j in range(1, ntap):
            out = out + cw[j:j + 1] * rows[k - ntap + 1 + j]
        outs.append(out)
    return outs


def _const_spec(shape):
    nd = len(shape)
    return pl.BlockSpec(shape, lambda *_: (0,) * nd, pipeline_mode=pl.Buffered(1))


def _rope_table_kernel(pos_ref, invf_ref, cos_ref, sin_ref):
    rows, per_row = pos_ref.shape
    half = LANES // per_row
    lane = lax.broadcasted_iota(jnp.int32, (rows, LANES), 1)
    pos = pos_ref[...]
    pos_rep = pos[:, per_row - 1:per_row]
    for g in range(per_row - 2, -1, -1):
        pos_rep = jnp.where(lane < (g + 1) * half, pos[:, g:g + 1], pos_rep)
    ang = pos_rep.astype(F32) * invf_ref[...]
    for table, out_ref in ((jnp.cos(ang), cos_ref), (jnp.sin(ang), sin_ref)):
        for g in range(per_row):
            base = table if g == 0 else pltpu.roll(table, LANES - g * half, 1)
            width = half
            while width < LANES:
                base = jnp.where(lane < width, base, pltpu.roll(base, width, 1))
                width *= 2
            out_ref[pl.ds(g, rows, stride=per_row), :] = base


def _rope_tables(positions, inv_freq):
    b, s = positions.shape
    half = inv_freq.shape[0]
    per_row = LANES // half
    rows = b * s // per_row
    tile = min(rows, 512)
    invf = jnp.tile(inv_freq, per_row).reshape(1, LANES)
    out_spec = pl.BlockSpec((tile * per_row, LANES), lambda i: (i, 0))
    cos, sin = pl.pallas_call(
        _rope_table_kernel,
        grid=(rows // tile,),
        in_specs=[pl.BlockSpec((tile, per_row), lambda i: (i, 0)), _const_spec(invf.shape)],
        out_specs=[out_spec, out_spec],
        out_shape=[jax.ShapeDtypeStruct((b * s, LANES), F32)] * 2,
        compiler_params=pltpu.CompilerParams(dimension_semantics=("parallel",)),
        name="rope_tables",
    )(positions.reshape(rows, per_row), invf)
    return cos.reshape(b, s, LANES), sin.reshape(b, s, LANES)


def _in_proj_kernel(x_ref, xh_ref, cos_ref, sin_ref, g_ref, w_ref, cw_ref, bias_ref,
                    qT_ref, k_ref, kmean_ref, vT_ref, qm_ref, kmT_ref, vm_ref, om_ref, gif_ref, gifT_ref,
                    ext_ref, act_ref):
    tm = SEQ_TILE
    aw, mw = ATTN_WIDTH, MLSTM_WIDTH
    j = pl.program_id(1)
    g = g_ref[...]
    h = _rms(x_ref[0], g).astype(BF16)

    cos = cos_ref[0]
    sin = sin_ref[0]
    lane = lax.broadcasted_iota(jnp.int32, (tm, LANES), 1)
    first_half = (lane % ATTN_HEAD_DIM) < (ATTN_HEAD_DIM // 2)
    sin_signed = jnp.where(first_half, -sin, sin)

    def rope(t):
        outs = []
        for c in range(aw // LANES):
            tg = t[:, c * LANES:(c + 1) * LANES]
            swapped = jnp.where(first_half, pltpu.roll(tg, LANES - 32, 1), pltpu.roll(tg, 32, 1))
            outs.append(tg * cos + swapped * sin_signed)
        return jnp.concatenate(outs, axis=1)

    qk = _dot(h, w_ref[:, 0:2 * aw])
    q = rope(qk[:, :aw]) * (LOG2_E / math.sqrt(ATTN_HEAD_DIM))
    k = rope(qk[:, aw:])
    qT_ref[0, 0] = q.T.astype(BF16)
    k_ref[0, 0] = k.astype(BF16)
    kmean_ref[0] = jnp.mean(k, axis=0, keepdims=True)
    v = _dot(h, w_ref[:, 2 * aw:3 * aw])
    vT_ref[0, 0] = v.T.astype(BF16)

    c0 = 3 * aw
    pm = _dot(h, w_ref[:, c0:c0 + 2 * mw])
    hh = _rms(xh_ref[0], g).astype(BF16)
    ph = jnp.where(j == 0, 0.0, _dot(hh, w_ref[:, c0:c0 + 2 * mw]))
    cw = cw_ref[...]
    nslab = 2 * mw // LANES
    for sl in range(nslab):
        cols = slice(sl * LANES, (sl + 1) * LANES)
        ext_ref[sl, 0:HALO, :] = ph[:, cols]
        ext_ref[sl, HALO:HALO + tm, :] = pm[:, cols]
    for sl in range(nslab):
        cws = cw[:, sl * LANES:(sl + 1) * LANES]
        for r0 in range(0, tm, 8 * CONV_STRIDE):
            for kk, conv in enumerate(_conv_block(ext_ref, sl, r0, cws)):
                act_ref[sl, pl.ds(r0 + kk, 8, stride=CONV_STRIDE), :] = conv * _sigmoid(conv)
    q_act = jnp.concatenate([act_ref[sl] for sl in range(nslab // 2)], axis=1)
    k_act = jnp.concatenate([act_ref[sl] for sl in range(nslab // 2, nslab)], axis=1)
    qm_ref[0] = (q_act * (1.0 / math.sqrt(MLSTM_HEAD_DIM))).astype(BF16)
    kmT_ref[0, 0] = k_act.T.astype(BF16)

    c1 = c0 + 2 * mw
    vo = _dot(h, w_ref[:, c1:c1 + 2 * mw])
    vm_ref[0] = vo[:, :mw].astype(BF16)
    om_ref[0] = vo[:, mw:]

    c2 = c1 + 2 * mw
    gi = _dot(h, w_ref[:, c2:c2 + LANES]) + bias_ref[...]
    log_sig = jnp.minimum(gi, 0.0) - jnp.log1p(jnp.exp(-jnp.abs(gi)))
    gt = jnp.where(lane < N_MLSTM_HEADS, gi, log_sig)
    gif_ref[0] = gt[:, 0:2 * N_MLSTM_HEADS]
    gifT_ref[0] = gt.T[0:2 * N_MLSTM_HEADS, :]


def _in_proj(x, cos, sin, g, w, cw, bias):
    b, s, d = x.shape
    tm = SEQ_TILE
    nb = s // tm
    aw, mw = ATTN_WIDTH, MLSTM_WIDTH
    nh2 = 2 * N_MLSTM_HEADS
    out_shape = [
        jax.ShapeDtypeStruct((b, nb, aw, tm), BF16),
        jax.ShapeDtypeStruct((b, nb, tm, aw), BF16),
        jax.ShapeDtypeStruct((b * nb, 1, aw), F32),
        jax.ShapeDtypeStruct((b, nb, aw, tm), BF16),
        jax.ShapeDtypeStruct((b, s, mw), BF16),
        jax.ShapeDtypeStruct((b, nb, mw, tm), BF16),
        jax.ShapeDtypeStruct((b, s, mw), BF16),
        jax.ShapeDtypeStruct((b, s, mw), F32),
        jax.ShapeDtypeStruct((b, s, nh2), F32),
        jax.ShapeDtypeStruct((b, nh2, s), F32),
    ]
    blk4 = lambda r, c: pl.BlockSpec((1, 1, r, c), lambda bi, ji: (bi, ji, 0, 0))
    row3 = lambda c: pl.BlockSpec((1, tm, c), lambda bi, ji: (bi, ji, 0))
    out_specs = [
        blk4(aw, tm), blk4(tm, aw),
        pl.BlockSpec((1, 1, aw), lambda bi, ji: (bi * nb + ji, 0, 0)),
        blk4(aw, tm), row3(mw), blk4(mw, tm), row3(mw), row3(mw), row3(nh2),
        pl.BlockSpec((1, nh2, tm), lambda bi, ji: (bi, 0, ji)),
    ]
    in_specs = [
        pl.BlockSpec((1, tm, d), lambda bi, ji: (bi, ji, 0)),
        pl.BlockSpec((1, HALO, d), lambda bi, ji: (bi, jnp.maximum(ji * (tm // HALO) - 1, 0), 0)),
        row3(LANES), row3(LANES),
        _const_spec(g.shape), _const_spec(w.shape), _const_spec(cw.shape), _const_spec(bias.shape),
    ]
    return pl.pallas_call(
        _in_proj_kernel,
        grid=(b, nb),
        in_specs=in_specs,
        out_specs=out_specs,
        out_shape=out_shape,
        scratch_shapes=[pltpu.VMEM((2 * mw // LANES, HALO + tm, LANES), F32),
                        pltpu.VMEM((2 * mw // LANES, tm, LANES), F32)],
        compiler_params=pltpu.CompilerParams(
            dimension_semantics=("parallel", "parallel"), vmem_limit_bytes=VMEM_LIMIT),
        name="in_proj",
    )(x, x, cos, sin, g, w, cw, bias)


def _moba_kernel(jtab_ref, ntab_ref, qT_ref, k_ref, vT_ref, kmean_ref, out_ref,
                 bias_ref, m_ref, acc_ref, s_ref, cm_ref, p_ref, a_ref):
    blk = MOBA_BLOCK
    hd = ATTN_HEAD_DIM
    nb = k_ref.shape[1]
    nh = 2 * (qT_ref.shape[2] // LANES)
    n_units = nb * (nb - 1) // 2
    dim_row = lax.broadcasted_iota(jnp.int32, (LANES, blk), 0)
    dim_col = lax.broadcasted_iota(jnp.int32, (blk, LANES), 1)
    blk_id = lax.broadcasted_iota(jnp.int32, (nb, blk), 0)
    blk_idf = blk_id.astype(F32)
    key_row = lax.broadcasted_iota(jnp.int32, (blk, blk), 0)
    qry_col = lax.broadcasted_iota(jnp.int32, (blk, blk), 1)

    def lanes(h):
        return slice((h // 2) * LANES, (h // 2 + 1) * LANES)

    ones_rows = jnp.ones((SUM_ROWS, blk), BF16)

    def v_ext(n, h):
        return jnp.concatenate([vT_ref[0, n, h * hd:(h + 1) * hd, :], ones_rows], axis=0)

    def phase_a(j, carry):
        for h in range(nh):
            qT = qT_ref[0, j, lanes(h), :]
            qh = jnp.where((dim_row >= hd) == (h % 2 == 1), qT, jnp.zeros_like(qT))
            km = kmean_ref[0, :, lanes(h)]
            km_hi = km.astype(BF16)
            km_lo = (km - km_hi.astype(F32)).astype(BF16)
            gate = _dot(km_hi, qh) + _dot(km_lo, qh)
            gate = jnp.where(blk_id < j, gate, NEG)
            picked = blk_id < 0
            for _ in range(min(MOBA_TOPK, nb)):
                top = jnp.max(gate, axis=0, keepdims=True)
                first = jnp.min(jnp.where(gate == top, blk_idf, float(nb)), axis=0, keepdims=True)
                hit = blk_idf == first
                picked = picked | hit
                gate = jnp.where(hit, -jnp.inf, gate)
            selected = picked & (blk_id < j)
            bias_ref[h, j, 0:nb, :] = jnp.where(selected, 0.0, NEG)
            bias_ref[h, j, nb:nb + 1, :] = jnp.full((1, blk), NEG, F32)
            s_own = jnp.where(key_row <= qry_col, _dot(k_ref[0, j, :, lanes(h)], qh), NEG)
            m0 = jnp.max(s_own, axis=0, keepdims=True)
            p0 = jnp.exp2(s_own - m0)
            m_ref[h, j] = m0
            acc_ref[h, j, 0:hd, :] = _dot(vT_ref[0, j, h * hd:(h + 1) * hd, :], p0.astype(BF16))
            acc_ref[h, j, hd:hd + SUM_ROWS, :] = jnp.broadcast_to(jnp.sum(p0, axis=0, keepdims=True), (SUM_ROWS, blk))
        return carry

    lax.fori_loop(0, nb, phase_a, 0)

    for h in range(nh):
        s_ref[h] = jnp.zeros((blk, blk), F32)
        cm_ref[h] = jnp.zeros((1, blk), F32)
        p_ref[h] = jnp.zeros((blk, blk), BF16)
        a_ref[h] = jnp.ones((1, blk), F32)

    def phase_b(i, carry):
        j1, n1 = jtab_ref[i + 2], ntab_ref[i + 2]
        j2, n2 = jtab_ref[i + 1], ntab_ref[i + 1]
        j3, n3 = jtab_ref[i], ntab_ref[i]
        k_blk = jnp.minimum(n1, nb - 1)
        v_blk = jnp.minimum(n3, nb - 1)
        for h in range(nh):
            acc_ref[h, j3] = a_ref[h] * acc_ref[h, j3] + _dot(v_ext(v_blk, h), p_ref[h])
        for h in range(nh):
            bias = bias_ref[h, j2, pl.ds(n2, 1), :]
            m_old = m_ref[h, j2]
            m_new = jnp.maximum(m_old, cm_ref[h] + bias)
            alpha = jnp.exp2(m_old - m_new)
            p = jnp.exp2(s_ref[h] - (m_new - bias))
            m_ref[h, j2] = m_new
            a_ref[h] = alpha
            p_ref[h] = p.astype(BF16)
        for h in range(nh):
            kb = k_ref[0, k_blk, :, lanes(h)]
            kb = jnp.where((dim_col >= hd) == (h % 2 == 1), kb, jnp.zeros_like(kb))
            sT = _dot(kb, qT_ref[0, j1, lanes(h), :])
            s_ref[h] = sT
            cm_ref[h] = jnp.max(sT, axis=0, keepdims=True)
        return carry

    lax.fori_loop(0, n_units + 2, phase_b, 0)

    def phase_c(j, carry):
        row0 = pl.multiple_of(j * blk, blk)
        for pair in range(nh // 2):
            oT = jnp.concatenate([acc_ref[h, j, 0:hd, :] / acc_ref[h, j, hd:hd + 1, :]
                                  for h in (2 * pair, 2 * pair + 1)], axis=0)
            out_ref[0, pl.ds(row0, blk), pair * LANES:(pair + 1) * LANES] = oT.T.astype(BF16)
        return carry

    lax.fori_loop(0, nb, phase_c, 0)


def _moba_units(nb):
    pad = (0, nb)
    units = [pad, pad] + [(j, n) for j in range(nb) for n in range(j)] + [pad, pad]
    jtab = jnp.asarray([u[0] for u in units], jnp.int32)
    ntab = jnp.asarray([u[1] for u in units], jnp.int32)
    return jtab, ntab


def _moba(qT, k, vT, kmean):
    b, nb, aw, blk = qT.shape
    s = nb * blk
    w = MOBA_PAIRS_PER_STEP * LANES
    nh = 2 * MOBA_PAIRS_PER_STEP
    jtab, ntab = _moba_units(nb)
    grid_spec = pltpu.PrefetchScalarGridSpec(
        num_scalar_prefetch=2,
        grid=(b, aw // w),
        in_specs=[
            pl.BlockSpec((1, nb, w, blk), lambda bi, pi, *_: (bi, 0, pi, 0)),
            pl.BlockSpec((1, nb, blk, w), lambda bi, pi, *_: (bi, 0, 0, pi)),
            pl.BlockSpec((1, nb, w, blk), lambda bi, pi, *_: (bi, 0, pi, 0)),
            pl.BlockSpec((1, nb, w), lambda bi, pi, *_: (bi, 0, pi)),
        ],
        out_specs=pl.BlockSpec((1, s, w), lambda bi, pi, *_: (bi, 0, pi)),
        scratch_shapes=[pltpu.VMEM((nh, nb, nb + 8, blk), F32),
                        pltpu.VMEM((nh, nb, 1, blk), F32),
                        pltpu.VMEM((nh, nb, ATTN_HEAD_DIM + SUM_ROWS, blk), F32),
                        pltpu.VMEM((nh, blk, blk), F32),
                        pltpu.VMEM((nh, 1, blk), F32),
                        pltpu.VMEM((nh, blk, blk), BF16),
                        pltpu.VMEM((nh, 1, blk), F32)],
    )
    return pl.pallas_call(
        _moba_kernel,
        grid_spec=grid_spec,
        out_shape=jax.ShapeDtypeStruct((b, s, aw), BF16),
        compiler_params=pltpu.CompilerParams(
            dimension_semantics=("parallel", "parallel"), vmem_limit_bytes=VMEM_LIMIT),
        name="moba",
    )(jtab, ntab, qT, k, vT, kmean)


def _split3(x):
    hi = x.astype(BF16)
    r = x - hi.astype(F32)
    mid = r.astype(BF16)
    lo = (r - mid.astype(F32)).astype(BF16)
    return hi, mid, lo


def _mlstm_kernel(q_ref, kT_ref, v_ref, o_ref, gif_ref, gifT_ref, g_ref, out_ref, c_ref, m_ref):
    L = MLSTM_CHUNK
    nh, hd = N_MLSTM_HEADS, MLSTM_HEAD_DIM
    c = pl.program_id(1)

    @pl.when(c == 0)
    def _():
        c_ref[...] = jnp.zeros_like(c_ref)
        m_ref[...] = jnp.zeros_like(m_ref)

    t_idx = lax.broadcasted_iota(jnp.int32, (L, L), 0)
    s_idx = lax.broadcasted_iota(jnp.int32, (L, L), 1)
    causal = s_idx <= t_idx
    tril = jnp.where(causal, 1.0, 0.0).astype(BF16)
    triu = jnp.where(t_idx <= s_idx, 1.0, 0.0).astype(BF16)

    ones_col = jnp.where(lax.broadcasted_iota(jnp.int32, (L, hd), 1) == 0, 1.0, 0.0).astype(BF16)
    gn = g_ref[...]

    for r, r0 in [(r, r0) for r in range(q_ref.shape[0]) for r0 in range(0, SEQ_TILE, L)]:
        rows = slice(r0, r0 + L)
        gi_col = gif_ref[r, rows, :]
        gi_row = gifT_ref[r, :, rows]
        b_col_all = sum(_dot(tril, part) for part in _split3(gi_col))
        b_row_all = sum(_dot(part, triu) for part in _split3(gi_row))

        for h in range(nh):
            cols = slice(h * hd, (h + 1) * hd)
            a_row = gi_row[h:h + 1, :]
            b_row = b_row_all[nh + h:nh + h + 1, :]
            b_col = b_col_all[:, nh + h:nh + h + 1]
            q = q_ref[r, rows, cols]
            kT = kT_ref[r, 0, cols, rows]
            v_ext = jnp.concatenate([v_ref[r, rows, cols], ones_col], axis=1)

            dmat = jnp.where(causal, b_col - b_row + a_row, NEG)
            mi = jnp.max(dmat, axis=1, keepdims=True)
            intra = _dot((jnp.exp(dmat - mi) * _dot(q, kT)).astype(BF16), v_ext)
            b_last = b_row[:, L - 1:L]
            dec = b_last - b_row + a_row
            md = jnp.max(dec, axis=1, keepdims=True)
            kv = _dot((kT.astype(F32) * jnp.exp(dec - md)).astype(BF16), v_ext)

            m_prev = m_ref[r * nh + h]
            c_old = c_ref[r * nh + h]
            inter = b_col + m_prev
            m_t = jnp.maximum(inter, mi)
            num_ext = jnp.exp(inter - m_t) * _dot(q, c_old.astype(BF16)) + jnp.exp(mi - m_t) * intra
            num = num_ext[:, :hd]
            den = num_ext[:, hd:hd + 1]
            h_out = num / jnp.maximum(jnp.abs(den), jnp.exp(-m_t))

            m_new = jnp.maximum(b_last + m_prev, md)
            c_ref[r * nh + h] = jnp.exp(b_last + m_prev - m_new) * c_old + jnp.exp(md - m_new) * kv
            m_ref[r * nh + h] = m_new

            hn = _rms(h_out, gn[:, cols])
            gate_o = _sigmoid(o_ref[r, rows, cols])
            out_ref[r, rows, cols] = (hn * gate_o).astype(BF16)


def _mlstm(qm, kmT, vm, om, gif, gifT, g):
    b, s, mw = qm.shape
    L = SEQ_TILE
    nc = s // L
    nh2 = 2 * N_MLSTM_HEADS
    rows = MLSTM_ROWS if b % MLSTM_ROWS == 0 else 1
    row = lambda cdim: pl.BlockSpec((rows, L, cdim), lambda bi, ci: (bi, ci, 0))
    return pl.pallas_call(
        _mlstm_kernel,
        grid=(b // rows, nc),
        in_specs=[
            row(mw),
            pl.BlockSpec((rows, 1, mw, L), lambda bi, ci: (bi, ci, 0, 0)),
            row(mw), row(mw), row(nh2),
            pl.BlockSpec((rows, nh2, L), lambda bi, ci: (bi, 0, ci)),
            _const_spec(g.shape),
        ],
        out_specs=row(mw),
        out_shape=jax.ShapeDtypeStruct((b, s, mw), BF16),
        scratch_shapes=[pltpu.VMEM((rows * N_MLSTM_HEADS, MLSTM_HEAD_DIM, 2 * MLSTM_HEAD_DIM), F32),
                        pltpu.VMEM((rows * N_MLSTM_HEADS, 1, 1), F32)],
        compiler_params=pltpu.CompilerParams(
            dimension_semantics=("parallel", "arbitrary"), vmem_limit_bytes=VMEM_LIMIT),
        name="mlstm",
    )(qm, kmT, vm, om, gif, gifT, g)


def _mix_kernel(x_ref, a_ref, y_ref, g_ref, wg_ref, wa_ref, wm_ref, wo_ref, out_ref):
    d = x_ref.shape[-1]
    x = x_ref[...]
    h = _rms(x, g_ref[...]).astype(BF16)
    gate_a = _sigmoid(_dot(h, wg_ref[:, :d]))
    gate_m = _sigmoid(_dot(h, wg_ref[:, d:]))
    merged = gate_a * _dot(a_ref[...], wa_ref[...]) + gate_m * _dot(y_ref[...], wm_ref[...])
    out_ref[...] = x + _dot(merged.astype(BF16), wo_ref[...])


def _mix(x2d, a2d, y2d, g, wg, wa, wm, wo):
    t, d = x2d.shape
    tm = MIX_TILE
    row = lambda cdim: pl.BlockSpec((tm, cdim), lambda i: (i, 0))
    return pl.pallas_call(
        _mix_kernel,
        grid=(t // tm,),
        in_specs=[row(d), row(a2d.shape[1]), row(y2d.shape[1]), _const_spec(g.shape),
                  _const_spec(wg.shape), _const_spec(wa.shape), _const_spec(wm.shape), _const_spec(wo.shape)],
        out_specs=row(d),
        out_shape=jax.ShapeDtypeStruct((t, d), F32),
        compiler_params=pltpu.CompilerParams(
            dimension_semantics=("parallel",), vmem_limit_bytes=VMEM_LIMIT),
        name="mix",
    )(x2d, a2d, y2d, g, wg, wa, wm, wo)


def _ffn_kernel(x_ref, xh_ref, g_ref, wup_ref, cg_ref, cu_ref, wd_ref, gf_ref, out_ref,
                hext_ref, ug_ref, uu_ref, act_ref, actall_ref):
    tm = FFN_TILE
    nslab = ug_ref.shape[0]
    nchunk = actall_ref.shape[1] // FFN_CHUNK
    j = pl.program_id(1)
    x = x_ref[0]
    g = g_ref[...]
    hext_ref[HALO:HALO + tm, :] = _rms(x, g).astype(BF16)
    hh = _rms(xh_ref[0], g)
    hext_ref[0:HALO, :] = jnp.where(j == 0, 0.0, hh).astype(BF16)

    def stage_a(c):
        gate_cols = pl.ds(pl.multiple_of(c * FFN_CHUNK, LANES), FFN_CHUNK)
        up_cols = pl.ds(pl.multiple_of(nchunk * FFN_CHUNK + c * FFN_CHUNK, LANES), FFN_CHUNK)
        ug = _dot(hext_ref[...], wup_ref[:, gate_cols])
        uu = _dot(hext_ref[...], wup_ref[:, up_cols])
        for sl in range(nslab):
            ug_ref[sl] = ug[:, sl * LANES:(sl + 1) * LANES]
            uu_ref[sl] = uu[:, sl * LANES:(sl + 1) * LANES]

    def stage_b(c):
        for sl in range(nslab):
            cg = cg_ref[c][:, sl * LANES:(sl + 1) * LANES]
            cu = cu_ref[c][:, sl * LANES:(sl + 1) * LANES]
            for r0 in range(0, tm, 8 * CONV_STRIDE):
                gates = _conv_block(ug_ref, sl, r0, cg)
                ups = _conv_block(uu_ref, sl, r0, cu)
                for k in range(CONV_STRIDE):
                    act_ref[sl, pl.ds(r0 + k, 8, stride=CONV_STRIDE), :] = gates[k] * _sigmoid(gates[k]) * ups[k]
        act = jnp.concatenate([act_ref[sl] for sl in range(nslab)], axis=1)
        actall_ref[:, pl.ds(pl.multiple_of(c * FFN_CHUNK, LANES), FFN_CHUNK)] = act.astype(BF16)

    stage_a(0)

    def body(c, carry):
        stage_b(c - 1)
        stage_a(c)
        return carry

    lax.fori_loop(1, nchunk, body, 0)
    stage_b(nchunk - 1)
    out_ref[0] = _rms(x + _dot(actall_ref[...], wd_ref[...]), gf_ref[...])


def _ffn(x1, g2, wup, cg, cu, wd, gf):
    b, s, d = x1.shape
    tm = FFN_TILE
    return pl.pallas_call(
        _ffn_kernel,
        grid=(b, s // tm),
        in_specs=[
            pl.BlockSpec((1, tm, d), lambda bi, ji: (bi, ji, 0)),
            pl.BlockSpec((1, HALO, d), lambda bi, ji: (bi, jnp.maximum(ji * (tm // HALO) - 1, 0), 0)),
            _const_spec(g2.shape), _const_spec(wup.shape),
            _const_spec(cg.shape), _const_spec(cu.shape), _const_spec(wd.shape), _const_spec(gf.shape),
        ],
        out_specs=pl.BlockSpec((1, tm, d), lambda bi, ji: (bi, ji, 0)),
        out_shape=jax.ShapeDtypeStruct((b, s, d), F32),
        scratch_shapes=[pltpu.VMEM((HALO + tm, d), BF16),
                        pltpu.VMEM((FFN_CHUNK // LANES, HALO + tm, LANES), F32),
                        pltpu.VMEM((FFN_CHUNK // LANES, HALO + tm, LANES), F32),
                        pltpu.VMEM((FFN_CHUNK // LANES, tm, LANES), F32),
                        pltpu.VMEM((tm, wd.shape[0]), BF16)],
        compiler_params=pltpu.CompilerParams(
            dimension_semantics=("parallel", "parallel"), vmem_limit_bytes=VMEM_LIMIT),
        name="ffn",
    )(x1, x1, g2, wup, cg, cu, wd, gf)


def _chunk_cols(w, chunk):
    kdim, n = w.shape
    return w.reshape(kdim, n // chunk, chunk).transpose(1, 0, 2)


def kernel(x, positions, norm_mix_g, w_in, conv_mlstm, i_bias, f_bias, mlstm_norm_g, w_branch_attn,
           w_branch_mlstm, w_out, norm_ffn_g, w_up, conv_ffn, w_down, norm_final_g):
    b, s, d = x.shape
    aw, mw, nh = ATTN_WIDTH, MLSTM_WIDTH, N_MLSTM_HEADS
    assert s % FFN_TILE == 0 and s % SEQ_TILE == 0 and d % LANES == 0
    n_qkv = 3 * aw + 4 * mw
    d_ff = w_down.shape[1]
    assert d_ff % FFN_CHUNK == 0
    half = ATTN_HEAD_DIM // 2
    inv_freq = ROPE_THETA ** (-(jnp.arange(half, dtype=F32) / half))
    cos, sin = _rope_tables(positions, inv_freq)

    for layer in range(w_in.shape[0]):
        wl = w_in[layer]
        w1 = jnp.concatenate(
            [wl[:, :n_qkv].astype(BF16), wl[:, n_qkv:n_qkv + 2 * nh].astype(BF16),
             jnp.zeros((d, LANES - 2 * nh), BF16)], axis=1)
        wg = wl[:, n_qkv + 2 * nh:].astype(BF16)
        bias = jnp.concatenate(
            [i_bias[layer], f_bias[layer], jnp.zeros((LANES - 2 * nh,), F32)]).reshape(1, LANES)

        qT, k, kmean, vT, qm, kmT, vm, om, gif, gifT = _in_proj(
            x, cos, sin, norm_mix_g[layer].reshape(1, d), w1, conv_mlstm[layer], bias)
        attn = _moba(qT, k, vT, kmean.reshape(b, s // MOBA_BLOCK, aw))
        y_m = _mlstm(qm, kmT, vm, om, gif, gifT, mlstm_norm_g[layer].reshape(1, mw))
        x1 = _mix(x.reshape(b * s, d), attn.reshape(b * s, aw), y_m.reshape(b * s, mw),
                  norm_mix_g[layer].reshape(1, d), wg, w_branch_attn[layer].astype(BF16),
                  w_branch_mlstm[layer].astype(BF16), w_out[layer].astype(BF16)).reshape(b, s, d)

        wu = w_up[layer].astype(BF16)
        cf = conv_ffn[layer]
        last = layer == w_in.shape[0] - 1
        gfin = norm_final_g.reshape(1, d) if last else None
        assert last, "only the final layer fuses the output norm"
        x = _ffn(x1, norm_ffn_g[layer].reshape(1, d), wu,
                 _chunk_cols(cf[:, :d_ff], FFN_CHUNK), _chunk_cols(cf[:, d_ff:], FFN_CHUNK),
                 w_down[layer].astype(BF16), gfin)
    return x
```

```python
import math

import jax
import jax.numpy as jnp
from jax import lax
from jax.experimental import pallas as pl
from jax.experimental.pallas import tpu as pltpu

F32 = jnp.float32
BF16 = jnp.bfloat16

EPS = 1e-6
NEG = -1e30
ROPE_THETA = 10000.0

N_ATTN_HEADS = 8
ATTN_HEAD_DIM = 64
ATTN_WIDTH = N_ATTN_HEADS * ATTN_HEAD_DIM
MOBA_BLOCK = 256
MOBA_TOPK = 3
N_MLSTM_HEADS = 4
MLSTM_HEAD_DIM = 128
MLSTM_WIDTH = N_MLSTM_HEADS * MLSTM_HEAD_DIM

LANES = 128
HALO = 16
SEQ_TILE = MOBA_BLOCK
MIX_TILE = 1024
FFN_TILE = 1024
FFN_CHUNK = 256
CONV_STRIDE = 4
SUM_ROWS = 16
MLSTM_BLOCK = 256
MLSTM_CHUNK = 128
MLSTM_ROWS = 1
MOBA_PAIRS_PER_STEP = 4
LOG2_E = math.log2(math.e)
VMEM_LIMIT = 56 * 1024 * 1024


def _dot(a, b):
    return jnp.dot(a, b, preferred_element_type=F32)


def _rms(x, g):
    ms = jnp.mean(x * x, axis=-1, keepdims=True)
    return x * lax.rsqrt(ms + EPS) * g


def _sigmoid(x):
    return 1.0 / (1.0 + jnp.exp(-x))


def _conv_block(u_ref, sl, r0, cw):
    ntap = cw.shape[0]
    rows = {k: u_ref[sl, pl.ds(HALO + r0 + k, 8, stride=CONV_STRIDE), :] for k in range(1 - ntap, CONV_STRIDE)}
    outs = []
    for k in range(CONV_STRIDE):
        out = cw[0:1] * rows[k - ntap + 1]
        for j in range(1, ntap):
            out = out + cw[j:j + 1] * rows[k - ntap + 1 + j]
        outs.append(out)
    return outs


def _const_spec(shape):
    nd = len(shape)
    return pl.BlockSpec(shape, lambda *_: (0,) * nd, pipeline_mode=pl.Buffered(1))


def _rope_table_kernel(pos_ref, invf_ref, cos_ref, sin_ref):
    rows, per_row = pos_ref.shape
    half = LANES // per_row
    lane = lax.broadcasted_iota(jnp.int32, (rows, LANES), 1)
    pos = pos_ref[...]
    pos_rep = pos[:, per_row - 1:per_row]
    for g in range(per_row - 2, -1, -1):
        pos_rep = jnp.where(lane < (g + 1) * half, pos[:, g:g + 1], pos_rep)
    ang = pos_rep.astype(F32) * invf_ref[...]
    for table, out_ref in ((jnp.cos(ang), cos_ref), (jnp.sin(ang), sin_ref)):
        for g in range(per_row):
            base = table if g == 0 else pltpu.roll(table, LANES - g * half, 1)
            width = half
            while width < LANES:
                base = jnp.where(lane < width, base, pltpu.roll(base, width, 1))
                width *= 2
            out_ref[pl.ds(g, rows, stride=per_row), :] = base


def _rope_tables(positions, inv_freq):
    b, s = positions.shape
    half = inv_freq.shape[0]
    per_row = LANES // half
    rows = b * s // per_row
    tile = min(rows, 512)
    invf = jnp.tile(inv_freq, per_row).reshape(1, LANES)
    out_spec = pl.BlockSpec((tile * per_row, LANES), lambda i: (i, 0))
    cos, sin = pl.pallas_call(
        _rope_table_kernel,
        grid=(rows // tile,),
        in_specs=[pl.BlockSpec((tile, per_row), lambda i: (i, 0)), _const_spec(invf.shape)],
        out_specs=[out_spec, out_spec],
        out_shape=[jax.ShapeDtypeStruct((b * s, LANES), F32)] * 2,
        compiler_params=pltpu.CompilerParams(dimension_semantics=("parallel",)),
        name="rope_tables",
    )(positions.reshape(rows, per_row), invf)
    return cos.reshape(b, s, LANES), sin.reshape(b, s, LANES)


def _in_proj_kernel(x_ref, xh_ref, cos_ref, sin_ref, g_ref, w_ref, cw_ref, bias_ref,
                    qT_ref, k_ref, kmean_ref, vT_ref, qm_ref, kmT_ref, vm_ref, om_ref, gif_ref, gifT_ref,
                    ext_ref, act_ref):
    tm = SEQ_TILE
    aw, mw = ATTN_WIDTH, MLSTM_WIDTH
    j = pl.program_id(1)
    g = g_ref[...]
    h = _rms(x_ref[0], g).astype(BF16)

    cos = cos_ref[0]
    sin = sin_ref[0]
    lane = lax.broadcasted_iota(jnp.int32, (tm, LANES), 1)
    first_half = (lane % ATTN_HEAD_DIM) < (ATTN_HEAD_DIM // 2)
    sin_signed = jnp.where(first_half, -sin, sin)

    def rope(t):
        outs = []
        for c in range(aw // LANES):
            tg = t[:, c * LANES:(c + 1) * LANES]
            swapped = jnp.where(first_half, pltpu.roll(tg, LANES - 32, 1), pltpu.roll(tg, 32, 1))
            outs.append(tg * cos + swapped * sin_signed)
        return jnp.concatenate(outs, axis=1)

    qk = _dot(h, w_ref[:, 0:2 * aw])
    q = rope(qk[:, :aw]) * (LOG2_E / math.sqrt(ATTN_HEAD_DIM))
    k = rope(qk[:, aw:])
    qT_ref[0, 0] = q.T.astype(BF16)
    k_ref[0, 0] = k.astype(BF16)
    kmean_ref[0] = jnp.mean(k, axis=0, keepdims=True)
    v = _dot(h, w_ref[:, 2 * aw:3 * aw])
    vT_ref[0, 0] = v.T.astype(BF16)

    c0 = 3 * aw
    pm = _dot(h, w_ref[:, c0:c0 + 2 * mw])
    hh = _rms(xh_ref[0], g).astype(BF16)
    ph = jnp.where(j == 0, 0.0, _dot(hh, w_ref[:, c0:c0 + 2 * mw]))
    cw = cw_ref[...]
    nslab = 2 * mw // LANES
    for sl in range(nslab):
        cols = slice(sl * LANES, (sl + 1) * LANES)
        ext_ref[sl, 0:HALO, :] = ph[:, cols]
        ext_ref[sl, HALO:HALO + tm, :] = pm[:, cols]
    for sl in range(nslab):
        cws = cw[:, sl * LANES:(sl + 1) * LANES]
        for r0 in range(0, tm, 8 * CONV_STRIDE):
            for kk, conv in enumerate(_conv_block(ext_ref, sl, r0, cws)):
                act_ref[sl, pl.ds(r0 + kk, 8, stride=CONV_STRIDE), :] = conv * _sigmoid(conv)
    q_act = jnp.concatenate([act_ref[sl] for sl in range(nslab // 2)], axis=1)
    k_act = jnp.concatenate([act_ref[sl] for sl in range(nslab // 2, nslab)], axis=1)
    qm_ref[0] = (q_act * (1.0 / math.sqrt(MLSTM_HEAD_DIM))).astype(BF16)
    kmT_ref[0, 0] = k_act.T.astype(BF16)

    c1 = c0 + 2 * mw
    vo = _dot(h, w_ref[:, c1:c1 + 2 * mw])
    vm_ref[0] = vo[:, :mw].astype(BF16)
    om_ref[0] = vo[:, mw:]

    c2 = c1 + 2 * mw
    gi = _dot(h, w_ref[:, c2:c2 + LANES]) + bias_ref[...]
    log_sig = jnp.minimum(gi, 0.0) - jnp.log1p(jnp.exp(-jnp.abs(gi)))
    gt = jnp.where(lane < N_MLSTM_HEADS, gi, log_sig)
    gif_ref[0] = gt[:, 0:2 * N_MLSTM_HEADS]
    gifT_ref[0] = gt.T[0:2 * N_MLSTM_HEADS, :]


def _in_proj(x, cos, sin, g, w, cw, bias):
    b, s, d = x.shape
    tm = SEQ_TILE
    nb = s // tm
    aw, mw = ATTN_WIDTH, MLSTM_WIDTH
    nh2 = 2 * N_MLSTM_HEADS
    out_shape = [
        jax.ShapeDtypeStruct((b, nb, aw, tm), BF16),
        jax.ShapeDtypeStruct((b, nb, tm, aw), BF16),
        jax.ShapeDtypeStruct((b * nb, 1, aw), F32),
        jax.ShapeDtypeStruct((b, nb, aw, tm), BF16),
        jax.ShapeDtypeStruct((b, s, mw), BF16),
        jax.ShapeDtypeStruct((b, nb, mw, tm), BF16),
        jax.ShapeDtypeStruct((b, s, mw), BF16),
        jax.ShapeDtypeStruct((b, s, mw), F32),
        jax.ShapeDtypeStruct((b, s, nh2), F32),
        jax.ShapeDtypeStruct((b, nh2, s), F32),
    ]
    blk4 = lambda r, c: pl.BlockSpec((1, 1, r, c), lambda bi, ji: (bi, ji, 0, 0))
    row3 = lambda c: pl.BlockSpec((1, tm, c), lambda bi, ji: (bi, ji, 0))
    out_specs = [
        blk4(aw, tm), blk4(tm, aw),
        pl.BlockSpec((1, 1, aw), lambda bi, ji: (bi * nb + ji, 0, 0)),
        blk4(aw, tm), row3(mw), blk4(mw, tm), row3(mw), row3(mw), row3(nh2),
        pl.BlockSpec((1, nh2, tm), lambda bi, ji: (bi, 0, ji)),
    ]
    in_specs = [
        pl.BlockSpec((1, tm, d), lambda bi, ji: (bi, ji, 0)),
        pl.BlockSpec((1, HALO, d), lambda bi, ji: (bi, jnp.maximum(ji * (tm // HALO) - 1, 0), 0)),
        row3(LANES), row3(LANES),
        _const_spec(g.shape), _const_spec(w.shape), _const_spec(cw.shape), _const_spec(bias.shape),
    ]
    return pl.pallas_call(
        _in_proj_kernel,
        grid=(b, nb),
        in_specs=in_specs,
        out_specs=out_specs,
        out_shape=out_shape,
        scratch_shapes=[pltpu.VMEM((2 * mw // LANES, HALO + tm, LANES), F32),
                        pltpu.VMEM((2 * mw // LANES, tm, LANES), F32)],
        compiler_params=pltpu.CompilerParams(
            dimension_semantics=("parallel", "parallel"), vmem_limit_bytes=VMEM_LIMIT),
        name="in_proj",
    )(x, x, cos, sin, g, w, cw, bias)


def _moba_kernel(jtab_ref, ntab_ref, qT_ref, k_ref, vT_ref, kmean_ref, out_ref,
                 bias_ref, m_ref, acc_ref, s_ref, cm_ref, p_ref, a_ref):
    blk = MOBA_BLOCK
    hd = ATTN_HEAD_DIM
    nb = k_ref.shape[1]
    nh = 2 * (qT_ref.shape[2] // LANES)
    n_units = nb * (nb - 1) // 2
    dim_row = lax.broadcasted_iota(jnp.int32, (LANES, blk), 0)
    dim_col = lax.broadcasted_iota(jnp.int32, (blk, LANES), 1)
    blk_id = lax.broadcasted_iota(jnp.int32, (nb, blk), 0)
    blk_idf = blk_id.astype(F32)
    key_row = lax.broadcasted_iota(jnp.int32, (blk, blk), 0)
    qry_col = lax.broadcasted_iota(jnp.int32, (blk, blk), 1)

    def lanes(h):
        return slice((h // 2) * LANES, (h // 2 + 1) * LANES)

    ones_rows = jnp.ones((SUM_ROWS, blk), BF16)

    def v_ext(n, h):
        return jnp.concatenate([vT_ref[0, n, h * hd:(h + 1) * hd, :], ones_rows], axis=0)

    def phase_a(j, carry):
        for h in range(nh):
            qT = qT_ref[0, j, lanes(h), :]
            qh = jnp.where((dim_row >= hd) == (h % 2 == 1), qT, jnp.zeros_like(qT))
            km = kmean_ref[0, :, lanes(h)]
            km_hi = km.astype(BF16)
            km_lo = (km - km_hi.astype(F32)).astype(BF16)
            gate = _dot(km_hi, qh) + _dot(km_lo, qh)
            gate = jnp.where(blk_id < j, gate, NEG)
            picked = blk_id < 0
            for _ in range(min(MOBA_TOPK, nb)):
                top = jnp.max(gate, axis=0, keepdims=True)
                first = jnp.min(jnp.where(gate == top, blk_idf, float(nb)), axis=0, keepdims=True)
                hit = blk_idf == first
                picked = picked | hit
                gate = jnp.where(hit, -jnp.inf, gate)
            selected = picked & (blk_id < j)
            bias_ref[h, j, 0:nb, :] = jnp.where(selected, 0.0, NEG)
            bias_ref[h, j, nb:nb + 1, :] = jnp.full((1, blk), NEG, F32)
            s_own = jnp.where(key_row <= qry_col, _dot(k_ref[0, j, :, lanes(h)], qh), NEG)
            m0 = jnp.max(s_own, axis=0, keepdims=True)
            p0 = jnp.exp2(s_own - m0)
            m_ref[h, j] = m0
            acc_ref[h, j, 0:hd, :] = _dot(vT_ref[0, j, h * hd:(h + 1) * hd, :], p0.astype(BF16))
            acc_ref[h, j, hd:hd + SUM_ROWS, :] = jnp.broadcast_to(jnp.sum(p0, axis=0, keepdims=True), (SUM_ROWS, blk))
        return carry

    lax.fori_loop(0, nb, phase_a, 0)

    for h in range(nh):
        s_ref[h] = jnp.zeros((blk, blk), F32)
        cm_ref[h] = jnp.zeros((1, blk), F32)
        p_ref[h] = jnp.zeros((blk, blk), BF16)
        a_ref[h] = jnp.ones((1, blk), F32)

    def phase_b(i, carry):
        j1, n1 = jtab_ref[i + 2], ntab_ref[i + 2]
        j2, n2 = jtab_ref[i + 1], ntab_ref[i + 1]
        j3, n3 = jtab_ref[i], ntab_ref[i]
        k_blk = jnp.minimum(n1, nb - 1)
        v_blk = jnp.minimum(n3, nb - 1)
        for h in range(nh):
            acc_ref[h, j3] = a_ref[h] * acc_ref[h, j3] + _dot(v_ext(v_blk, h), p_ref[h])
        for h in range(nh):
            bias = bias_ref[h, j2, pl.ds(n2, 1), :]
            m_old = m_ref[h, j2]
            m_new = jnp.maximum(m_old, cm_ref[h] + bias)
            alpha = jnp.exp2(m_old - m_new)
            p = jnp.exp2(s_ref[h] - (m_new - bias))
            m_ref[h, j2] = m_new
            a_ref[h] = alpha
            p_ref[h] = p.astype(BF16)
        for h in range(nh):
            kb = k_ref[0, k_blk, :, lanes(h)]
            kb = jnp.where((dim_col >= hd) == (h % 2 == 1), kb, jnp.zeros_like(kb))
            sT = _dot(kb, qT_ref[0, j1, lanes(h), :])
            s_ref[h] = sT
            cm_ref[h] = jnp.max(sT, axis=0, keepdims=True)
        return carry

    lax.fori_loop(0, n_units + 2, phase_b, 0)

    def phase_c(j, carry):
        row0 = pl.multiple_of(j * blk, blk)
        for pair in range(nh // 2):
            oT = jnp.concatenate([acc_ref[h, j, 0:hd, :] / acc_ref[h, j, hd:hd + 1, :]
                                  for h in (2 * pair, 2 * pair + 1)], axis=0)
            out_ref[0, pl.ds(row0, blk), pair * LANES:(pair + 1) * LANES] = oT.T.astype(BF16)
        return carry

    lax.fori_loop(0, nb, phase_c, 0)


def _moba_units(nb):
    pad = (0, nb)
    units = [pad, pad] + [(j, n) for j in range(nb) for n in range(j)] + [pad, pad]
    jtab = jnp.asarray([u[0] for u in units], jnp.int32)
    ntab = jnp.asarray([u[1] for u in units], jnp.int32)
    return jtab, ntab


def _moba(qT, k, vT, kmean):
    b, nb, aw, blk = qT.shape
    s = nb * blk
    w = MOBA_PAIRS_PER_STEP * LANES
    nh = 2 * MOBA_PAIRS_PER_STEP
    jtab, ntab = _moba_units(nb)
    grid_spec = pltpu.PrefetchScalarGridSpec(
        num_scalar_prefetch=2,
        grid=(b, aw // w),
        in_specs=[
            pl.BlockSpec((1, nb, w, blk), lambda bi, pi, *_: (bi, 0, pi, 0)),
            pl.BlockSpec((1, nb, blk, w), lambda bi, pi, *_: (bi, 0, 0, pi)),
            pl.BlockSpec((1, nb, w, blk), lambda bi, pi, *_: (bi, 0, pi, 0)),
            pl.BlockSpec((1, nb, w), lambda bi, pi, *_: (bi, 0, pi)),
        ],
        out_specs=pl.BlockSpec((1, s, w), lambda bi, pi, *_: (bi, 0, pi)),
        scratch_shapes=[pltpu.VMEM((nh, nb, nb + 8, blk), F32),
                        pltpu.VMEM((nh, nb, 1, blk), F32),
                        pltpu.VMEM((nh, nb, ATTN_HEAD_DIM + SUM_ROWS, blk), F32),
                        pltpu.VMEM((nh, blk, blk), F32),
                        pltpu.VMEM((nh, 1, blk), F32),
                        pltpu.VMEM((nh, blk, blk), BF16),
                        pltpu.VMEM((nh, 1, blk), F32)],
    )
    return pl.pallas_call(
        _moba_kernel,
        grid_spec=grid_spec,
        out_shape=jax.ShapeDtypeStruct((b, s, aw), BF16),
        compiler_params=pltpu.CompilerParams(
            dimension_semantics=("parallel", "parallel"), vmem_limit_bytes=VMEM_LIMIT),
        name="moba",
    )(jtab, ntab, qT, k, vT, kmean)


def _split3(x):
    hi = x.astype(BF16)
    r = x - hi.astype(F32)
    mid = r.astype(BF16)
    lo = (r - mid.astype(F32)).astype(BF16)
    return hi, mid, lo


def _mlstm_kernel(q_ref, kT_ref, v_ref, o_ref, gif_ref, gifT_ref, g_ref, out_ref, c_ref, m_ref):
    L = MLSTM_CHUNK
    nh, hd = N_MLSTM_HEADS, MLSTM_HEAD_DIM
    c = pl.program_id(1)

    @pl.when(c == 0)
    def _():
        c_ref[...] = jnp.zeros_like(c_ref)
        m_ref[...] = jnp.zeros_like(m_ref)

    t_idx = lax.broadcasted_iota(jnp.int32, (L, L), 0)
    s_idx = lax.broadcasted_iota(jnp.int32, (L, L), 1)
    causal = s_idx <= t_idx
    tril = jnp.where(causal, 1.0, 0.0).astype(BF16)
    triu = jnp.where(t_idx <= s_idx, 1.0, 0.0).astype(BF16)

    ones_col = jnp.where(lax.broadcasted_iota(jnp.int32, (L, hd), 1) == 0, 1.0, 0.0).astype(BF16)
    gn = g_ref[...]

    for r, r0 in [(r, r0) for r in range(q_ref.shape[0]) for r0 in range(0, q_ref.shape[1], L)]:
        rows = slice(r0, r0 + L)
        gi_col = gif_ref[r, rows, :]
        gi_row = gifT_ref[r, :, rows]
        b_col_all = sum(_dot(tril, part) for part in _split3(gi_col))
        b_row_all = sum(_dot(part, triu) for part in _split3(gi_row))

        for h in range(nh):
            cols = slice(h * hd, (h + 1) * hd)
            a_row = gi_row[h:h + 1, :]
            b_row = b_row_all[nh + h:nh + h + 1, :]
            b_col = b_col_all[:, nh + h:nh + h + 1]
            q = q_ref[r, rows, cols]
            if L >= SEQ_TILE:
                kT = jnp.concatenate([kT_ref[r, (r0 + i) // SEQ_TILE, cols, :] for i in range(0, L, SEQ_TILE)], axis=1)
            else:
                kT = kT_ref[r, r0 // SEQ_TILE, cols, r0 % SEQ_TILE:r0 % SEQ_TILE + L]
            v_ext = jnp.concatenate([v_ref[r, rows, cols], ones_col], axis=1)

            dmat = jnp.where(causal, b_col - b_row + a_row, NEG)
            mi = jnp.max(dmat, axis=1, keepdims=True)
            intra = _dot((jnp.exp(dmat - mi) * _dot(q, kT)).astype(BF16), v_ext)
            b_last = b_row[:, L - 1:L]
            dec = b_last - b_row + a_row
            md = jnp.max(dec, axis=1, keepdims=True)
            kv = _dot((kT.astype(F32) * jnp.exp(dec - md)).astype(BF16), v_ext)

            m_prev = m_ref[r * nh + h]
            c_old = c_ref[r * nh + h]
            inter = b_col + m_prev
            m_t = jnp.maximum(inter, mi)
            num_ext = jnp.exp(inter - m_t) * _dot(q, c_old.astype(BF16)) + jnp.exp(mi - m_t) * intra
            num = num_ext[:, :hd]
            den = num_ext[:, hd:hd + 1]
            h_out = num / jnp.maximum(jnp.abs(den), jnp.exp(-m_t))

            m_new = jnp.maximum(b_last + m_prev, md)
            c_ref[r * nh + h] = jnp.exp(b_last + m_prev - m_new) * c_old + jnp.exp(md - m_new) * kv
            m_ref[r * nh + h] = m_new

            hn = _rms(h_out, gn[:, cols])
            gate_o = _sigmoid(o_ref[r, rows, cols])
            out_ref[r, rows, cols] = (hn * gate_o).astype(BF16)


def _mlstm(qm, kmT, vm, om, gif, gifT, g):
    b, s, mw = qm.shape
    L = MLSTM_BLOCK
    nc = s // L
    nh2 = 2 * N_MLSTM_HEADS
    rows = MLSTM_ROWS if b % MLSTM_ROWS == 0 else 1
    row = lambda cdim: pl.BlockSpec((rows, L, cdim), lambda bi, ci: (bi, ci, 0))
    return pl.pallas_call(
        _mlstm_kernel,
        grid=(b // rows, nc),
        in_specs=[
            row(mw),
            pl.BlockSpec((rows, L // SEQ_TILE, mw, SEQ_TILE), lambda bi, ci: (bi, ci, 0, 0)),
            row(mw), row(mw), row(nh2),
            pl.BlockSpec((rows, nh2, L), lambda bi, ci: (bi, 0, ci)),
            _const_spec(g.shape),
        ],
        out_specs=row(mw),
        out_shape=jax.ShapeDtypeStruct((b, s, mw), BF16),
        scratch_shapes=[pltpu.VMEM((rows * N_MLSTM_HEADS, MLSTM_HEAD_DIM, 2 * MLSTM_HEAD_DIM), F32),
                        pltpu.VMEM((rows * N_MLSTM_HEADS, 1, 1), F32)],
        compiler_params=pltpu.CompilerParams(
            dimension_semantics=("parallel", "arbitrary"), vmem_limit_bytes=VMEM_LIMIT),
        name="mlstm",
    )(qm, kmT, vm, om, gif, gifT, g)


def _mix_kernel(x_ref, a_ref, y_ref, g_ref, wg_ref, wa_ref, wm_ref, wo_ref, out_ref):
    d = x_ref.shape[-1]
    x = x_ref[...]
    h = _rms(x, g_ref[...]).astype(BF16)
    gate_a = _sigmoid(_dot(h, wg_ref[:, :d]))
    gate_m = _sigmoid(_dot(h, wg_ref[:, d:]))
    merged = gate_a * _dot(a_ref[...], wa_ref[...]) + gate_m * _dot(y_ref[...], wm_ref[...])
    out_ref[...] = x + _dot(merged.astype(BF16), wo_ref[...])


def _mix(x2d, a2d, y2d, g, wg, wa, wm, wo):
    t, d = x2d.shape
    tm = MIX_TILE
    row = lambda cdim: pl.BlockSpec((tm, cdim), lambda i: (i, 0))
    return pl.pallas_call(
        _mix_kernel,
        grid=(t // tm,),
        in_specs=[row(d), row(a2d.shape[1]), row(y2d.shape[1]), _const_spec(g.shape),
                  _const_spec(wg.shape), _const_spec(wa.shape), _const_spec(wm.shape), _const_spec(wo.shape)],
        out_specs=row(d),
        out_shape=jax.ShapeDtypeStruct((t, d), F32),
        compiler_params=pltpu.CompilerParams(
            dimension_semantics=("parallel",), vmem_limit_bytes=VMEM_LIMIT),
        name="mix",
    )(x2d, a2d, y2d, g, wg, wa, wm, wo)


def _ffn_kernel(x_ref, xh_ref, g_ref, wup_ref, cg_ref, cu_ref, wd_ref, gf_ref, out_ref,
                hext_ref, ug_ref, uu_ref, act_ref, actall_ref):
    tm = FFN_TILE
    nslab = ug_ref.shape[0]
    nchunk = actall_ref.shape[1] // FFN_CHUNK
    j = pl.program_id(1)
    x = x_ref[0]
    g = g_ref[...]
    hext_ref[HALO:HALO + tm, :] = _rms(x, g).astype(BF16)
    hh = _rms(xh_ref[0], g)
    hext_ref[0:HALO, :] = jnp.where(j == 0, 0.0, hh).astype(BF16)

    def stage_a(c):
        gate_cols = pl.ds(pl.multiple_of(c * FFN_CHUNK, LANES), FFN_CHUNK)
        up_cols = pl.ds(pl.multiple_of(nchunk * FFN_CHUNK + c * FFN_CHUNK, LANES), FFN_CHUNK)
        ug = _dot(hext_ref[...], wup_ref[:, gate_cols])
        uu = _dot(hext_ref[...], wup_ref[:, up_cols])
        for sl in range(nslab):
            ug_ref[sl] = ug[:, sl * LANES:(sl + 1) * LANES]
            uu_ref[sl] = uu[:, sl * LANES:(sl + 1) * LANES]

    def stage_b(c):
        for sl in range(nslab):
            cg = cg_ref[c][:, sl * LANES:(sl + 1) * LANES]
            cu = cu_ref[c][:, sl * LANES:(sl + 1) * LANES]
            for r0 in range(0, tm, 8 * CONV_STRIDE):
                gates = _conv_block(ug_ref, sl, r0, cg)
                ups = _conv_block(uu_ref, sl, r0, cu)
                for k in range(CONV_STRIDE):
                    act_ref[sl, pl.ds(r0 + k, 8, stride=CONV_STRIDE), :] = gates[k] * _sigmoid(gates[k]) * ups[k]
        act = jnp.concatenate([act_ref[sl] for sl in range(nslab)], axis=1)
        actall_ref[:, pl.ds(pl.multiple_of(c * FFN_CHUNK, LANES), FFN_CHUNK)] = act.astype(BF16)

    stage_a(0)

    def body(c, carry):
        stage_b(c - 1)
        stage_a(c)
        return carry

    lax.fori_loop(1, nchunk, body, 0)
    stage_b(nchunk - 1)
    out_ref[0] = _rms(x + _dot(actall_ref[...], wd_ref[...]), gf_ref[...])


def _ffn(x1, g2, wup, cg, cu, wd, gf):
    b, s, d = x1.shape
    tm = FFN_TILE
    return pl.pallas_call(
        _ffn_kernel,
        grid=(b, s // tm),
        in_specs=[
            pl.BlockSpec((1, tm, d), lambda bi, ji: (bi, ji, 0)),
            pl.BlockSpec((1, HALO, d), lambda bi, ji: (bi, jnp.maximum(ji * (tm // HALO) - 1, 0), 0)),
            _const_spec(g2.shape), _const_spec(wup.shape),
            _const_spec(cg.shape), _const_spec(cu.shape), _const_spec(wd.shape), _const_spec(gf.shape),
        ],
        out_specs=pl.BlockSpec((1, tm, d), lambda bi, ji: (bi, ji, 0)),
        out_shape=jax.ShapeDtypeStruct((b, s, d), F32),
        scratch_shapes=[pltpu.VMEM((HALO + tm, d), BF16),
                        pltpu.VMEM((FFN_CHUNK // LANES, HALO + tm, LANES), F32),
                        pltpu.VMEM((FFN_CHUNK // LANES, HALO + tm, LANES), F32),
                        pltpu.VMEM((FFN_CHUNK // LANES, tm, LANES), F32),
                        pltpu.VMEM((tm, wd.shape[0]), BF16)],
        compiler_params=pltpu.CompilerParams(
            dimension_semantics=("parallel", "parallel"), vmem_limit_bytes=VMEM_LIMIT),
        name="ffn",
    )(x1, x1, g2, wup, cg, cu, wd, gf)


def _chunk_cols(w, chunk):
    kdim, n = w.shape
    return w.reshape(kdim, n // chunk, chunk).transpose(1, 0, 2)


def kernel(x, positions, norm_mix_g, w_in, conv_mlstm, i_bias, f_bias, mlstm_norm_g, w_branch_attn,
           w_branch_mlstm, w_out, norm_ffn_g, w_up, conv_ffn, w_down, norm_final_g):
    b, s, d = x.shape
    aw, mw, nh = ATTN_WIDTH, MLSTM_WIDTH, N_MLSTM_HEADS
    assert s % FFN_TILE == 0 and s % SEQ_TILE == 0 and s % MLSTM_BLOCK == 0 and (b * s) % MIX_TILE == 0 and d % LANES == 0
    assert w_in.shape[0] == 1, "the ffn kernel fuses the output norm, so exactly one layer is supported"
    n_qkv = 3 * aw + 4 * mw
    d_ff = w_down.shape[1]
    assert d_ff % FFN_CHUNK == 0
    half = ATTN_HEAD_DIM // 2
    inv_freq = ROPE_THETA ** (-(jnp.arange(half, dtype=F32) / half))
    cos, sin = _rope_tables(positions, inv_freq)

    for layer in range(w_in.shape[0]):
        wl = w_in[layer]
        w1 = jnp.concatenate(
            [wl[:, :n_qkv].astype(BF16), wl[:, n_qkv:n_qkv + 2 * nh].astype(BF16),
             jnp.zeros((d, LANES - 2 * nh), BF16)], axis=1)
        wg = wl[:, n_qkv + 2 * nh:].astype(BF16)
        bias = jnp.concatenate(
            [i_bias[layer], f_bias[layer], jnp.zeros((LANES - 2 * nh,), F32)]).reshape(1, LANES)

        qT, k, kmean, vT, qm, kmT, vm, om, gif, gifT = _in_proj(
            x, cos, sin, norm_mix_g[layer].reshape(1, d), w1, conv_mlstm[layer], bias)
        attn = _moba(qT, k, vT, kmean.reshape(b, s // MOBA_BLOCK, aw))
        y_m = _mlstm(qm, kmT, vm, om, gif, gifT, mlstm_norm_g[layer].reshape(1, mw))
        x1 = _mix(x.reshape(b * s, d), attn.reshape(b * s, aw), y_m.reshape(b * s, mw),
                  norm_mix_g[layer].reshape(1, d), wg, w_branch_attn[layer].astype(BF16),
                  w_branch_mlstm[layer].astype(BF16), w_out[layer].astype(BF16)).reshape(b, s, d)

        wu = w_up[layer].astype(BF16)
        cf = conv_ffn[layer]
        x = _ffn(x1, norm_ffn_g[layer].reshape(1, d), wu,
                 _chunk_cols(cf[:, :d_ff], FFN_CHUNK), _chunk_cols(cf[:, d_ff:], FFN_CHUNK),
                 w_down[layer].astype(BF16), norm_final_g.reshape(1, d))
    return x
```

```python
import math

import jax
import jax.numpy as jnp
from jax import lax
from jax.experimental import pallas as pl
from jax.experimental.pallas import tpu as pltpu

F32 = jnp.float32
BF16 = jnp.bfloat16

EPS = 1e-6
NEG = -1e30
ROPE_THETA = 10000.0

N_ATTN_HEADS = 8
ATTN_HEAD_DIM = 64
ATTN_WIDTH = N_ATTN_HEADS * ATTN_HEAD_DIM
MOBA_BLOCK = 256
MOBA_TOPK = 3
N_MLSTM_HEADS = 4
MLSTM_HEAD_DIM = 128
MLSTM_WIDTH = N_MLSTM_HEADS * MLSTM_HEAD_DIM

LANES = 128
HALO = 16
SEQ_TILE = MOBA_BLOCK
MIX_TILE = 1024
FFN_TILE = 1024
FFN_CHUNK = 256
CONV_STRIDE = 4
SUM_ROWS = 16
MLSTM_BLOCK = 256
MLSTM_CHUNK = 128
MLSTM_ROWS = 1
MOBA_PAIRS_PER_STEP = 4
LOG2_E = math.log2(math.e)
VMEM_LIMIT = 56 * 1024 * 1024


def _dot(a, b):
    return jnp.dot(a, b, preferred_element_type=F32)


def _rms(x, g):
    ms = jnp.mean(x * x, axis=-1, keepdims=True)
    return x * lax.rsqrt(ms + EPS) * g


def _sigmoid(x):
    return 1.0 / (1.0 + jnp.exp(-x))


def _conv_block(u_ref, sl, r0, cw):
    ntap = cw.shape[0]
    rows = {k: u_ref[sl, pl.ds(HALO + r0 + k, 8, stride=CONV_STRIDE), :] for k in range(1 - ntap, CONV_STRIDE)}
    outs = []
    for k in range(CONV_STRIDE):
        out = cw[0:1] * rows[k - ntap + 1]
        for j in range(1, ntap):
            out = out + cw[j:j + 1] * rows[k - ntap + 1 + j]
        outs.append(out)
    return outs


def _const_spec(shape):
    nd = len(shape)
    return pl.BlockSpec(shape, lambda *_: (0,) * nd, pipeline_mode=pl.Buffered(1))


def _rope_table_kernel(pos_ref, invf_ref, cos_ref, sin_ref):
    rows, per_row = pos_ref.shape
    half = LANES // per_row
    lane = lax.broadcasted_iota(jnp.int32, (rows, LANES), 1)
    pos = pos_ref[...]
    pos_rep = pos[:, per_row - 1:per_row]
    for g in range(per_row - 2, -1, -1):
        pos_rep = jnp.where(lane < (g + 1) * half, pos[:, g:g + 1], pos_rep)
    ang = pos_rep.astype(F32) * invf_ref[...]
    for table, out_ref in ((jnp.cos(ang), cos_ref), (jnp.sin(ang), sin_ref)):
        for g in range(per_row):
            base = table if g == 0 else pltpu.roll(table, LANES - g * half, 1)
            width = half
            while width < LANES:
                base = jnp.where(lane < width, base, pltpu.roll(base, width, 1))
                width *= 2
            out_ref[pl.ds(g, rows, stride=per_row), :] = base


def _rope_tables(positions, inv_freq):
    b, s = positions.shape
    half = inv_freq.shape[0]
    per_row = LANES // half
    rows = b * s // per_row
    tile = min(rows, 512)
    invf = jnp.tile(inv_freq, per_row).reshape(1, LANES)
    out_spec = pl.BlockSpec((tile * per_row, LANES), lambda i: (i, 0))
    cos, sin = pl.pallas_call(
        _rope_table_kernel,
        grid=(rows // tile,),
        in_specs=[pl.BlockSpec((tile, per_row), lambda i: (i, 0)), _const_spec(invf.shape)],
        out_specs=[out_spec, out_spec],
        out_shape=[jax.ShapeDtypeStruct((b * s, LANES), F32)] * 2,
        compiler_params=pltpu.CompilerParams(dimension_semantics=("parallel",)),
        name="rope_tables",
    )(positions.reshape(rows, per_row), invf)
    return cos.reshape(b, s, LANES), sin.reshape(b, s, LANES)


def _in_proj_kernel(x_ref, xh_ref, cos_ref, sin_ref, g_ref, w_ref, cw_ref, bias_ref,
                    qT_ref, k_ref, kmean_ref, vT_ref, qm_ref, kmT_ref, vm_ref, om_ref, gif_ref, gifT_ref,
                    ext_ref, act_ref):
    tm = SEQ_TILE
    aw, mw = ATTN_WIDTH, MLSTM_WIDTH
    j = pl.program_id(1)
    g = g_ref[...]
    h = _rms(x_ref[0], g).astype(BF16)

    cos = cos_ref[0]
    sin = sin_ref[0]
    lane = lax.broadcasted_iota(jnp.int32, (tm, LANES), 1)
    first_half = (lane % ATTN_HEAD_DIM) < (ATTN_HEAD_DIM // 2)
    sin_signed = jnp.where(first_half, -sin, sin)

    def rope(t):
        outs = []
        for c in range(aw // LANES):
            tg = t[:, c * LANES:(c + 1) * LANES]
            swapped = jnp.where(first_half, pltpu.roll(tg, LANES - 32, 1), pltpu.roll(tg, 32, 1))
            outs.append(tg * cos + swapped * sin_signed)
        return jnp.concatenate(outs, axis=1)

    qk = _dot(h, w_ref[:, 0:2 * aw])
    q = rope(qk[:, :aw]) * (LOG2_E / math.sqrt(ATTN_HEAD_DIM))
    k = rope(qk[:, aw:])
    qT_ref[0, 0] = q.T.astype(BF16)
    k_ref[0, 0] = k.astype(BF16)
    kmean_ref[0] = jnp.mean(k, axis=0, keepdims=True)
    v = _dot(h, w_ref[:, 2 * aw:3 * aw])
    vT_ref[0, 0] = v.T.astype(BF16)

    c0 = 3 * aw
    pm = _dot(h, w_ref[:, c0:c0 + 2 * mw])
    hh = _rms(xh_ref[0], g).astype(BF16)
    ph = jnp.where(j == 0, 0.0, _dot(hh, w_ref[:, c0:c0 + 2 * mw]))
    cw = cw_ref[...]
    nslab = 2 * mw // LANES
    for sl in range(nslab):
        cols = slice(sl * LANES, (sl + 1) * LANES)
        ext_ref[sl, 0:HALO, :] = ph[:, cols]
        ext_ref[sl, HALO:HALO + tm, :] = pm[:, cols]
    for sl in range(nslab):
        cws = cw[:, sl * LANES:(sl + 1) * LANES]
        for r0 in range(0, tm, 8 * CONV_STRIDE):
            for kk, conv in enumerate(_conv_block(ext_ref, sl, r0, cws)):
                act_ref[sl, pl.ds(r0 + kk, 8, stride=CONV_STRIDE), :] = conv * _sigmoid(conv)
    q_act = jnp.concatenate([act_ref[sl] for sl in range(nslab // 2)], axis=1)
    k_act = jnp.concatenate([act_ref[sl] for sl in range(nslab // 2, nslab)], axis=1)
    qm_ref[0] = (q_act * (1.0 / math.sqrt(MLSTM_HEAD_DIM))).astype(BF16)
    kmT_ref[0, 0] = k_act.T.astype(BF16)

    c1 = c0 + 2 * mw
    vo = _dot(h, w_ref[:, c1:c1 + 2 * mw])
    vm_ref[0] = vo[:, :mw].astype(BF16)
    om_ref[0] = vo[:, mw:]

    c2 = c1 + 2 * mw
    gi = _dot(h, w_ref[:, c2:c2 + LANES]) + bias_ref[...]
    log_sig = jnp.minimum(gi, 0.0) - jnp.log1p(jnp.exp(-jnp.abs(gi)))
    gt = jnp.where(lane < N_MLSTM_HEADS, gi, log_sig)
    gif_ref[0] = gt[:, 0:2 * N_MLSTM_HEADS]
    gifT_ref[0] = gt.T[0:2 * N_MLSTM_HEADS, :]


def _in_proj(x, cos, sin, g, w, cw, bias):
    b, s, d = x.shape
    tm = SEQ_TILE
    nb = s // tm
    aw, mw = ATTN_WIDTH, MLSTM_WIDTH
    nh2 = 2 * N_MLSTM_HEADS
    out_shape = [
        jax.ShapeDtypeStruct((b, nb, aw, tm), BF16),
        jax.ShapeDtypeStruct((b, nb, tm, aw), BF16),
        jax.ShapeDtypeStruct((b * nb, 1, aw), F32),
        jax.ShapeDtypeStruct((b, nb, aw, tm), BF16),
        jax.ShapeDtypeStruct((b, s, mw), BF16),
        jax.ShapeDtypeStruct((b, nb, mw, tm), BF16),
        jax.ShapeDtypeStruct((b, s, mw), BF16),
        jax.ShapeDtypeStruct((b, s, mw), F32),
        jax.ShapeDtypeStruct((b, s, nh2), F32),
        jax.ShapeDtypeStruct((b, nh2, s), F32),
    ]
    blk4 = lambda r, c: pl.BlockSpec((1, 1, r, c), lambda bi, ji: (bi, ji, 0, 0))
    row3 = lambda c: pl.BlockSpec((1, tm, c), lambda bi, ji: (bi, ji, 0))
    out_specs = [
        blk4(aw, tm), blk4(tm, aw),
        pl.BlockSpec((1, 1, aw), lambda bi, ji: (bi * nb + ji, 0, 0)),
        blk4(aw, tm), row3(mw), blk4(mw, tm), row3(mw), row3(mw), row3(nh2),
        pl.BlockSpec((1, nh2, tm), lambda bi, ji: (bi, 0, ji)),
    ]
    in_specs = [
        pl.BlockSpec((1, tm, d), lambda bi, ji: (bi, ji, 0)),
        pl.BlockSpec((1, HALO, d), lambda bi, ji: (bi, jnp.maximum(ji * (tm // HALO) - 1, 0), 0)),
        row3(LANES), row3(LANES),
        _const_spec(g.shape), _const_spec(w.shape), _const_spec(cw.shape), _const_spec(bias.shape),
    ]
    return pl.pallas_call(
        _in_proj_kernel,
        grid=(b, nb),
        in_specs=in_specs,
        out_specs=out_specs,
        out_shape=out_shape,
        scratch_shapes=[pltpu.VMEM((2 * mw // LANES, HALO + tm, LANES), F32),
                        pltpu.VMEM((2 * mw // LANES, tm, LANES), F32)],
        compiler_params=pltpu.CompilerParams(
            dimension_semantics=("parallel", "parallel"), vmem_limit_bytes=VMEM_LIMIT),
        name="in_proj",
    )(x, x, cos, sin, g, w, cw, bias)


def _moba_kernel(jtab_ref, ntab_ref, qT_ref, k_ref, vT_ref, kmean_ref, out_ref,
                 bias_ref, m_ref, acc_ref, s_ref, cm_ref, p_ref, a_ref):
    blk = MOBA_BLOCK
    hd = ATTN_HEAD_DIM
    nb = k_ref.shape[1]
    nh = 2 * (qT_ref.shape[2] // LANES)
    n_units = nb * (nb - 1) // 2
    dim_row = lax.broadcasted_iota(jnp.int32, (LANES, blk), 0)
    dim_col = lax.broadcasted_iota(jnp.int32, (blk, LANES), 1)
    blk_id = lax.broadcasted_iota(jnp.int32, (nb, blk), 0)
    blk_idf = blk_id.astype(F32)
    key_row = lax.broadcasted_iota(jnp.int32, (blk, blk), 0)
    qry_col = lax.broadcasted_iota(jnp.int32, (blk, blk), 1)

    def lanes(h):
        return slice((h // 2) * LANES, (h // 2 + 1) * LANES)

    ones_rows = jnp.ones((SUM_ROWS, blk), BF16)

    def v_ext(n, h):
        return jnp.concatenate([vT_ref[0, n, h * hd:(h + 1) * hd, :], ones_rows], axis=0)

    def phase_a(j, carry):
        for h in range(nh):
            qT = qT_ref[0, j, lanes(h), :]
            qh = jnp.where((dim_row >= hd) == (h % 2 == 1), qT, jnp.zeros_like(qT))
            km = kmean_ref[0, :, lanes(h)]
            km_hi = km.astype(BF16)
            km_lo = (km - km_hi.astype(F32)).astype(BF16)
            gate = _dot(km_hi, qh) + _dot(km_lo, qh)
            gate = jnp.where(blk_id < j, gate, NEG)
            picked = blk_id < 0
            for _ in range(min(MOBA_TOPK, nb)):
                top = jnp.max(gate, axis=0, keepdims=True)
                first = jnp.min(jnp.where(gate == top, blk_idf, float(nb)), axis=0, keepdims=True)
                hit = blk_idf == first
                picked = picked | hit
                gate = jnp.where(hit, -jnp.inf, gate)
            selected = picked & (blk_id < j)
            bias_ref[h, j, 0:nb, :] = jnp.where(selected, 0.0, NEG)
            bias_ref[h, j, nb:nb + 1, :] = jnp.full((1, blk), NEG, F32)
            s_own = jnp.where(key_row <= qry_col, _dot(k_ref[0, j, :, lanes(h)], qh), NEG)
            m0 = jnp.max(s_own, axis=0, keepdims=True)
            p0 = jnp.exp2(s_own - m0)
            m_ref[h, j] = m0
            acc_ref[h, j, 0:hd, :] = _dot(vT_ref[0, j, h * hd:(h + 1) * hd, :], p0.astype(BF16))
            acc_ref[h, j, hd:hd + SUM_ROWS, :] = jnp.broadcast_to(jnp.sum(p0, axis=0, keepdims=True), (SUM_ROWS, blk))
        return carry

    lax.fori_loop(0, nb, phase_a, 0)

    for h in range(nh):
        s_ref[h] = jnp.zeros((blk, blk), F32)
        cm_ref[h] = jnp.zeros((1, blk), F32)
        p_ref[h] = jnp.zeros((blk, blk), BF16)
        a_ref[h] = jnp.ones((1, blk), F32)

    def phase_b(i, carry):
        j1, n1 = jtab_ref[i + 2], ntab_ref[i + 2]
        j2, n2 = jtab_ref[i + 1], ntab_ref[i + 1]
        j3, n3 = jtab_ref[i], ntab_ref[i]
        k_blk = jnp.minimum(n1, nb - 1)
        v_blk = jnp.minimum(n3, nb - 1)
        for h in range(nh):
            acc_ref[h, j3] = a_ref[h] * acc_ref[h, j3] + _dot(v_ext(v_blk, h), p_ref[h])
        for h in range(nh):
            bias = bias_ref[h, j2, pl.ds(n2, 1), :]
            m_old = m_ref[h, j2]
            m_new = jnp.maximum(m_old, cm_ref[h] + bias)
            alpha = jnp.exp2(m_old - m_new)
            p = jnp.exp2(s_ref[h] - (m_new - bias))
            m_ref[h, j2] = m_new
            a_ref[h] = alpha
            p_ref[h] = p.astype(BF16)
        for h in range(nh):
            kb = k_ref[0, k_blk, :, lanes(h)]
            kb = jnp.where((dim_col >= hd) == (h % 2 == 1), kb, jnp.zeros_like(kb))
            sT = _dot(kb, qT_ref[0, j1, lanes(h), :])
            s_ref[h] = sT
            cm_ref[h] = jnp.max(sT, axis=0, keepdims=True)
        return carry

    lax.fori_loop(0, n_units + 2, phase_b, 0)

    def phase_c(j, carry):
        row0 = pl.multiple_of(j * blk, blk)
        for pair in range(nh // 2):
            oT = jnp.concatenate([acc_ref[h, j, 0:hd, :] / acc_ref[h, j, hd:hd + 1, :]
                                  for h in (2 * pair, 2 * pair + 1)], axis=0)
            out_ref[0, pl.ds(row0, blk), pair * LANES:(pair + 1) * LANES] = oT.T.astype(BF16)
        return carry

    lax.fori_loop(0, nb, phase_c, 0)


def _moba_units(nb):
    pad = (0, nb)
    units = [pad, pad] + [(j, n) for j in range(nb) for n in range(j)] + [pad, pad]
    jtab = jnp.asarray([u[0] for u in units], jnp.int32)
    ntab = jnp.asarray([u[1] for u in units], jnp.int32)
    return jtab, ntab


def _moba(qT, k, vT, kmean):
    b, nb, aw, blk = qT.shape
    s = nb * blk
    w = MOBA_PAIRS_PER_STEP * LANES
    nh = 2 * MOBA_PAIRS_PER_STEP
    jtab, ntab = _moba_units(nb)
    grid_spec = pltpu.PrefetchScalarGridSpec(
        num_scalar_prefetch=2,
        grid=(b, aw // w),
        in_specs=[
            pl.BlockSpec((1, nb, w, blk), lambda bi, pi, *_: (bi, 0, pi, 0)),
            pl.BlockSpec((1, nb, blk, w), lambda bi, pi, *_: (bi, 0, 0, pi)),
            pl.BlockSpec((1, nb, w, blk), lambda bi, pi, *_: (bi, 0, pi, 0)),
            pl.BlockSpec((1, nb, w), lambda bi, pi, *_: (bi, 0, pi)),
        ],
        out_specs=pl.BlockSpec((1, s, w), lambda bi, pi, *_: (bi, 0, pi)),
        scratch_shapes=[pltpu.VMEM((nh, nb, nb + 8, blk), F32),
                        pltpu.VMEM((nh, nb, 1, blk), F32),
                        pltpu.VMEM((nh, nb, ATTN_HEAD_DIM + SUM_ROWS, blk), F32),
                        pltpu.VMEM((nh, blk, blk), F32),
                        pltpu.VMEM((nh, 1, blk), F32),
                        pltpu.VMEM((nh, blk, blk), BF16),
                        pltpu.VMEM((nh, 1, blk), F32)],
    )
    return pl.pallas_call(
        _moba_kernel,
        grid_spec=grid_spec,
        out_shape=jax.ShapeDtypeStruct((b, s, aw), BF16),
        compiler_params=pltpu.CompilerParams(
            dimension_semantics=("parallel", "parallel"), vmem_limit_bytes=VMEM_LIMIT),
        name="moba",
    )(jtab, ntab, qT, k, vT, kmean)


def _split3(x):
    hi = x.astype(BF16)
    r = x - hi.astype(F32)
    mid = r.astype(BF16)
    lo = (r - mid.astype(F32)).astype(BF16)
    return hi, mid, lo


def _mlstm_kernel(q_ref, kT_ref, v_ref, o_ref, gif_ref, gifT_ref, g_ref, out_ref, c_ref, m_ref):
    L = MLSTM_CHUNK
    nh, hd = N_MLSTM_HEADS, MLSTM_HEAD_DIM
    c = pl.program_id(1)

    @pl.when(c == 0)
    def _():
        c_ref[...] = jnp.zeros_like(c_ref)
        m_ref[...] = jnp.zeros_like(m_ref)

    t_idx = lax.broadcasted_iota(jnp.int32, (L, L), 0)
    s_idx = lax.broadcasted_iota(jnp.int32, (L, L), 1)
    causal = s_idx <= t_idx
    tril = jnp.where(causal, 1.0, 0.0).astype(BF16)
    triu = jnp.where(t_idx <= s_idx, 1.0, 0.0).astype(BF16)

    ones_col = jnp.where(lax.broadcasted_iota(jnp.int32, (L, hd), 1) == 0, 1.0, 0.0).astype(BF16)
    gn = g_ref[...]

    for r, r0 in [(r, r0) for r in range(q_ref.shape[0]) for r0 in range(0, q_ref.shape[1], L)]:
        rows = slice(r0, r0 + L)
        gi_col = gif_ref[r, rows, :]
        gi_row = gifT_ref[r, :, rows]
        b_col_all = sum(_dot(tril, part) for part in _split3(gi_col))
        b_row_all = sum(_dot(part, triu) for part in _split3(gi_row))

        for h in range(nh):
            cols = slice(h * hd, (h + 1) * hd)
            a_row = gi_row[h:h + 1, :]
            b_row = b_row_all[nh + h:nh + h + 1, :]
            b_col = b_col_all[:, nh + h:nh + h + 1]
            q = q_ref[r, rows, cols]
            if L >= SEQ_TILE:
                kT = jnp.concatenate([kT_ref[r, (r0 + i) // SEQ_TILE, cols, :] for i in range(0, L, SEQ_TILE)], axis=1)
            else:
                kT = kT_ref[r, r0 // SEQ_TILE, cols, r0 % SEQ_TILE:r0 % SEQ_TILE + L]
            v_ext = jnp.concatenate([v_ref[r, rows, cols], ones_col], axis=1)

            dmat = jnp.where(causal, b_col - b_row + a_row, NEG)
            mi = jnp.max(dmat, axis=1, keepdims=True)
            intra = _dot((jnp.exp(dmat - mi) * _dot(q, kT)).astype(BF16), v_ext)
            b_last = b_row[:, L - 1:L]
            dec = b_last - b_row + a_row
            md = jnp.max(dec, axis=1, keepdims=True)
            kv = _dot((kT.astype(F32) * jnp.exp(dec - md)).astype(BF16), v_ext)

            m_prev = m_ref[r * nh + h]
            c_old = c_ref[r * nh + h]
            inter = b_col + m_prev
            m_t = jnp.maximum(inter, mi)
            num_ext = jnp.exp(inter - m_t) * _dot(q, c_old.astype(BF16)) + jnp.exp(mi - m_t) * intra
            num = num_ext[:, :hd]
            den = num_ext[:, hd:hd + 1]
            h_out = num / jnp.maximum(jnp.abs(den), jnp.exp(-m_t))

            m_new = jnp.maximum(b_last + m_prev, md)
            c_ref[r * nh + h] = jnp.exp(b_last + m_prev - m_new) * c_old + jnp.exp(md - m_new) * kv
            m_ref[r * nh + h] = m_new

            hn = _rms(h_out, gn[:, cols])
            gate_o = _sigmoid(o_ref[r, rows, cols])
            out_ref[r, rows, cols] = (hn * gate_o).astype(BF16)


def _mlstm(qm, kmT, vm, om, gif, gifT, g):
    b, s, mw = qm.shape
    L = MLSTM_BLOCK
    nc = s // L
    nh2 = 2 * N_MLSTM_HEADS
    rows = MLSTM_ROWS if b % MLSTM_ROWS == 0 else 1
    row = lambda cdim: pl.BlockSpec((rows, L, cdim), lambda bi, ci: (bi, ci, 0))
    return pl.pallas_call(
        _mlstm_kernel,
        grid=(b // rows, nc),
        in_specs=[
            row(mw),
            pl.BlockSpec((rows, L // SEQ_TILE, mw, SEQ_TILE), lambda bi, ci: (bi, ci, 0, 0)),
            row(mw), row(mw), row(nh2),
            pl.BlockSpec((rows, nh2, L), lambda bi, ci: (bi, 0, ci)),
            _const_spec(g.shape),
        ],
        out_specs=row(mw),
        out_shape=jax.ShapeDtypeStruct((b, s, mw), BF16),
        scratch_shapes=[pltpu.VMEM((rows * N_MLSTM_HEADS, MLSTM_HEAD_DIM, 2 * MLSTM_HEAD_DIM), F32),
                        pltpu.VMEM((rows * N_MLSTM_HEADS, 1, 1), F32)],
        compiler_params=pltpu.CompilerParams(
            dimension_semantics=("parallel", "arbitrary"), vmem_limit_bytes=VMEM_LIMIT),
        name="mlstm",
    )(qm, kmT, vm, om, gif, gifT, g)


def _mix_kernel(x_ref, a_ref, y_ref, g_ref, wg_ref, wa_ref, wm_ref, wo_ref, out_ref):
    d = x_ref.shape[-1]
    x = x_ref[...]
    h = _rms(x, g_ref[...]).astype(BF16)
    gate_a = _sigmoid(_dot(h, wg_ref[:, :d]))
    gate_m = _sigmoid(_dot(h, wg_ref[:, d:]))
    merged = gate_a * _dot(a_ref[...], wa_ref[...]) + gate_m * _dot(y_ref[...], wm_ref[...])
    out_ref[...] = x + _dot(merged.astype(BF16), wo_ref[...])


def _mix(x2d, a2d, y2d, g, wg, wa, wm, wo):
    t, d = x2d.shape
    tm = MIX_TILE
    row = lambda cdim: pl.BlockSpec((tm, cdim), lambda i: (i, 0))
    return pl.pallas_call(
        _mix_kernel,
        grid=(t // tm,),
        in_specs=[row(d), row(a2d.shape[1]), row(y2d.shape[1]), _const_spec(g.shape),
                  _const_spec(wg.shape), _const_spec(wa.shape), _const_spec(wm.shape), _const_spec(wo.shape)],
        out_specs=row(d),
        out_shape=jax.ShapeDtypeStruct((t, d), F32),
        compiler_params=pltpu.CompilerParams(
            dimension_semantics=("parallel",), vmem_limit_bytes=VMEM_LIMIT),
        name="mix",
    )(x2d, a2d, y2d, g, wg, wa, wm, wo)


def _ffn_kernel(x_ref, xh_ref, g_ref, wup_ref, cg_ref, cu_ref, wd_ref, gf_ref, out_ref,
                hext_ref, ug_ref, uu_ref, act_ref, actall_ref):
    tm = FFN_TILE
    nslab = ug_ref.shape[0]
    nchunk = actall_ref.shape[1] // FFN_CHUNK
    j = pl.program_id(1)
    x = x_ref[0]
    g = g_ref[...]
    hext_ref[HALO:HALO + tm, :] = _rms(x, g).astype(BF16)
    hh = _rms(xh_ref[0], g)
    hext_ref[0:HALO, :] = jnp.where(j == 0, 0.0, hh).astype(BF16)

    def stage_a(c):
        gate_cols = pl.ds(pl.multiple_of(c * FFN_CHUNK, LANES), FFN_CHUNK)
        up_cols = pl.ds(pl.multiple_of(nchunk * FFN_CHUNK + c * FFN_CHUNK, LANES), FFN_CHUNK)
        ug = _dot(hext_ref[...], wup_ref[:, gate_cols])
        uu = _dot(hext_ref[...], wup_ref[:, up_cols])
        for sl in range(nslab):
            ug_ref[sl] = ug[:, sl * LANES:(sl + 1) * LANES]
            uu_ref[sl] = uu[:, sl * LANES:(sl + 1) * LANES]

    def stage_b(c):
        for sl in range(nslab):
            cg = cg_ref[c][:, sl * LANES:(sl + 1) * LANES]
            cu = cu_ref[c][:, sl * LANES:(sl + 1) * LANES]
            for r0 in range(0, tm, 8 * CONV_STRIDE):
                gates = _conv_block(ug_ref, sl, r0, cg)
                ups = _conv_block(uu_ref, sl, r0, cu)
                for k in range(CONV_STRIDE):
                    act_ref[sl, pl.ds(r0 + k, 8, stride=CONV_STRIDE), :] = gates[k] * _sigmoid(gates[k]) * ups[k]
        act = jnp.concatenate([act_ref[sl] for sl in range(nslab)], axis=1)
        actall_ref[:, pl.ds(pl.multiple_of(c * FFN_CHUNK, LANES), FFN_CHUNK)] = act.astype(BF16)

    stage_a(0)

    def body(c, carry):
        stage_b(c - 1)
        stage_a(c)
        return carry

    lax.fori_loop(1, nchunk, body, 0)
    stage_b(nchunk - 1)
    out_ref[0] = _rms(x + _dot(actall_ref[...], wd_ref[...]), gf_ref[...])


def _ffn(x1, g2, wup, cg, cu, wd, gf):
    b, s, d = x1.shape
    tm = FFN_TILE
    return pl.pallas_call(
        _ffn_kernel,
        grid=(b, s // tm),
        in_specs=[
            pl.BlockSpec((1, tm, d), lambda bi, ji: (bi, ji, 0)),
            pl.BlockSpec((1, HALO, d), lambda bi, ji: (bi, jnp.maximum(ji * (tm // HALO) - 1, 0), 0)),
            _const_spec(g2.shape), _const_spec(wup.shape),
            _const_spec(cg.shape), _const_spec(cu.shape), _const_spec(wd.shape), _const_spec(gf.shape),
        ],
        out_specs=pl.BlockSpec((1, tm, d), lambda bi, ji: (bi, ji, 0)),
        out_shape=jax.ShapeDtypeStruct((b, s, d), F32),
        scratch_shapes=[pltpu.VMEM((HALO + tm, d), BF16),
                        pltpu.VMEM((FFN_CHUNK // LANES, HALO + tm, LANES), F32),
                        pltpu.VMEM((FFN_CHUNK // LANES, HALO + tm, LANES), F32),
                        pltpu.VMEM((FFN_CHUNK // LANES, tm, LANES), F32),
                        pltpu.VMEM((tm, wd.shape[0]), BF16)],
        compiler_params=pltpu.CompilerParams(
            dimension_semantics=("parallel", "parallel"), vmem_limit_bytes=VMEM_LIMIT),
        name="ffn",
    )(x1, x1, g2, wup, cg, cu, wd, gf)


def _split_w_in_kernel(w_ref, w1_ref, wg_ref):
    n1 = w1_ref.shape[1]
    n_used = w_ref.shape[2] - wg_ref.shape[1]
    head = w_ref[0, :, 0:n1]
    lane = lax.broadcasted_iota(jnp.int32, head.shape, 1)
    w1_ref[...] = jnp.where(lane < n_used, head, 0.0).astype(BF16)
    wg_ref[...] = w_ref[0, :, n_used:].astype(BF16)


def _split_w_in(w_layer, n_used, n_gates):
    d = w_layer.shape[1]
    n1 = -(-n_used // LANES) * LANES
    rows = 128
    return pl.pallas_call(
        _split_w_in_kernel,
        grid=(d // rows,),
        in_specs=[pl.BlockSpec((1, rows, n_used + n_gates), lambda i: (0, i, 0))],
        out_specs=[pl.BlockSpec((rows, n1), lambda i: (i, 0)), pl.BlockSpec((rows, n_gates), lambda i: (i, 0))],
        out_shape=[jax.ShapeDtypeStruct((d, n1), BF16), jax.ShapeDtypeStruct((d, n_gates), BF16)],
        compiler_params=pltpu.CompilerParams(dimension_semantics=("parallel",), vmem_limit_bytes=VMEM_LIMIT),
        name="split_w_in",
    )(w_layer)


def _chunk_cols(w, chunk):
    kdim, n = w.shape
    return w.reshape(kdim, n // chunk, chunk).transpose(1, 0, 2)


def kernel(x, positions, norm_mix_g, w_in, conv_mlstm, i_bias, f_bias, mlstm_norm_g, w_branch_attn,
           w_branch_mlstm, w_out, norm_ffn_g, w_up, conv_ffn, w_down, norm_final_g):
    b, s, d = x.shape
    aw, mw, nh = ATTN_WIDTH, MLSTM_WIDTH, N_MLSTM_HEADS
    assert s % FFN_TILE == 0 and s % SEQ_TILE == 0 and s % MLSTM_BLOCK == 0 and (b * s) % MIX_TILE == 0 and d % LANES == 0
    assert w_in.shape[0] == 1, "the ffn kernel fuses the output norm, so exactly one layer is supported"
    n_qkv = 3 * aw + 4 * mw
    d_ff = w_down.shape[1]
    assert d_ff % FFN_CHUNK == 0
    half = ATTN_HEAD_DIM // 2
    inv_freq = ROPE_THETA ** (-(jnp.arange(half, dtype=F32) / half))
    cos, sin = _rope_tables(positions, inv_freq)

    for layer in range(w_in.shape[0]):
        w1, wg = _split_w_in(w_in[layer:layer + 1], n_qkv + 2 * nh, 2 * d)
        bias = jnp.concatenate(
            [i_bias[layer], f_bias[layer], jnp.zeros((LANES - 2 * nh,), F32)]).reshape(1, LANES)

        qT, k, kmean, vT, qm, kmT, vm, om, gif, gifT = _in_proj(
            x, cos, sin, norm_mix_g[layer].reshape(1, d), w1, conv_mlstm[layer], bias)
        attn = _moba(qT, k, vT, kmean.reshape(b, s // MOBA_BLOCK, aw))
        y_m = _mlstm(qm, kmT, vm, om, gif, gifT, mlstm_norm_g[layer].reshape(1, mw))
        x1 = _mix(x.reshape(b * s, d), attn.reshape(b * s, aw), y_m.reshape(b * s, mw),
                  norm_mix_g[layer].reshape(1, d), wg, w_branch_attn[layer].astype(BF16),
                  w_branch_mlstm[layer].astype(BF16), w_out[layer].astype(BF16)).reshape(b, s, d)

        wu = w_up[layer].astype(BF16)
        cf = conv_ffn[layer]
        x = _ffn(x1, norm_ffn_g[layer].reshape(1, d), wu,
                 _chunk_cols(cf[:, :d_ff], FFN_CHUNK), _chunk_cols(cf[:, d_ff:], FFN_CHUNK),
                 w_down[layer].astype(BF16), norm_final_g.reshape(1, d))
    return x
```

```python
import math

import jax
import jax.numpy as jnp
from jax import lax
from jax.experimental import pallas as pl
from jax.experimental.pallas import tpu as pltpu

F32 = jnp.float32
BF16 = jnp.bfloat16

EPS = 1e-6
NEG = -1e30
ROPE_THETA = 10000.0

N_ATTN_HEADS = 8
ATTN_HEAD_DIM = 64
ATTN_WIDTH = N_ATTN_HEADS * ATTN_HEAD_DIM
MOBA_BLOCK = 256
MOBA_TOPK = 3
N_MLSTM_HEADS = 4
MLSTM_HEAD_DIM = 128
MLSTM_WIDTH = N_MLSTM_HEADS * MLSTM_HEAD_DIM

LANES = 128
HALO = 16
SEQ_TILE = MOBA_BLOCK
MIX_TILE = 1024
FFN_TILE = 1024
FFN_CHUNK = 256
CONV_STRIDE = 4
SUM_ROWS = 16
MLSTM_BLOCK = 256
MLSTM_CHUNK = 128
MLSTM_ROWS = 1
MOBA_PAIRS_PER_STEP = 4
LOG2_E = math.log2(math.e)
VMEM_LIMIT = 56 * 1024 * 1024


def _dot(a, b):
    return jnp.dot(a, b, preferred_element_type=F32)


def _rms(x, g):
    ms = jnp.mean(x * x, axis=-1, keepdims=True)
    return x * lax.rsqrt(ms + EPS) * g


def _sigmoid(x):
    return 1.0 / (1.0 + jnp.exp(-x))


def _conv_block(u_ref, sl, r0, cw):
    ntap = cw.shape[0]
    rows = {k: u_ref[sl, pl.ds(HALO + r0 + k, 8, stride=CONV_STRIDE), :] for k in range(1 - ntap, CONV_STRIDE)}
    outs = []
    for k in range(CONV_STRIDE):
        out = cw[0:1] * rows[k - ntap + 1]
        for j in range(1, ntap):
            out = out + cw[j:j + 1] * rows[k - ntap + 1 + j]
        outs.append(out)
    return outs


def _const_spec(shape):
    nd = len(shape)
    return pl.BlockSpec(shape, lambda *_: (0,) * nd, pipeline_mode=pl.Buffered(1))


def _rope_table_kernel(pos_ref, invf_ref, cos_ref, sin_ref):
    rows, per_row = pos_ref.shape
    half = LANES // per_row
    lane = lax.broadcasted_iota(jnp.int32, (rows, LANES), 1)
    pos = pos_ref[...]
    pos_rep = pos[:, per_row - 1:per_row]
    for g in range(per_row - 2, -1, -1):
        pos_rep = jnp.where(lane < (g + 1) * half, pos[:, g:g + 1], pos_rep)
    ang = pos_rep.astype(F32) * invf_ref[...]
    for table, out_ref in ((jnp.cos(ang), cos_ref), (jnp.sin(ang), sin_ref)):
        for g in range(per_row):
            base = table if g == 0 else pltpu.roll(table, LANES - g * half, 1)
            width = half
            while width < LANES:
                base = jnp.where(lane < width, base, pltpu.roll(base, width, 1))
                width *= 2
            out_ref[pl.ds(g, rows, stride=per_row), :] = base


def _rope_tables(positions, inv_freq):
    b, s = positions.shape
    half = inv_freq.shape[0]
    per_row = LANES // half
    rows = b * s // per_row
    tile = min(rows, 512)
    invf = jnp.tile(inv_freq, per_row).reshape(1, LANES)
    out_spec = pl.BlockSpec((tile * per_row, LANES), lambda i: (i, 0))
    cos, sin = pl.pallas_call(
        _rope_table_kernel,
        grid=(rows // tile,),
        in_specs=[pl.BlockSpec((tile, per_row), lambda i: (i, 0)), _const_spec(invf.shape)],
        out_specs=[out_spec, out_spec],
        out_shape=[jax.ShapeDtypeStruct((b * s, LANES), F32)] * 2,
        compiler_params=pltpu.CompilerParams(dimension_semantics=("parallel",)),
        name="rope_tables",
    )(positions.reshape(rows, per_row), invf)
    return cos.reshape(b, s, LANES), sin.reshape(b, s, LANES)


def _in_proj_kernel(x_ref, xh_ref, cos_ref, sin_ref, g_ref, w_ref, cw_ref, bias_ref,
                    qT_ref, k_ref, kmean_ref, vT_ref, qm_ref, kmT_ref, vm_ref, om_ref, gif_ref, gifT_ref,
                    ext_ref, act_ref):
    tm = SEQ_TILE
    aw, mw = ATTN_WIDTH, MLSTM_WIDTH
    j = pl.program_id(1)
    g = g_ref[...]
    h = _rms(x_ref[0], g).astype(BF16)

    cos = cos_ref[0]
    sin = sin_ref[0]
    lane = lax.broadcasted_iota(jnp.int32, (tm, LANES), 1)
    half = ATTN_HEAD_DIM // 2
    first_half = (lane % ATTN_HEAD_DIM) < half
    sin_signed = jnp.where(first_half, -sin, sin)

    def rope(t):
        outs = []
        for c in range(aw // LANES):
            tg = t[:, c * LANES:(c + 1) * LANES]
            swapped = jnp.where(first_half, pltpu.roll(tg, LANES - half, 1), pltpu.roll(tg, half, 1))
            outs.append(tg * cos + swapped * sin_signed)
        return jnp.concatenate(outs, axis=1)

    qk = _dot(h, w_ref[:, 0:2 * aw])
    q = rope(qk[:, :aw]) * (LOG2_E / math.sqrt(ATTN_HEAD_DIM))
    k = rope(qk[:, aw:])
    qT_ref[0, 0] = q.T.astype(BF16)
    k_ref[0, 0] = k.astype(BF16)
    kmean_ref[0] = jnp.mean(k, axis=0, keepdims=True)
    v = _dot(h, w_ref[:, 2 * aw:3 * aw])
    vT_ref[0, 0] = v.T.astype(BF16)

    c0 = 3 * aw
    pm = _dot(h, w_ref[:, c0:c0 + 2 * mw])
    hh = _rms(xh_ref[0], g).astype(BF16)
    ph = jnp.where(j == 0, 0.0, _dot(hh, w_ref[:, c0:c0 + 2 * mw]))
    cw = cw_ref[...]
    nslab = 2 * mw // LANES
    for sl in range(nslab):
        cols = slice(sl * LANES, (sl + 1) * LANES)
        ext_ref[sl, 0:HALO, :] = ph[:, cols]
        ext_ref[sl, HALO:HALO + tm, :] = pm[:, cols]
    for sl in range(nslab):
        cws = cw[:, sl * LANES:(sl + 1) * LANES]
        for r0 in range(0, tm, 8 * CONV_STRIDE):
            for kk, conv in enumerate(_conv_block(ext_ref, sl, r0, cws)):
                act_ref[sl, pl.ds(r0 + kk, 8, stride=CONV_STRIDE), :] = conv * _sigmoid(conv)
    q_act = jnp.concatenate([act_ref[sl] for sl in range(nslab // 2)], axis=1)
    k_act = jnp.concatenate([act_ref[sl] for sl in range(nslab // 2, nslab)], axis=1)
    qm_ref[0] = (q_act * (1.0 / math.sqrt(MLSTM_HEAD_DIM))).astype(BF16)
    kmT_ref[0, 0] = k_act.T.astype(BF16)

    c1 = c0 + 2 * mw
    vo = _dot(h, w_ref[:, c1:c1 + 2 * mw])
    vm_ref[0] = vo[:, :mw].astype(BF16)
    om_ref[0] = vo[:, mw:]

    c2 = c1 + 2 * mw
    gi = _dot(h, w_ref[:, c2:c2 + LANES]) + bias_ref[...]
    log_sig = jnp.minimum(gi, 0.0) - jnp.log1p(jnp.exp(-jnp.abs(gi)))
    gt = jnp.where(lane < N_MLSTM_HEADS, gi, log_sig)
    gif_ref[0] = gt[:, 0:2 * N_MLSTM_HEADS]
    gifT_ref[0] = gt.T[0:2 * N_MLSTM_HEADS, :]


def _in_proj(x, cos, sin, g, w, cw, bias):
    b, s, d = x.shape
    tm = SEQ_TILE
    nb = s // tm
    aw, mw = ATTN_WIDTH, MLSTM_WIDTH
    nh2 = 2 * N_MLSTM_HEADS
    out_shape = [
        jax.ShapeDtypeStruct((b, nb, aw, tm), BF16),
        jax.ShapeDtypeStruct((b, nb, tm, aw), BF16),
        jax.ShapeDtypeStruct((b * nb, 1, aw), F32),
        jax.ShapeDtypeStruct((b, nb, aw, tm), BF16),
        jax.ShapeDtypeStruct((b, s, mw), BF16),
        jax.ShapeDtypeStruct((b, nb, mw, tm), BF16),
        jax.ShapeDtypeStruct((b, s, mw), BF16),
        jax.ShapeDtypeStruct((b, s, mw), F32),
        jax.ShapeDtypeStruct((b, s, nh2), F32),
        jax.ShapeDtypeStruct((b, nh2, s), F32),
    ]
    blk4 = lambda r, c: pl.BlockSpec((1, 1, r, c), lambda bi, ji: (bi, ji, 0, 0))
    row3 = lambda c: pl.BlockSpec((1, tm, c), lambda bi, ji: (bi, ji, 0))
    out_specs = [
        blk4(aw, tm), blk4(tm, aw),
        pl.BlockSpec((1, 1, aw), lambda bi, ji: (bi * nb + ji, 0, 0)),
        blk4(aw, tm), row3(mw), blk4(mw, tm), row3(mw), row3(mw), row3(nh2),
        pl.BlockSpec((1, nh2, tm), lambda bi, ji: (bi, 0, ji)),
    ]
    in_specs = [
        pl.BlockSpec((1, tm, d), lambda bi, ji: (bi, ji, 0)),
        pl.BlockSpec((1, HALO, d), lambda bi, ji: (bi, jnp.maximum(ji * (tm // HALO) - 1, 0), 0)),
        row3(LANES), row3(LANES),
        _const_spec(g.shape), _const_spec(w.shape), _const_spec(cw.shape), _const_spec(bias.shape),
    ]
    return pl.pallas_call(
        _in_proj_kernel,
        grid=(b, nb),
        in_specs=in_specs,
        out_specs=out_specs,
        out_shape=out_shape,
        scratch_shapes=[pltpu.VMEM((2 * mw // LANES, HALO + tm, LANES), F32),
                        pltpu.VMEM((2 * mw // LANES, tm, LANES), F32)],
        compiler_params=pltpu.CompilerParams(
            dimension_semantics=("parallel", "parallel"), vmem_limit_bytes=VMEM_LIMIT),
        name="in_proj",
    )(x, x, cos, sin, g, w, cw, bias)


def _moba_kernel(jtab_ref, ntab_ref, qT_ref, k_ref, vT_ref, kmean_ref, out_ref,
                 bias_ref, m_ref, acc_ref, s_ref, cm_ref, p_ref, a_ref):
    blk = MOBA_BLOCK
    hd = ATTN_HEAD_DIM
    nb = k_ref.shape[1]
    nh = 2 * (qT_ref.shape[2] // LANES)
    n_units = nb * (nb - 1) // 2
    dim_row = lax.broadcasted_iota(jnp.int32, (LANES, blk), 0)
    dim_col = lax.broadcasted_iota(jnp.int32, (blk, LANES), 1)
    blk_id = lax.broadcasted_iota(jnp.int32, (nb, blk), 0)
    blk_idf = blk_id.astype(F32)
    key_row = lax.broadcasted_iota(jnp.int32, (blk, blk), 0)
    qry_col = lax.broadcasted_iota(jnp.int32, (blk, blk), 1)

    def lanes(h):
        return slice((h // 2) * LANES, (h // 2 + 1) * LANES)

    ones_rows = jnp.ones((SUM_ROWS, blk), BF16)

    def v_ext(n, h):
        return jnp.concatenate([vT_ref[0, n, h * hd:(h + 1) * hd, :], ones_rows], axis=0)

    def phase_a(j, carry):
        for h in range(nh):
            qT = qT_ref[0, j, lanes(h), :]
            qh = jnp.where((dim_row >= hd) == (h % 2 == 1), qT, jnp.zeros_like(qT))
            km = kmean_ref[0, :, lanes(h)]
            km_hi = km.astype(BF16)
            km_lo = (km - km_hi.astype(F32)).astype(BF16)
            gate = _dot(km_hi, qh) + _dot(km_lo, qh)
            gate = jnp.where(blk_id < j, gate, NEG)
            picked = blk_id < 0
            for _ in range(min(MOBA_TOPK, nb)):
                top = jnp.max(gate, axis=0, keepdims=True)
                first = jnp.min(jnp.where(gate == top, blk_idf, float(nb)), axis=0, keepdims=True)
                hit = blk_idf == first
                picked = picked | hit
                gate = jnp.where(hit, -jnp.inf, gate)
            selected = picked & (blk_id < j)
            bias_ref[h, j, 0:nb, :] = jnp.where(selected, 0.0, NEG)
            bias_ref[h, j, nb:nb + 1, :] = jnp.full((1, blk), NEG, F32)
            s_own = jnp.where(key_row <= qry_col, _dot(k_ref[0, j, :, lanes(h)], qh), NEG)
            m0 = jnp.max(s_own, axis=0, keepdims=True)
            p0 = jnp.exp2(s_own - m0)
            m_ref[h, j] = m0
            acc_ref[h, j, 0:hd, :] = _dot(vT_ref[0, j, h * hd:(h + 1) * hd, :], p0.astype(BF16))
            acc_ref[h, j, hd:hd + SUM_ROWS, :] = jnp.broadcast_to(jnp.sum(p0, axis=0, keepdims=True), (SUM_ROWS, blk))
        return carry

    lax.fori_loop(0, nb, phase_a, 0, unroll=4)

    for h in range(nh):
        s_ref[h] = jnp.zeros((blk, blk), F32)
        cm_ref[h] = jnp.zeros((1, blk), F32)
        p_ref[h] = jnp.zeros((blk, blk), BF16)
        a_ref[h] = jnp.ones((1, blk), F32)

    def phase_b(i, carry):
        j1, n1 = jtab_ref[i + 2], ntab_ref[i + 2]
        j2, n2 = jtab_ref[i + 1], ntab_ref[i + 1]
        j3, n3 = jtab_ref[i], ntab_ref[i]
        k_blk = jnp.minimum(n1, nb - 1)
        v_blk = jnp.minimum(n3, nb - 1)
        for h in range(nh):
            acc_ref[h, j3] = a_ref[h] * acc_ref[h, j3] + _dot(v_ext(v_blk, h), p_ref[h])
        for h in range(nh):
            bias = bias_ref[h, j2, pl.ds(n2, 1), :]
            m_old = m_ref[h, j2]
            m_new = jnp.maximum(m_old, cm_ref[h] + bias)
            alpha = jnp.exp2(m_old - m_new)
            p = jnp.exp2(s_ref[h] - (m_new - bias))
            m_ref[h, j2] = m_new
            a_ref[h] = alpha
            p_ref[h] = p.astype(BF16)
        for h in range(nh):
            kb = k_ref[0, k_blk, :, lanes(h)]
            kb = jnp.where((dim_col >= hd) == (h % 2 == 1), kb, jnp.zeros_like(kb))
            sT = _dot(kb, qT_ref[0, j1, lanes(h), :])
            s_ref[h] = sT
            cm_ref[h] = jnp.max(sT, axis=0, keepdims=True)
        return carry

    lax.fori_loop(0, n_units + 2, phase_b, 0)

    def phase_c(j, carry):
        row0 = pl.multiple_of(j * blk, blk)
        for pair in range(nh // 2):
            oT = jnp.concatenate([acc_ref[h, j, 0:hd, :] / acc_ref[h, j, hd:hd + 1, :]
                                  for h in (2 * pair, 2 * pair + 1)], axis=0)
            out_ref[0, pl.ds(row0, blk), pair * LANES:(pair + 1) * LANES] = oT.T.astype(BF16)
        return carry

    lax.fori_loop(0, nb, phase_c, 0)


def _moba_units(nb):
    pad = (0, nb)
    units = [pad, pad] + [(j, n) for j in range(nb) for n in range(j)] + [pad, pad]
    jtab = jnp.asarray([u[0] for u in units], jnp.int32)
    ntab = jnp.asarray([u[1] for u in units], jnp.int32)
    return jtab, ntab


def _moba(qT, k, vT, kmean):
    b, nb, aw, blk = qT.shape
    s = nb * blk
    w = MOBA_PAIRS_PER_STEP * LANES
    nh = 2 * MOBA_PAIRS_PER_STEP
    jtab, ntab = _moba_units(nb)
    grid_spec = pltpu.PrefetchScalarGridSpec(
        num_scalar_prefetch=2,
        grid=(b, aw // w),
        in_specs=[
            pl.BlockSpec((1, nb, w, blk), lambda bi, pi, *_: (bi, 0, pi, 0)),
            pl.BlockSpec((1, nb, blk, w), lambda bi, pi, *_: (bi, 0, 0, pi)),
            pl.BlockSpec((1, nb, w, blk), lambda bi, pi, *_: (bi, 0, pi, 0)),
            pl.BlockSpec((1, nb, w), lambda bi, pi, *_: (bi, 0, pi)),
        ],
        out_specs=pl.BlockSpec((1, s, w), lambda bi, pi, *_: (bi, 0, pi)),
        scratch_shapes=[pltpu.VMEM((nh, nb, nb + 8, blk), F32),
                        pltpu.VMEM((nh, nb, 1, blk), F32),
                        pltpu.VMEM((nh, nb, ATTN_HEAD_DIM + SUM_ROWS, blk), F32),
                        pltpu.VMEM((nh, blk, blk), F32),
                        pltpu.VMEM((nh, 1, blk), F32),
                        pltpu.VMEM((nh, blk, blk), BF16),
                        pltpu.VMEM((nh, 1, blk), F32)],
    )
    return pl.pallas_call(
        _moba_kernel,
        grid_spec=grid_spec,
        out_shape=jax.ShapeDtypeStruct((b, s, aw), BF16),
        compiler_params=pltpu.CompilerParams(
            dimension_semantics=("parallel", "parallel"), vmem_limit_bytes=VMEM_LIMIT),
        name="moba",
    )(jtab, ntab, qT, k, vT, kmean)


def _split3(x):
    hi = x.astype(BF16)
    r = x - hi.astype(F32)
    mid = r.astype(BF16)
    lo = (r - mid.astype(F32)).astype(BF16)
    return hi, mid, lo


def _mlstm_kernel(q_ref, kT_ref, v_ref, o_ref, gif_ref, gifT_ref, g_ref, out_ref, c_ref, m_ref):
    L = MLSTM_CHUNK
    nh, hd = N_MLSTM_HEADS, MLSTM_HEAD_DIM
    c = pl.program_id(1)

    @pl.when(c == 0)
    def _():
        c_ref[...] = jnp.zeros_like(c_ref)
        m_ref[...] = jnp.zeros_like(m_ref)

    t_idx = lax.broadcasted_iota(jnp.int32, (L, L), 0)
    s_idx = lax.broadcasted_iota(jnp.int32, (L, L), 1)
    causal = s_idx <= t_idx
    tril = jnp.where(causal, 1.0, 0.0).astype(BF16)
    triu = jnp.where(t_idx <= s_idx, 1.0, 0.0).astype(BF16)

    ones_col = jnp.where(lax.broadcasted_iota(jnp.int32, (L, hd), 1) == 0, 1.0, 0.0).astype(BF16)
    gn = g_ref[...]

    for r, r0 in [(r, r0) for r in range(q_ref.shape[0]) for r0 in range(0, q_ref.shape[1], L)]:
        rows = slice(r0, r0 + L)
        gi_col = gif_ref[r, rows, :]
        gi_row = gifT_ref[r, :, rows]
        b_col_all = sum(_dot(tril, part) for part in _split3(gi_col))
        b_row_all = sum(_dot(part, triu) for part in _split3(gi_row))

        for h in range(nh):
            cols = slice(h * hd, (h + 1) * hd)
            a_row = gi_row[h:h + 1, :]
            b_row = b_row_all[nh + h:nh + h + 1, :]
            b_col = b_col_all[:, nh + h:nh + h + 1]
            q = q_ref[r, rows, cols]
            if L >= SEQ_TILE:
                kT = jnp.concatenate([kT_ref[r, (r0 + i) // SEQ_TILE, cols, :] for i in range(0, L, SEQ_TILE)], axis=1)
            else:
                kT = kT_ref[r, r0 // SEQ_TILE, cols, r0 % SEQ_TILE:r0 % SEQ_TILE + L]
            v_ext = jnp.concatenate([v_ref[r, rows, cols], ones_col], axis=1)

            dmat = jnp.where(causal, b_col - b_row + a_row, NEG)
            mi = jnp.max(dmat, axis=1, keepdims=True)
            intra = _dot((jnp.exp(dmat - mi) * _dot(q, kT)).astype(BF16), v_ext)
            b_last = b_row[:, L - 1:L]
            dec = b_last - b_row + a_row
            md = jnp.max(dec, axis=1, keepdims=True)
            kv = _dot((kT.astype(F32) * jnp.exp(dec - md)).astype(BF16), v_ext)

            m_prev = m_ref[r * nh + h]
            c_old = c_ref[r * nh + h]
            inter = b_col + m_prev
            m_t = jnp.maximum(inter, mi)
            num_ext = jnp.exp(inter - m_t) * _dot(q, c_old.astype(BF16)) + jnp.exp(mi - m_t) * intra
            num = num_ext[:, :hd]
            den = num_ext[:, hd:hd + 1]
            h_out = num / jnp.maximum(jnp.abs(den), jnp.exp(-m_t))

            m_new = jnp.maximum(b_last + m_prev, md)
            c_ref[r * nh + h] = jnp.exp(b_last + m_prev - m_new) * c_old + jnp.exp(md - m_new) * kv
            m_ref[r * nh + h] = m_new

            hn = _rms(h_out, gn[:, cols])
            gate_o = _sigmoid(o_ref[r, rows, cols])
            out_ref[r, rows, cols] = (hn * gate_o).astype(BF16)


def _mlstm(qm, kmT, vm, om, gif, gifT, g):
    b, s, mw = qm.shape
    L = MLSTM_BLOCK
    nc = s // L
    nh2 = 2 * N_MLSTM_HEADS
    rows = MLSTM_ROWS if b % MLSTM_ROWS == 0 else 1
    row = lambda cdim: pl.BlockSpec((rows, L, cdim), lambda bi, ci: (bi, ci, 0))
    return pl.pallas_call(
        _mlstm_kernel,
        grid=(b // rows, nc),
        in_specs=[
            row(mw),
            pl.BlockSpec((rows, L // SEQ_TILE, mw, SEQ_TILE), lambda bi, ci: (bi, ci, 0, 0)),
            row(mw), row(mw), row(nh2),
            pl.BlockSpec((rows, nh2, L), lambda bi, ci: (bi, 0, ci)),
            _const_spec(g.shape),
        ],
        out_specs=row(mw),
        out_shape=jax.ShapeDtypeStruct((b, s, mw), BF16),
        scratch_shapes=[pltpu.VMEM((rows * N_MLSTM_HEADS, MLSTM_HEAD_DIM, 2 * MLSTM_HEAD_DIM), F32),
                        pltpu.VMEM((rows * N_MLSTM_HEADS, 1, 1), F32)],
        compiler_params=pltpu.CompilerParams(
            dimension_semantics=("parallel", "arbitrary"), vmem_limit_bytes=VMEM_LIMIT),
        name="mlstm",
    )(qm, kmT, vm, om, gif, gifT, g)


def _mix_kernel(x_ref, a_ref, y_ref, g_ref, wg_ref, wa_ref, wm_ref, wo_ref, out_ref):
    d = x_ref.shape[-1]
    x = x_ref[...]
    h = _rms(x, g_ref[...]).astype(BF16)
    gate_a = _sigmoid(_dot(h, wg_ref[:, :d]))
    gate_m = _sigmoid(_dot(h, wg_ref[:, d:]))
    merged = gate_a * _dot(a_ref[...], wa_ref[...]) + gate_m * _dot(y_ref[...], wm_ref[...])
    out_ref[...] = x + _dot(merged.astype(BF16), wo_ref[...])


def _mix(x2d, a2d, y2d, g, wg, wa, wm, wo):
    t, d = x2d.shape
    tm = MIX_TILE
    row = lambda cdim: pl.BlockSpec((tm, cdim), lambda i: (i, 0))
    return pl.pallas_call(
        _mix_kernel,
        grid=(t // tm,),
        in_specs=[row(d), row(a2d.shape[1]), row(y2d.shape[1]), _const_spec(g.shape),
                  _const_spec(wg.shape), _const_spec(wa.shape), _const_spec(wm.shape), _const_spec(wo.shape)],
        out_specs=row(d),
        out_shape=jax.ShapeDtypeStruct((t, d), F32),
        compiler_params=pltpu.CompilerParams(
            dimension_semantics=("parallel",), vmem_limit_bytes=VMEM_LIMIT),
        name="mix",
    )(x2d, a2d, y2d, g, wg, wa, wm, wo)


def _ffn_kernel(x_ref, xh_ref, g_ref, wup_ref, cg_ref, cu_ref, wd_ref, gf_ref, out_ref,
                hext_ref, ug_ref, uu_ref, act_ref, actall_ref):
    tm = FFN_TILE
    nslab = ug_ref.shape[0]
    nchunk = actall_ref.shape[1] // FFN_CHUNK
    j = pl.program_id(1)
    x = x_ref[0]
    g = g_ref[...]
    hext_ref[HALO:HALO + tm, :] = _rms(x, g).astype(BF16)
    hh = _rms(xh_ref[0], g)
    hext_ref[0:HALO, :] = jnp.where(j == 0, 0.0, hh).astype(BF16)

    def stage_a(c):
        gate_cols = pl.ds(pl.multiple_of(c * FFN_CHUNK, LANES), FFN_CHUNK)
        up_cols = pl.ds(pl.multiple_of(nchunk * FFN_CHUNK + c * FFN_CHUNK, LANES), FFN_CHUNK)
        ug = _dot(hext_ref[...], wup_ref[:, gate_cols])
        uu = _dot(hext_ref[...], wup_ref[:, up_cols])
        for sl in range(nslab):
            ug_ref[sl] = ug[:, sl * LANES:(sl + 1) * LANES]
            uu_ref[sl] = uu[:, sl * LANES:(sl + 1) * LANES]

    def stage_b(c):
        for sl in range(nslab):
            cg = cg_ref[c][:, sl * LANES:(sl + 1) * LANES]
            cu = cu_ref[c][:, sl * LANES:(sl + 1) * LANES]
            for r0 in range(0, tm, 8 * CONV_STRIDE):
                gates = _conv_block(ug_ref, sl, r0, cg)
                ups = _conv_block(uu_ref, sl, r0, cu)
                for k in range(CONV_STRIDE):
                    act_ref[sl, pl.ds(r0 + k, 8, stride=CONV_STRIDE), :] = gates[k] * _sigmoid(gates[k]) * ups[k]
        act = jnp.concatenate([act_ref[sl] for sl in range(nslab)], axis=1)
        actall_ref[:, pl.ds(pl.multiple_of(c * FFN_CHUNK, LANES), FFN_CHUNK)] = act.astype(BF16)

    stage_a(0)

    def body(c, carry):
        stage_b(c - 1)
        stage_a(c)
        return carry

    lax.fori_loop(1, nchunk, body, 0, unroll=2)
    stage_b(nchunk - 1)
    out_ref[0] = _rms(x + _dot(actall_ref[...], wd_ref[...]), gf_ref[...])


def _ffn(x1, g2, wup, cg, cu, wd, gf):
    b, s, d = x1.shape
    tm = FFN_TILE
    return pl.pallas_call(
        _ffn_kernel,
        grid=(b, s // tm),
        in_specs=[
            pl.BlockSpec((1, tm, d), lambda bi, ji: (bi, ji, 0)),
            pl.BlockSpec((1, HALO, d), lambda bi, ji: (bi, jnp.maximum(ji * (tm // HALO) - 1, 0), 0)),
            _const_spec(g2.shape), _const_spec(wup.shape),
            _const_spec(cg.shape), _const_spec(cu.shape), _const_spec(wd.shape), _const_spec(gf.shape),
        ],
        out_specs=pl.BlockSpec((1, tm, d), lambda bi, ji: (bi, ji, 0)),
        out_shape=jax.ShapeDtypeStruct((b, s, d), F32),
        scratch_shapes=[pltpu.VMEM((HALO + tm, d), BF16),
                        pltpu.VMEM((FFN_CHUNK // LANES, HALO + tm, LANES), F32),
                        pltpu.VMEM((FFN_CHUNK // LANES, HALO + tm, LANES), F32),
                        pltpu.VMEM((FFN_CHUNK // LANES, tm, LANES), F32),
                        pltpu.VMEM((tm, wd.shape[0]), BF16)],
        compiler_params=pltpu.CompilerParams(
            dimension_semantics=("parallel", "parallel"), vmem_limit_bytes=VMEM_LIMIT),
        name="ffn",
    )(x1, x1, g2, wup, cg, cu, wd, gf)


def _chunk_cols(w, chunk):
    kdim, n = w.shape
    return w.reshape(kdim, n // chunk, chunk).transpose(1, 0, 2)


def kernel(x, positions, norm_mix_g, w_in, conv_mlstm, i_bias, f_bias, mlstm_norm_g, w_branch_attn,
           w_branch_mlstm, w_out, norm_ffn_g, w_up, conv_ffn, w_down, norm_final_g):
    b, s, d = x.shape
    aw, mw, nh = ATTN_WIDTH, MLSTM_WIDTH, N_MLSTM_HEADS
    assert s % FFN_TILE == 0 and s % SEQ_TILE == 0 and s % MLSTM_BLOCK == 0 and (b * s) % MIX_TILE == 0 and d % LANES == 0
    assert w_in.shape[0] == 1, "the ffn kernel fuses the output norm, so exactly one layer is supported"
    n_qkv = 3 * aw + 4 * mw
    d_ff = w_down.shape[1]
    assert d_ff % FFN_CHUNK == 0
    half = ATTN_HEAD_DIM // 2
    inv_freq = ROPE_THETA ** (-(jnp.arange(half, dtype=F32) / half))
    cos, sin = _rope_tables(positions, inv_freq)

    for layer in range(w_in.shape[0]):
        wl = w_in[layer]
        w1 = jnp.concatenate(
            [wl[:, :n_qkv].astype(BF16), wl[:, n_qkv:n_qkv + 2 * nh].astype(BF16),
             jnp.zeros((d, LANES - 2 * nh), BF16)], axis=1)
        wg = wl[:, n_qkv + 2 * nh:].astype(BF16)
        bias = jnp.concatenate(
            [i_bias[layer], f_bias[layer], jnp.zeros((LANES - 2 * nh,), F32)]).reshape(1, LANES)

        qT, k, kmean, vT, qm, kmT, vm, om, gif, gifT = _in_proj(
            x, cos, sin, norm_mix_g[layer].reshape(1, d), w1, conv_mlstm[layer], bias)
        attn = _moba(qT, k, vT, kmean.reshape(b, s // MOBA_BLOCK, aw))
        y_m = _mlstm(qm, kmT, vm, om, gif, gifT, mlstm_norm_g[layer].reshape(1, mw))
        x1 = _mix(x.reshape(b * s, d), attn.reshape(b * s, aw), y_m.reshape(b * s, mw),
                  norm_mix_g[layer].reshape(1, d), wg, w_branch_attn[layer].astype(BF16),
                  w_branch_mlstm[layer].astype(BF16), w_out[layer].astype(BF16)).reshape(b, s, d)

        wu = w_up[layer].astype(BF16)
        cf = conv_ffn[layer]
        x = _ffn(x1, norm_ffn_g[layer].reshape(1, d), wu,
                 _chunk_cols(cf[:, :d_ff], FFN_CHUNK), _chunk_cols(cf[:, d_ff:], FFN_CHUNK),
                 w_down[layer].astype(BF16), norm_final_g.reshape(1, d))
    return x
```

```python
import math

import jax
import jax.numpy as jnp
from jax import lax
from jax.experimental import pallas as pl
from jax.experimental.pallas import tpu as pltpu

F32 = jnp.float32
BF16 = jnp.bfloat16

EPS = 1e-6
NEG = -1e30
ROPE_THETA = 10000.0

N_ATTN_HEADS = 8
ATTN_HEAD_DIM = 64
ATTN_WIDTH = N_ATTN_HEADS * ATTN_HEAD_DIM
MOBA_BLOCK = 256
MOBA_TOPK = 3
N_MLSTM_HEADS = 4
MLSTM_HEAD_DIM = 128
MLSTM_WIDTH = N_MLSTM_HEADS * MLSTM_HEAD_DIM

LANES = 128
HALO = 16
SEQ_TILE = MOBA_BLOCK
MIX_TILE = 1024
FFN_TILE = 1024
FFN_CHUNK = 256
CONV_STRIDE = 4
SUM_ROWS = 16
MLSTM_BLOCK = 256
MLSTM_CHUNK = 128
MLSTM_ROWS = 1
MOBA_PAIRS_PER_STEP = 4
LOG2_E = math.log2(math.e)
VMEM_LIMIT = 56 * 1024 * 1024


def _dot(a, b):
    return jnp.dot(a, b, preferred_element_type=F32)


def _rms(x, g):
    ms = jnp.mean(x * x, axis=-1, keepdims=True)
    return x * lax.rsqrt(ms + EPS) * g


def _sigmoid(x):
    return 1.0 / (1.0 + jnp.exp(-x))


def _conv_block(u_ref, sl, r0, cw):
    ntap = cw.shape[0]
    rows = {k: u_ref[sl, pl.ds(HALO + r0 + k, 8, stride=CONV_STRIDE), :] for k in range(1 - ntap, CONV_STRIDE)}
    outs = []
    for k in range(CONV_STRIDE):
        out = cw[0:1] * rows[k - ntap + 1]
        for j in range(1, ntap):
            out = out + cw[j:j + 1] * rows[k - ntap + 1 + j]
        outs.append(out)
    return outs


def _const_spec(shape):
    nd = len(shape)
    return pl.BlockSpec(shape, lambda *_: (0,) * nd, pipeline_mode=pl.Buffered(1))


def _rope_table_kernel(pos_ref, invf_ref, cos_ref, sin_ref):
    rows, per_row = pos_ref.shape
    half = LANES // per_row
    lane = lax.broadcasted_iota(jnp.int32, (rows, LANES), 1)
    pos = pos_ref[...]
    pos_rep = pos[:, per_row - 1:per_row]
    for g in range(per_row - 2, -1, -1):
        pos_rep = jnp.where(lane < (g + 1) * half, pos[:, g:g + 1], pos_rep)
    ang = pos_rep.astype(F32) * invf_ref[...]
    for table, out_ref in ((jnp.cos(ang), cos_ref), (jnp.sin(ang), sin_ref)):
        for g in range(per_row):
            base = table if g == 0 else pltpu.roll(table, LANES - g * half, 1)
            width = half
            while width < LANES:
                base = jnp.where(lane < width, base, pltpu.roll(base, width, 1))
                width *= 2
            out_ref[pl.ds(g, rows, stride=per_row), :] = base


def _rope_tables(positions, inv_freq):
    b, s = positions.shape
    half = inv_freq.shape[0]
    per_row = LANES // half
    rows = b * s // per_row
    tile = min(rows, 512)
    invf = jnp.tile(inv_freq, per_row).reshape(1, LANES)
    out_spec = pl.BlockSpec((tile * per_row, LANES), lambda i: (i, 0))
    cos, sin = pl.pallas_call(
        _rope_table_kernel,
        grid=(rows // tile,),
        in_specs=[pl.BlockSpec((tile, per_row), lambda i: (i, 0)), _const_spec(invf.shape)],
        out_specs=[out_spec, out_spec],
        out_shape=[jax.ShapeDtypeStruct((b * s, LANES), F32)] * 2,
        compiler_params=pltpu.CompilerParams(dimension_semantics=("parallel",)),
        name="rope_tables",
    )(positions.reshape(rows, per_row), invf)
    return cos.reshape(b, s, LANES), sin.reshape(b, s, LANES)


def _in_proj_kernel(x_ref, xh_ref, cos_ref, sin_ref, g_ref, w_ref, cw_ref, bias_ref,
                    qT_ref, k_ref, kmean_ref, vT_ref, qm_ref, kmT_ref, vm_ref, om_ref, gif_ref, gifT_ref,
                    ext_ref, act_ref):
    tm = SEQ_TILE
    aw, mw = ATTN_WIDTH, MLSTM_WIDTH
    j = pl.program_id(1)
    g = g_ref[...]
    h = _rms(x_ref[0], g).astype(BF16)

    cos = cos_ref[0]
    sin = sin_ref[0]
    lane = lax.broadcasted_iota(jnp.int32, (tm, LANES), 1)
    half = ATTN_HEAD_DIM // 2
    first_half = (lane % ATTN_HEAD_DIM) < half
    sin_signed = jnp.where(first_half, -sin, sin)

    def rope(t):
        outs = []
        for c in range(aw // LANES):
            tg = t[:, c * LANES:(c + 1) * LANES]
            swapped = jnp.where(first_half, pltpu.roll(tg, LANES - half, 1), pltpu.roll(tg, half, 1))
            outs.append(tg * cos + swapped * sin_signed)
        return jnp.concatenate(outs, axis=1)

    qk = _dot(h, w_ref[:, 0:2 * aw])
    q = rope(qk[:, :aw]) * (LOG2_E / math.sqrt(ATTN_HEAD_DIM))
    k = rope(qk[:, aw:])
    qT_ref[0, 0] = q.T.astype(BF16)
    k_ref[0, 0] = k.astype(BF16)
    kmean_ref[0] = jnp.mean(k, axis=0, keepdims=True)
    v = _dot(h, w_ref[:, 2 * aw:3 * aw])
    vT_ref[0, 0] = v.T.astype(BF16)

    c0 = 3 * aw
    pm = _dot(h, w_ref[:, c0:c0 + 2 * mw])
    hh = _rms(xh_ref[0], g).astype(BF16)
    ph = jnp.where(j == 0, 0.0, _dot(hh, w_ref[:, c0:c0 + 2 * mw]))
    cw = cw_ref[...]
    nslab = 2 * mw // LANES
    for sl in range(nslab):
        cols = slice(sl * LANES, (sl + 1) * LANES)
        ext_ref[sl, 0:HALO, :] = ph[:, cols]
        ext_ref[sl, HALO:HALO + tm, :] = pm[:, cols]
    for sl in range(nslab):
        cws = cw[:, sl * LANES:(sl + 1) * LANES]
        for r0 in range(0, tm, 8 * CONV_STRIDE):
            for kk, conv in enumerate(_conv_block(ext_ref, sl, r0, cws)):
                act_ref[sl, pl.ds(r0 + kk, 8, stride=CONV_STRIDE), :] = conv * _sigmoid(conv)
    q_act = jnp.concatenate([act_ref[sl] for sl in range(nslab // 2)], axis=1)
    k_act = jnp.concatenate([act_ref[sl] for sl in range(nslab // 2, nslab)], axis=1)
    qm_ref[0] = (q_act * (1.0 / math.sqrt(MLSTM_HEAD_DIM))).astype(BF16)
    kmT_ref[0, 0] = k_act.T.astype(BF16)

    c1 = c0 + 2 * mw
    vo = _dot(h, w_ref[:, c1:c1 + 2 * mw])
    vm_ref[0] = vo[:, :mw].astype(BF16)
    om_ref[0] = vo[:, mw:]

    c2 = c1 + 2 * mw
    gi = _dot(h, w_ref[:, c2:c2 + LANES]) + bias_ref[...]
    log_sig = jnp.minimum(gi, 0.0) - jnp.log1p(jnp.exp(-jnp.abs(gi)))
    gt = jnp.where(lane < N_MLSTM_HEADS, gi, log_sig)
    gif_ref[0] = gt[:, 0:2 * N_MLSTM_HEADS]
    gifT_ref[0] = gt.T[0:2 * N_MLSTM_HEADS, :]


def _in_proj(x, cos, sin, g, w, cw, bias):
    b, s, d = x.shape
    tm = SEQ_TILE
    nb = s // tm
    aw, mw = ATTN_WIDTH, MLSTM_WIDTH
    nh2 = 2 * N_MLSTM_HEADS
    out_shape = [
        jax.ShapeDtypeStruct((b, nb, aw, tm), BF16),
        jax.ShapeDtypeStruct((b, nb, tm, aw), BF16),
        jax.ShapeDtypeStruct((b * nb, 1, aw), F32),
        jax.ShapeDtypeStruct((b, nb, aw, tm), BF16),
        jax.ShapeDtypeStruct((b, s, mw), BF16),
        jax.ShapeDtypeStruct((b, nb, mw, tm), BF16),
        jax.ShapeDtypeStruct((b, s, mw), BF16),
        jax.ShapeDtypeStruct((b, s, mw), F32),
        jax.ShapeDtypeStruct((b, s, nh2), F32),
        jax.ShapeDtypeStruct((b, nh2, s), F32),
    ]
    blk4 = lambda r, c: pl.BlockSpec((1, 1, r, c), lambda bi, ji: (bi, ji, 0, 0))
    row3 = lambda c: pl.BlockSpec((1, tm, c), lambda bi, ji: (bi, ji, 0))
    out_specs = [
        blk4(aw, tm), blk4(tm, aw),
        pl.BlockSpec((1, 1, aw), lambda bi, ji: (bi * nb + ji, 0, 0)),
        blk4(aw, tm), row3(mw), blk4(mw, tm), row3(mw), row3(mw), row3(nh2),
        pl.BlockSpec((1, nh2, tm), lambda bi, ji: (bi, 0, ji)),
    ]
    in_specs = [
        pl.BlockSpec((1, tm, d), lambda bi, ji: (bi, ji, 0)),
        pl.BlockSpec((1, HALO, d), lambda bi, ji: (bi, jnp.maximum(ji * (tm // HALO) - 1, 0), 0)),
        row3(LANES), row3(LANES),
        _const_spec(g.shape), _const_spec(w.shape), _const_spec(cw.shape), _const_spec(bias.shape),
    ]
    return pl.pallas_call(
        _in_proj_kernel,
        grid=(b, nb),
        in_specs=in_specs,
        out_specs=out_specs,
        out_shape=out_shape,
        scratch_shapes=[pltpu.VMEM((2 * mw // LANES, HALO + tm, LANES), F32),
                        pltpu.VMEM((2 * mw // LANES, tm, LANES), F32)],
        compiler_params=pltpu.CompilerParams(
            dimension_semantics=("parallel", "parallel"), vmem_limit_bytes=VMEM_LIMIT),
        name="in_proj",
    )(x, x, cos, sin, g, w, cw, bias)


def _moba_kernel(jtab_ref, ntab_ref, qT_ref, k_ref, vT_ref, kmean_ref, out_ref,
                 bias_ref, m_ref, acc_ref, s_ref, cm_ref, p_ref, a_ref):
    blk = MOBA_BLOCK
    hd = ATTN_HEAD_DIM
    nb = k_ref.shape[1]
    nh = 2 * (qT_ref.shape[2] // LANES)
    n_units = nb * (nb - 1) // 2
    dim_row = lax.broadcasted_iota(jnp.int32, (LANES, blk), 0)
    dim_col = lax.broadcasted_iota(jnp.int32, (blk, LANES), 1)
    blk_id = lax.broadcasted_iota(jnp.int32, (nb, blk), 0)
    blk_idf = blk_id.astype(F32)
    key_row = lax.broadcasted_iota(jnp.int32, (blk, blk), 0)
    qry_col = lax.broadcasted_iota(jnp.int32, (blk, blk), 1)

    def lanes(h):
        return slice((h // 2) * LANES, (h // 2 + 1) * LANES)

    ones_rows = jnp.ones((SUM_ROWS, blk), BF16)

    def v_ext(n, h):
        return jnp.concatenate([vT_ref[0, n, h * hd:(h + 1) * hd, :], ones_rows], axis=0)

    def phase_a(j, carry):
        for h in range(nh):
            qT = qT_ref[0, j, lanes(h), :]
            qh = jnp.where((dim_row >= hd) == (h % 2 == 1), qT, jnp.zeros_like(qT))
            km = kmean_ref[0, :, lanes(h)]
            km_hi = km.astype(BF16)
            km_lo = (km - km_hi.astype(F32)).astype(BF16)
            gate = _dot(km_hi, qh) + _dot(km_lo, qh)
            gate = jnp.where(blk_id < j, gate, NEG)
            picked = blk_id < 0
            for _ in range(min(MOBA_TOPK, nb)):
                top = jnp.max(gate, axis=0, keepdims=True)
                first = jnp.min(jnp.where(gate == top, blk_idf, float(nb)), axis=0, keepdims=True)
                hit = blk_idf == first
                picked = picked | hit
                gate = jnp.where(hit, -jnp.inf, gate)
            selected = picked & (blk_id < j)
            bias_ref[h, j, 0:nb, :] = jnp.where(selected, 0.0, NEG)
            bias_ref[h, j, nb:nb + 1, :] = jnp.full((1, blk), NEG, F32)
            s_own = jnp.where(key_row <= qry_col, _dot(k_ref[0, j, :, lanes(h)], qh), NEG)
            m0 = jnp.max(s_own, axis=0, keepdims=True)
            p0 = jnp.exp2(s_own - m0)
            m_ref[h, j] = m0
            acc_ref[h, j, 0:hd, :] = _dot(vT_ref[0, j, h * hd:(h + 1) * hd, :], p0.astype(BF16))
            acc_ref[h, j, hd:hd + SUM_ROWS, :] = jnp.broadcast_to(jnp.sum(p0, axis=0, keepdims=True), (SUM_ROWS, blk))
        return carry

    lax.fori_loop(0, nb, phase_a, 0, unroll=4)

    for h in range(nh):
        s_ref[h] = jnp.zeros((blk, blk), F32)
        cm_ref[h] = jnp.zeros((1, blk), F32)
        p_ref[h] = jnp.zeros((blk, blk), BF16)
        a_ref[h] = jnp.ones((1, blk), F32)

    def phase_b(i, carry):
        j1, n1 = jtab_ref[i + 2], ntab_ref[i + 2]
        j2, n2 = jtab_ref[i + 1], ntab_ref[i + 1]
        j3, n3 = jtab_ref[i], ntab_ref[i]
        k_blk = jnp.minimum(n1, nb - 1)
        v_blk = jnp.minimum(n3, nb - 1)
        for h in range(nh):
            acc_ref[h, j3] = a_ref[h] * acc_ref[h, j3] + _dot(v_ext(v_blk, h), p_ref[h])
        for h in range(nh):
            bias = bias_ref[h, j2, pl.ds(n2, 1), :]
            m_old = m_ref[h, j2]
            m_new = jnp.maximum(m_old, cm_ref[h] + bias)
            alpha = jnp.exp2(m_old - m_new)
            p = jnp.exp2(s_ref[h] - (m_new - bias))
            m_ref[h, j2] = m_new
            a_ref[h] = alpha
            p_ref[h] = p.astype(BF16)
        for h in range(nh):
            kb = k_ref[0, k_blk, :, lanes(h)]
            kb = jnp.where((dim_col >= hd) == (h % 2 == 1), kb, jnp.zeros_like(kb))
            sT = _dot(kb, qT_ref[0, j1, lanes(h), :])
            s_ref[h] = sT
            cm_ref[h] = jnp.max(sT, axis=0, keepdims=True)
        return carry

    lax.fori_loop(0, n_units + 2, phase_b, 0, unroll=2)

    def phase_c(j, carry):
        row0 = pl.multiple_of(j * blk, blk)
        for pair in range(nh // 2):
            oT = jnp.concatenate([acc_ref[h, j, 0:hd, :] / acc_ref[h, j, hd:hd + 1, :]
                                  for h in (2 * pair, 2 * pair + 1)], axis=0)
            out_ref[0, pl.ds(row0, blk), pair * LANES:(pair + 1) * LANES] = oT.T.astype(BF16)
        return carry

    lax.fori_loop(0, nb, phase_c, 0)


def _moba_units(nb):
    pad = (0, nb)
    units = [pad, pad] + [(j, n) for j in range(nb) for n in range(j)] + [pad, pad]
    jtab = jnp.asarray([u[0] for u in units], jnp.int32)
    ntab = jnp.asarray([u[1] for u in units], jnp.int32)
    return jtab, ntab


def _moba(qT, k, vT, kmean):
    b, nb, aw, blk = qT.shape
    s = nb * blk
    w = MOBA_PAIRS_PER_STEP * LANES
    nh = 2 * MOBA_PAIRS_PER_STEP
    jtab, ntab = _moba_units(nb)
    grid_spec = pltpu.PrefetchScalarGridSpec(
        num_scalar_prefetch=2,
        grid=(b, aw // w),
        in_specs=[
            pl.BlockSpec((1, nb, w, blk), lambda bi, pi, *_: (bi, 0, pi, 0)),
            pl.BlockSpec((1, nb, blk, w), lambda bi, pi, *_: (bi, 0, 0, pi)),
            pl.BlockSpec((1, nb, w, blk), lambda bi, pi, *_: (bi, 0, pi, 0)),
            pl.BlockSpec((1, nb, w), lambda bi, pi, *_: (bi, 0, pi)),
        ],
        out_specs=pl.BlockSpec((1, s, w), lambda bi, pi, *_: (bi, 0, pi)),
        scratch_shapes=[pltpu.VMEM((nh, nb, nb + 8, blk), F32),
                        pltpu.VMEM((nh, nb, 1, blk), F32),
                        pltpu.VMEM((nh, nb, ATTN_HEAD_DIM + SUM_ROWS, blk), F32),
                        pltpu.VMEM((nh, blk, blk), F32),
                        pltpu.VMEM((nh, 1, blk), F32),
                        pltpu.VMEM((nh, blk, blk), BF16),
                        pltpu.VMEM((nh, 1, blk), F32)],
    )
    return pl.pallas_call(
        _moba_kernel,
        grid_spec=grid_spec,
        out_shape=jax.ShapeDtypeStruct((b, s, aw), BF16),
        compiler_params=pltpu.CompilerParams(
            dimension_semantics=("parallel", "parallel"), vmem_limit_bytes=VMEM_LIMIT),
        name="moba",
    )(jtab, ntab, qT, k, vT, kmean)


def _split3(x):
    hi = x.astype(BF16)
    r = x - hi.astype(F32)
    mid = r.astype(BF16)
    lo = (r - mid.astype(F32)).astype(BF16)
    return hi, mid, lo


def _mlstm_kernel(q_ref, kT_ref, v_ref, o_ref, gif_ref, gifT_ref, g_ref, out_ref, c_ref, m_ref):
    L = MLSTM_CHUNK
    nh, hd = N_MLSTM_HEADS, MLSTM_HEAD_DIM
    c = pl.program_id(1)

    @pl.when(c == 0)
    def _():
        c_ref[...] = jnp.zeros_like(c_ref)
        m_ref[...] = jnp.zeros_like(m_ref)

    t_idx = lax.broadcasted_iota(jnp.int32, (L, L), 0)
    s_idx = lax.broadcasted_iota(jnp.int32, (L, L), 1)
    causal = s_idx <= t_idx
    tril = jnp.where(causal, 1.0, 0.0).astype(BF16)
    triu = jnp.where(t_idx <= s_idx, 1.0, 0.0).astype(BF16)

    ones_col = jnp.where(lax.broadcasted_iota(jnp.int32, (L, hd), 1) == 0, 1.0, 0.0).astype(BF16)
    gn = g_ref[...]

    for r, r0 in [(r, r0) for r in range(q_ref.shape[0]) for r0 in range(0, q_ref.shape[1], L)]:
        rows = slice(r0, r0 + L)
        gi_col = gif_ref[r, rows, :]
        gi_row = gifT_ref[r, :, rows]
        b_col_all = sum(_dot(tril, part) for part in _split3(gi_col))
        b_row_all = sum(_dot(part, triu) for part in _split3(gi_row))

        for h in range(nh):
            cols = slice(h * hd, (h + 1) * hd)
            a_row = gi_row[h:h + 1, :]
            b_row = b_row_all[nh + h:nh + h + 1, :]
            b_col = b_col_all[:, nh + h:nh + h + 1]
            q = q_ref[r, rows, cols]
            if L >= SEQ_TILE:
                kT = jnp.concatenate([kT_ref[r, (r0 + i) // SEQ_TILE, cols, :] for i in range(0, L, SEQ_TILE)], axis=1)
            else:
                kT = kT_ref[r, r0 // SEQ_TILE, cols, r0 % SEQ_TILE:r0 % SEQ_TILE + L]
            v_ext = jnp.concatenate([v_ref[r, rows, cols], ones_col], axis=1)

            dmat = jnp.where(causal, b_col - b_row + a_row, NEG)
            mi = jnp.max(dmat, axis=1, keepdims=True)
            intra = _dot((jnp.exp(dmat - mi) * _dot(q, kT)).astype(BF16), v_ext)
            b_last = b_row[:, L - 1:L]
            dec = b_last - b_row + a_row
            md = jnp.max(dec, axis=1, keepdims=True)
            kv = _dot((kT.astype(F32) * jnp.exp(dec - md)).astype(BF16), v_ext)

            m_prev = m_ref[r * nh + h]
            c_old = c_ref[r * nh + h]
            inter = b_col + m_prev
            m_t = jnp.maximum(inter, mi)
            num_ext = jnp.exp(inter - m_t) * _dot(q, c_old.astype(BF16)) + jnp.exp(mi - m_t) * intra
            num = num_ext[:, :hd]
            den = num_ext[:, hd:hd + 1]
            h_out = num / jnp.maximum(jnp.abs(den), jnp.exp(-m_t))

            m_new = jnp.maximum(b_last + m_prev, md)
            c_ref[r * nh + h] = jnp.exp(b_last + m_prev - m_new) * c_old + jnp.exp(md - m_new) * kv
            m_ref[r * nh + h] = m_new

            hn = _rms(h_out, gn[:, cols])
            gate_o = _sigmoid(o_ref[r, rows, cols])
            out_ref[r, rows, cols] = (hn * gate_o).astype(BF16)


def _mlstm(qm, kmT, vm, om, gif, gifT, g):
    b, s, mw = qm.shape
    L = MLSTM_BLOCK
    nc = s // L
    nh2 = 2 * N_MLSTM_HEADS
    rows = MLSTM_ROWS if b % MLSTM_ROWS == 0 else 1
    row = lambda cdim: pl.BlockSpec((rows, L, cdim), lambda bi, ci: (bi, ci, 0))
    return pl.pallas_call(
        _mlstm_kernel,
        grid=(b // rows, nc),
        in_specs=[
            row(mw),
            pl.BlockSpec((rows, L // SEQ_TILE, mw, SEQ_TILE), lambda bi, ci: (bi, ci, 0, 0)),
            row(mw), row(mw), row(nh2),
            pl.BlockSpec((rows, nh2, L), lambda bi, ci: (bi, 0, ci)),
            _const_spec(g.shape),
        ],
        out_specs=row(mw),
        out_shape=jax.ShapeDtypeStruct((b, s, mw), BF16),
        scratch_shapes=[pltpu.VMEM((rows * N_MLSTM_HEADS, MLSTM_HEAD_DIM, 2 * MLSTM_HEAD_DIM), F32),
                        pltpu.VMEM((rows * N_MLSTM_HEADS, 1, 1), F32)],
        compiler_params=pltpu.CompilerParams(
            dimension_semantics=("parallel", "arbitrary"), vmem_limit_bytes=VMEM_LIMIT),
        name="mlstm",
    )(qm, kmT, vm, om, gif, gifT, g)


def _mix_kernel(x_ref, a_ref, y_ref, g_ref, wg_ref, wa_ref, wm_ref, wo_ref, out_ref):
    d = x_ref.shape[-1]
    x = x_ref[...]
    h = _rms(x, g_ref[...]).astype(BF16)
    gate_a = _sigmoid(_dot(h, wg_ref[:, :d]))
    gate_m = _sigmoid(_dot(h, wg_ref[:, d:]))
    merged = gate_a * _dot(a_ref[...], wa_ref[...]) + gate_m * _dot(y_ref[...], wm_ref[...])
    out_ref[...] = x + _dot(merged.astype(BF16), wo_ref[...])


def _mix(x2d, a2d, y2d, g, wg, wa, wm, wo):
    t, d = x2d.shape
    tm = MIX_TILE
    row = lambda cdim: pl.BlockSpec((tm, cdim), lambda i: (i, 0))
    return pl.pallas_call(
        _mix_kernel,
        grid=(t // tm,),
        in_specs=[row(d), row(a2d.shape[1]), row(y2d.shape[1]), _const_spec(g.shape),
                  _const_spec(wg.shape), _const_spec(wa.shape), _const_spec(wm.shape), _const_spec(wo.shape)],
        out_specs=row(d),
        out_shape=jax.ShapeDtypeStruct((t, d), F32),
        compiler_params=pltpu.CompilerParams(
            dimension_semantics=("parallel",), vmem_limit_bytes=VMEM_LIMIT),
        name="mix",
    )(x2d, a2d, y2d, g, wg, wa, wm, wo)


def _ffn_kernel(x_ref, xh_ref, g_ref, wup_ref, cg_ref, cu_ref, wd_ref, gf_ref, out_ref,
                hext_ref, ug_ref, uu_ref, act_ref, actall_ref):
    tm = FFN_TILE
    nslab = ug_ref.shape[0]
    nchunk = actall_ref.shape[1] // FFN_CHUNK
    j = pl.program_id(1)
    x = x_ref[0]
    g = g_ref[...]
    hext_ref[HALO:HALO + tm, :] = _rms(x, g).astype(BF16)
    hh = _rms(xh_ref[0], g)
    hext_ref[0:HALO, :] = jnp.where(j == 0, 0.0, hh).astype(BF16)

    def stage_a(c):
        gate_cols = pl.ds(pl.multiple_of(c * FFN_CHUNK, LANES), FFN_CHUNK)
        up_cols = pl.ds(pl.multiple_of(nchunk * FFN_CHUNK + c * FFN_CHUNK, LANES), FFN_CHUNK)
        ug = _dot(hext_ref[...], wup_ref[:, gate_cols])
        uu = _dot(hext_ref[...], wup_ref[:, up_cols])
        for sl in range(nslab):
            ug_ref[sl] = ug[:, sl * LANES:(sl + 1) * LANES]
            uu_ref[sl] = uu[:, sl * LANES:(sl + 1) * LANES]

    def stage_b(c):
        for sl in range(nslab):
            cg = cg_ref[c][:, sl * LANES:(sl + 1) * LANES]
            cu = cu_ref[c][:, sl * LANES:(sl + 1) * LANES]
            for r0 in range(0, tm, 8 * CONV_STRIDE):
                gates = _conv_block(ug_ref, sl, r0, cg)
                ups = _conv_block(uu_ref, sl, r0, cu)
                for k in range(CONV_STRIDE):
                    act_ref[sl, pl.ds(r0 + k, 8, stride=CONV_STRIDE), :] = gates[k] * _sigmoid(gates[k]) * ups[k]
        act = jnp.concatenate([act_ref[sl] for sl in range(nslab)], axis=1)
        actall_ref[:, pl.ds(pl.multiple_of(c * FFN_CHUNK, LANES), FFN_CHUNK)] = act.astype(BF16)

    stage_a(0)

    def body(c, carry):
        stage_b(c - 1)
        stage_a(c)
        return carry

    lax.fori_loop(1, nchunk, body, 0, unroll=2)
    stage_b(nchunk - 1)
    out_ref[0] = _rms(x + _dot(actall_ref[...], wd_ref[...]), gf_ref[...])


def _ffn(x1, g2, wup, cg, cu, wd, gf):
    b, s, d = x1.shape
    tm = FFN_TILE
    return pl.pallas_call(
        _ffn_kernel,
        grid=(b, s // tm),
        in_specs=[
            pl.BlockSpec((1, tm, d), lambda bi, ji: (bi, ji, 0)),
            pl.BlockSpec((1, HALO, d), lambda bi, ji: (bi, jnp.maximum(ji * (tm // HALO) - 1, 0), 0)),
            _const_spec(g2.shape), _const_spec(wup.shape),
            _const_spec(cg.shape), _const_spec(cu.shape), _const_spec(wd.shape), _const_spec(gf.shape),
        ],
        out_specs=pl.BlockSpec((1, tm, d), lambda bi, ji: (bi, ji, 0)),
        out_shape=jax.ShapeDtypeStruct((b, s, d), F32),
        scratch_shapes=[pltpu.VMEM((HALO + tm, d), BF16),
                        pltpu.VMEM((FFN_CHUNK // LANES, HALO + tm, LANES), F32),
                        pltpu.VMEM((FFN_CHUNK // LANES, HALO + tm, LANES), F32),
                        pltpu.VMEM((FFN_CHUNK // LANES, tm, LANES), F32),
                        pltpu.VMEM((tm, wd.shape[0]), BF16)],
        compiler_params=pltpu.CompilerParams(
            dimension_semantics=("parallel", "parallel"), vmem_limit_bytes=VMEM_LIMIT),
        name="ffn",
    )(x1, x1, g2, wup, cg, cu, wd, gf)


def _chunk_cols(w, chunk):
    kdim, n = w.shape
    return w.reshape(kdim, n // chunk, chunk).transpose(1, 0, 2)


def kernel(x, positions, norm_mix_g, w_in, conv_mlstm, i_bias, f_bias, mlstm_norm_g, w_branch_attn,
           w_branch_mlstm, w_out, norm_ffn_g, w_up, conv_ffn, w_down, norm_final_g):
    b, s, d = x.shape
    aw, mw, nh = ATTN_WIDTH, MLSTM_WIDTH, N_MLSTM_HEADS
    assert s % FFN_TILE == 0 and s % SEQ_TILE == 0 and s % MLSTM_BLOCK == 0 and (b * s) % MIX_TILE == 0 and d % LANES == 0
    assert w_in.shape[0] == 1, "the ffn kernel fuses the output norm, so exactly one layer is supported"
    n_qkv = 3 * aw + 4 * mw
    d_ff = w_down.shape[1]
    assert d_ff % FFN_CHUNK == 0
    half = ATTN_HEAD_DIM // 2
    inv_freq = ROPE_THETA ** (-(jnp.arange(half, dtype=F32) / half))
    cos, sin = _rope_tables(positions, inv_freq)

    for layer in range(w_in.shape[0]):
        wl = w_in[layer]
        w1 = jnp.concatenate(
            [wl[:, :n_qkv].astype(BF16), wl[:, n_qkv:n_qkv + 2 * nh].astype(BF16),
             jnp.zeros((d, LANES - 2 * nh), BF16)], axis=1)
        wg = wl[:, n_qkv + 2 * nh:].astype(BF16)
        bias = jnp.concatenate(
            [i_bias[layer], f_bias[layer], jnp.zeros((LANES - 2 * nh,), F32)]).reshape(1, LANES)

        qT, k, kmean, vT, qm, kmT, vm, om, gif, gifT = _in_proj(
            x, cos, sin, norm_mix_g[layer].reshape(1, d), w1, conv_mlstm[layer], bias)
        attn = _moba(qT, k, vT, kmean.reshape(b, s // MOBA_BLOCK, aw))
        y_m = _mlstm(qm, kmT, vm, om, gif, gifT, mlstm_norm_g[layer].reshape(1, mw))
        x1 = _mix(x.reshape(b * s, d), attn.reshape(b * s, aw), y_m.reshape(b * s, mw),
                  norm_mix_g[layer].reshape(1, d), wg, w_branch_attn[layer].astype(BF16),
                  w_branch_mlstm[layer].astype(BF16), w_out[layer].astype(BF16)).reshape(b, s, d)

        wu = w_up[layer].astype(BF16)
        cf = conv_ffn[layer]
        x = _ffn(x1, norm_ffn_g[layer].reshape(1, d), wu,
                 _chunk_cols(cf[:, :d_ff], FFN_CHUNK), _chunk_cols(cf[:, d_ff:], FFN_CHUNK),
                 w_down[layer].astype(BF16), norm_final_g.reshape(1, d))
    return x
```

```python
import math

import jax
import jax.numpy as jnp
from jax import lax
from jax.experimental import pallas as pl
from jax.experimental.pallas import tpu as pltpu

F32 = jnp.float32
BF16 = jnp.bfloat16

EPS = 1e-6
NEG = -1e30
ROPE_THETA = 10000.0

N_ATTN_HEADS = 8
ATTN_HEAD_DIM = 64
ATTN_WIDTH = N_ATTN_HEADS * ATTN_HEAD_DIM
MOBA_BLOCK = 256
MOBA_TOPK = 3
N_MLSTM_HEADS = 4
MLSTM_HEAD_DIM = 128
MLSTM_WIDTH = N_MLSTM_HEADS * MLSTM_HEAD_DIM

LANES = 128
HALO = 16
SEQ_TILE = MOBA_BLOCK
MIX_TILE = 1024
FFN_TILE = 1024
FFN_CHUNK = 256
CONV_STRIDE = 4
SUM_ROWS = 16
MLSTM_BLOCK = 256
MLSTM_CHUNK = 128
MLSTM_ROWS = 1
MOBA_PAIRS_PER_STEP = 4
LOG2_E = math.log2(math.e)
VMEM_LIMIT = 56 * 1024 * 1024


def _dot(a, b):
    return jnp.dot(a, b, preferred_element_type=F32)


def _rms(x, g):
    ms = jnp.mean(x * x, axis=-1, keepdims=True)
    return x * lax.rsqrt(ms + EPS) * g


def _sigmoid(x):
    return 1.0 / (1.0 + jnp.exp(-x))


def _conv_block(u_ref, sl, r0, cw):
    ntap = cw.shape[0]
    rows = {k: u_ref[sl, pl.ds(HALO + r0 + k, 8, stride=CONV_STRIDE), :] for k in range(1 - ntap, CONV_STRIDE)}
    outs = []
    for k in range(CONV_STRIDE):
        out = cw[0:1] * rows[k - ntap + 1]
        for j in range(1, ntap):
            out = out + cw[j:j + 1] * rows[k - ntap + 1 + j]
        outs.append(out)
    return outs


def _const_spec(shape):
    nd = len(shape)
    return pl.BlockSpec(shape, lambda *_: (0,) * nd, pipeline_mode=pl.Buffered(1))


def _rope_table_kernel(pos_ref, invf_ref, cos_ref, sin_ref):
    rows, per_row = pos_ref.shape
    half = LANES // per_row
    lane = lax.broadcasted_iota(jnp.int32, (rows, LANES), 1)
    pos = pos_ref[...]
    pos_rep = pos[:, per_row - 1:per_row]
    for g in range(per_row - 2, -1, -1):
        pos_rep = jnp.where(lane < (g + 1) * half, pos[:, g:g + 1], pos_rep)
    ang = pos_rep.astype(F32) * invf_ref[...]
    for table, out_ref in ((jnp.cos(ang), cos_ref), (jnp.sin(ang), sin_ref)):
        for g in range(per_row):
            base = table if g == 0 else pltpu.roll(table, LANES - g * half, 1)
            width = half
            while width < LANES:
                base = jnp.where(lane < width, base, pltpu.roll(base, width, 1))
                width *= 2
            out_ref[pl.ds(g, rows, stride=per_row), :] = base


def _rope_tables(positions, inv_freq):
    b, s = positions.shape
    half = inv_freq.shape[0]
    per_row = LANES // half
    rows = b * s // per_row
    tile = min(rows, 512)
    invf = jnp.tile(inv_freq, per_row).reshape(1, LANES)
    out_spec = pl.BlockSpec((tile * per_row, LANES), lambda i: (i, 0))
    cos, sin = pl.pallas_call(
        _rope_table_kernel,
        grid=(rows // tile,),
        in_specs=[pl.BlockSpec((tile, per_row), lambda i: (i, 0)), _const_spec(invf.shape)],
        out_specs=[out_spec, out_spec],
        out_shape=[jax.ShapeDtypeStruct((b * s, LANES), F32)] * 2,
        compiler_params=pltpu.CompilerParams(dimension_semantics=("parallel",)),
        name="rope_tables",
    )(positions.reshape(rows, per_row), invf)
    return cos.reshape(b, s, LANES), sin.reshape(b, s, LANES)


def _in_proj_kernel(x_ref, xh_ref, cos_ref, sin_ref, g_ref, w_ref, cw_ref, bias_ref,
                    qT_ref, k_ref, kmean_ref, vT_ref, qm_ref, kmT_ref, vm_ref, om_ref, gif_ref, gifT_ref,
                    ext_ref, act_ref):
    tm = SEQ_TILE
    aw, mw = ATTN_WIDTH, MLSTM_WIDTH
    j = pl.program_id(1)
    g = g_ref[...]
    h = _rms(x_ref[0], g).astype(BF16)

    cos = cos_ref[0]
    sin = sin_ref[0]
    lane = lax.broadcasted_iota(jnp.int32, (tm, LANES), 1)
    half = ATTN_HEAD_DIM // 2
    first_half = (lane % ATTN_HEAD_DIM) < half
    sin_signed = jnp.where(first_half, -sin, sin)

    def rope(t):
        outs = []
        for c in range(aw // LANES):
            tg = t[:, c * LANES:(c + 1) * LANES]
            swapped = jnp.where(first_half, pltpu.roll(tg, LANES - half, 1), pltpu.roll(tg, half, 1))
            outs.append(tg * cos + swapped * sin_signed)
        return jnp.concatenate(outs, axis=1)

    qk = _dot(h, w_ref[:, 0:2 * aw])
    q = rope(qk[:, :aw]) * (LOG2_E / math.sqrt(ATTN_HEAD_DIM))
    k = rope(qk[:, aw:])
    qT_ref[0, 0] = q.T.astype(BF16)
    k_ref[0, 0] = k.astype(BF16)
    kmean_ref[0] = jnp.mean(k, axis=0, keepdims=True)
    v = _dot(h, w_ref[:, 2 * aw:3 * aw])
    vT_ref[0, 0] = v.T.astype(BF16)

    c0 = 3 * aw
    pm = _dot(h, w_ref[:, c0:c0 + 2 * mw])
    hh = _rms(xh_ref[0], g).astype(BF16)
    ph = jnp.where(j == 0, 0.0, _dot(hh, w_ref[:, c0:c0 + 2 * mw]))
    cw = cw_ref[...]
    nslab = 2 * mw // LANES
    for sl in range(nslab):
        cols = slice(sl * LANES, (sl + 1) * LANES)
        ext_ref[sl, 0:HALO, :] = ph[:, cols]
        ext_ref[sl, HALO:HALO + tm, :] = pm[:, cols]
    for sl in range(nslab):
        cws = cw[:, sl * LANES:(sl + 1) * LANES]
        for r0 in range(0, tm, 8 * CONV_STRIDE):
            for kk, conv in enumerate(_conv_block(ext_ref, sl, r0, cws)):
                act_ref[sl, pl.ds(r0 + kk, 8, stride=CONV_STRIDE), :] = conv * _sigmoid(conv)
    q_act = jnp.concatenate([act_ref[sl] for sl in range(nslab // 2)], axis=1)
    k_act = jnp.concatenate([act_ref[sl] for sl in range(nslab // 2, nslab)], axis=1)
    qm_ref[0] = (q_act * (1.0 / math.sqrt(MLSTM_HEAD_DIM))).astype(BF16)
    kmT_ref[0, 0] = k_act.T.astype(BF16)

    c1 = c0 + 2 * mw
    vo = _dot(h, w_ref[:, c1:c1 + 2 * mw])
    vm_ref[0] = vo[:, :mw].astype(BF16)
    om_ref[0] = vo[:, mw:]

    c2 = c1 + 2 * mw
    gi = _dot(h, w_ref[:, c2:c2 + LANES]) + bias_ref[...]
    log_sig = jnp.minimum(gi, 0.0) - jnp.log1p(jnp.exp(-jnp.abs(gi)))
    gt = jnp.where(lane < N_MLSTM_HEADS, gi, log_sig)
    gif_ref[0] = gt[:, 0:2 * N_MLSTM_HEADS]
    gifT_ref[0] = gt.T[0:2 * N_MLSTM_HEADS, :]


def _in_proj(x, cos, sin, g, w, cw, bias):
    b, s, d = x.shape
    tm = SEQ_TILE
    nb = s // tm
    aw, mw = ATTN_WIDTH, MLSTM_WIDTH
    nh2 = 2 * N_MLSTM_HEADS
    out_shape = [
        jax.ShapeDtypeStruct((b, nb, aw, tm), BF16),
        jax.ShapeDtypeStruct((b, nb, tm, aw), BF16),
        jax.ShapeDtypeStruct((b * nb, 1, aw), F32),
        jax.ShapeDtypeStruct((b, nb, aw, tm), BF16),
        jax.ShapeDtypeStruct((b, s, mw), BF16),
        jax.ShapeDtypeStruct((b, nb, mw, tm), BF16),
        jax.ShapeDtypeStruct((b, s, mw), BF16),
        jax.ShapeDtypeStruct((b, s, mw), F32),
        jax.ShapeDtypeStruct((b, s, nh2), F32),
        jax.ShapeDtypeStruct((b, nh2, s), F32),
    ]
    blk4 = lambda r, c: pl.BlockSpec((1, 1, r, c), lambda bi, ji: (bi, ji, 0, 0))
    row3 = lambda c: pl.BlockSpec((1, tm, c), lambda bi, ji: (bi, ji, 0))
    out_specs = [
        blk4(aw, tm), blk4(tm, aw),
        pl.BlockSpec((1, 1, aw), lambda bi, ji: (bi * nb + ji, 0, 0)),
        blk4(aw, tm), row3(mw), blk4(mw, tm), row3(mw), row3(mw), row3(nh2),
        pl.BlockSpec((1, nh2, tm), lambda bi, ji: (bi, 0, ji)),
    ]
    in_specs = [
        pl.BlockSpec((1, tm, d), lambda bi, ji: (bi, ji, 0)),
        pl.BlockSpec((1, HALO, d), lambda bi, ji: (bi, jnp.maximum(ji * (tm // HALO) - 1, 0), 0)),
        row3(LANES), row3(LANES),
        _const_spec(g.shape), _const_spec(w.shape), _const_spec(cw.shape), _const_spec(bias.shape),
    ]
    return pl.pallas_call(
        _in_proj_kernel,
        grid=(b, nb),
        in_specs=in_specs,
        out_specs=out_specs,
        out_shape=out_shape,
        scratch_shapes=[pltpu.VMEM((2 * mw // LANES, HALO + tm, LANES), F32),
                        pltpu.VMEM((2 * mw // LANES, tm, LANES), F32)],
        compiler_params=pltpu.CompilerParams(
            dimension_semantics=("parallel", "parallel"), vmem_limit_bytes=VMEM_LIMIT),
        name="in_proj",
    )(x, x, cos, sin, g, w, cw, bias)


def _moba_kernel(jtab_ref, ntab_ref, qT_ref, k_ref, vT_ref, kmean_ref, out_ref,
                 bias_ref, m_ref, acc_ref, s_ref, cm_ref, p_ref, a_ref):
    blk = MOBA_BLOCK
    hd = ATTN_HEAD_DIM
    nb = k_ref.shape[1]
    nh = 2 * (qT_ref.shape[2] // LANES)
    n_units = nb * (nb - 1) // 2
    dim_row = lax.broadcasted_iota(jnp.int32, (LANES, blk), 0)
    dim_col = lax.broadcasted_iota(jnp.int32, (blk, LANES), 1)
    blk_id = lax.broadcasted_iota(jnp.int32, (nb, blk), 0)
    blk_idf = blk_id.astype(F32)
    key_row = lax.broadcasted_iota(jnp.int32, (blk, blk), 0)
    qry_col = lax.broadcasted_iota(jnp.int32, (blk, blk), 1)

    def lanes(h):
        return slice((h // 2) * LANES, (h // 2 + 1) * LANES)

    ones_rows = jnp.ones((SUM_ROWS, blk), BF16)

    def v_ext(n, h):
        return jnp.concatenate([vT_ref[0, n, h * hd:(h + 1) * hd, :], ones_rows], axis=0)

    def phase_a(j, carry):
        for h in range(nh):
            qT = qT_ref[0, j, lanes(h), :]
            qh = jnp.where((dim_row >= hd) == (h % 2 == 1), qT, jnp.zeros_like(qT))
            km = kmean_ref[0, :, lanes(h)]
            km_hi = km.astype(BF16)
            km_lo = (km - km_hi.astype(F32)).astype(BF16)
            gate = _dot(km_hi, qh) + _dot(km_lo, qh)
            gate = jnp.where(blk_id < j, gate, NEG)
            picked = blk_id < 0
            for _ in range(min(MOBA_TOPK, nb)):
                top = jnp.max(gate, axis=0, keepdims=True)
                first = jnp.min(jnp.where(gate == top, blk_idf, float(nb)), axis=0, keepdims=True)
                hit = blk_idf == first
                picked = picked | hit
                gate = jnp.where(hit, -jnp.inf, gate)
            selected = picked & (blk_id < j)
            bias_ref[h, j, 0:nb, :] = jnp.where(selected, 0.0, NEG)
            bias_ref[h, j, nb:nb + 1, :] = jnp.full((1, blk), NEG, F32)
            s_own = jnp.where(key_row <= qry_col, _dot(k_ref[0, j, :, lanes(h)], qh), NEG)
            m0 = jnp.max(s_own, axis=0, keepdims=True)
            p0 = jnp.exp2(s_own - m0)
            m_ref[h, j] = m0
            acc_ref[h, j, 0:hd, :] = _dot(vT_ref[0, j, h * hd:(h + 1) * hd, :], p0.astype(BF16))
            acc_ref[h, j, hd:hd + SUM_ROWS, :] = jnp.broadcast_to(jnp.sum(p0, axis=0, keepdims=True), (SUM_ROWS, blk))
        return carry

    lax.fori_loop(0, nb, phase_a, 0, unroll=4)

    for h in range(nh):
        s_ref[h] = jnp.zeros((blk, blk), F32)
        cm_ref[h] = jnp.zeros((1, blk), F32)
        p_ref[h] = jnp.zeros((blk, blk), BF16)
        a_ref[h] = jnp.ones((1, blk), F32)

    def phase_b(i, carry):
        j1, n1 = jtab_ref[i + 2], ntab_ref[i + 2]
        j2, n2 = jtab_ref[i + 1], ntab_ref[i + 1]
        j3, n3 = jtab_ref[i], ntab_ref[i]
        k_blk = jnp.minimum(n1, nb - 1)
        v_blk = jnp.minimum(n3, nb - 1)
        for h in range(nh):
            acc_ref[h, j3] = a_ref[h] * acc_ref[h, j3] + _dot(v_ext(v_blk, h), p_ref[h])
        for h in range(nh):
            bias = bias_ref[h, j2, pl.ds(n2, 1), :]
            m_old = m_ref[h, j2]
            m_new = jnp.maximum(m_old, cm_ref[h] + bias)
            alpha = jnp.exp2(m_old - m_new)
            p = jnp.exp2(s_ref[h] - (m_new - bias))
            m_ref[h, j2] = m_new
            a_ref[h] = alpha
            p_ref[h] = p.astype(BF16)
        for h in range(nh):
            kb = k_ref[0, k_blk, :, lanes(h)]
            kb = jnp.where((dim_col >= hd) == (h % 2 == 1), kb, jnp.zeros_like(kb))
            sT = _dot(kb, qT_ref[0, j1, lanes(h), :])
            s_ref[h] = sT
            cm_ref[h] = jnp.max(sT, axis=0, keepdims=True)
        return carry

    lax.fori_loop(0, n_units + 2, phase_b, 0, unroll=2)

    def phase_c(j, carry):
        row0 = pl.multiple_of(j * blk, blk)
        for pair in range(nh // 2):
            oT = jnp.concatenate([acc_ref[h, j, 0:hd, :] / acc_ref[h, j, hd:hd + 1, :]
                                  for h in (2 * pair, 2 * pair + 1)], axis=0)
            out_ref[0, pl.ds(row0, blk), pair * LANES:(pair + 1) * LANES] = oT.T.astype(BF16)
        return carry

    lax.fori_loop(0, nb, phase_c, 0)


def _moba_units(nb):
    pad = (0, nb)
    units = [pad, pad] + [(j, n) for j in range(nb) for n in range(j)] + [pad, pad]
    jtab = jnp.asarray([u[0] for u in units], jnp.int32)
    ntab = jnp.asarray([u[1] for u in units], jnp.int32)
    return jtab, ntab


def _moba(qT, k, vT, kmean):
    b, nb, aw, blk = qT.shape
    s = nb * blk
    w = MOBA_PAIRS_PER_STEP * LANES
    nh = 2 * MOBA_PAIRS_PER_STEP
    jtab, ntab = _moba_units(nb)
    grid_spec = pltpu.PrefetchScalarGridSpec(
        num_scalar_prefetch=2,
        grid=(b, aw // w),
        in_specs=[
            pl.BlockSpec((1, nb, w, blk), lambda bi, pi, *_: (bi, 0, pi, 0)),
            pl.BlockSpec((1, nb, blk, w), lambda bi, pi, *_: (bi, 0, 0, pi)),
            pl.BlockSpec((1, nb, w, blk), lambda bi, pi, *_: (bi, 0, pi, 0)),
            pl.BlockSpec((1, nb, w), lambda bi, pi, *_: (bi, 0, pi)),
        ],
        out_specs=pl.BlockSpec((1, s, w), lambda bi, pi, *_: (bi, 0, pi)),
        scratch_shapes=[pltpu.VMEM((nh, nb, nb + 8, blk), F32),
                        pltpu.VMEM((nh, nb, 1, blk), F32),
                        pltpu.VMEM((nh, nb, ATTN_HEAD_DIM + SUM_ROWS, blk), F32),
                        pltpu.VMEM((nh, blk, blk), F32),
                        pltpu.VMEM((nh, 1, blk), F32),
                        pltpu.VMEM((nh, blk, blk), BF16),
                        pltpu.VMEM((nh, 1, blk), F32)],
    )
    return pl.pallas_call(
        _moba_kernel,
        grid_spec=grid_spec,
        out_shape=jax.ShapeDtypeStruct((b, s, aw), BF16),
        compiler_params=pltpu.CompilerParams(
            dimension_semantics=("parallel", "parallel"), vmem_limit_bytes=VMEM_LIMIT),
        name="moba",
    )(jtab, ntab, qT, k, vT, kmean)


def _split3(x):
    hi = x.astype(BF16)
    r = x - hi.astype(F32)
    mid = r.astype(BF16)
    lo = (r - mid.astype(F32)).astype(BF16)
    return hi, mid, lo


def _mlstm_kernel(q_ref, kT_ref, v_ref, gif_ref, gifT_ref, out_ref, c_ref, m_ref):
    L = MLSTM_CHUNK
    nh, hd = N_MLSTM_HEADS, MLSTM_HEAD_DIM
    c = pl.program_id(1)

    @pl.when(c == 0)
    def _():
        c_ref[...] = jnp.zeros_like(c_ref)
        m_ref[...] = jnp.zeros_like(m_ref)

    t_idx = lax.broadcasted_iota(jnp.int32, (L, L), 0)
    s_idx = lax.broadcasted_iota(jnp.int32, (L, L), 1)
    causal = s_idx <= t_idx
    tril = jnp.where(causal, 1.0, 0.0).astype(BF16)
    triu = jnp.where(t_idx <= s_idx, 1.0, 0.0).astype(BF16)

    ones_col = jnp.where(lax.broadcasted_iota(jnp.int32, (L, hd), 1) == 0, 1.0, 0.0).astype(BF16)

    for r, r0 in [(r, r0) for r in range(q_ref.shape[0]) for r0 in range(0, q_ref.shape[1], L)]:
        rows = slice(r0, r0 + L)
        gi_col = gif_ref[r, rows, :]
        gi_row = gifT_ref[r, :, rows]
        b_col_all = sum(_dot(tril, part) for part in _split3(gi_col))
        b_row_all = sum(_dot(part, triu) for part in _split3(gi_row))

        for h in range(nh):
            cols = slice(h * hd, (h + 1) * hd)
            a_row = gi_row[h:h + 1, :]
            b_row = b_row_all[nh + h:nh + h + 1, :]
            b_col = b_col_all[:, nh + h:nh + h + 1]
            q = q_ref[r, rows, cols]
            if L >= SEQ_TILE:
                kT = jnp.concatenate([kT_ref[r, (r0 + i) // SEQ_TILE, cols, :] for i in range(0, L, SEQ_TILE)], axis=1)
            else:
                kT = kT_ref[r, r0 // SEQ_TILE, cols, r0 % SEQ_TILE:r0 % SEQ_TILE + L]
            v_ext = jnp.concatenate([v_ref[r, rows, cols], ones_col], axis=1)

            dmat = jnp.where(causal, b_col - b_row + a_row, NEG)
            mi = jnp.max(dmat, axis=1, keepdims=True)
            intra = _dot((jnp.exp(dmat - mi) * _dot(q, kT)).astype(BF16), v_ext)
            b_last = b_row[:, L - 1:L]
            dec = b_last - b_row + a_row
            md = jnp.max(dec, axis=1, keepdims=True)
            kv = _dot((kT.astype(F32) * jnp.exp(dec - md)).astype(BF16), v_ext)

            m_prev = m_ref[r * nh + h]
            c_old = c_ref[r * nh + h]
            inter = b_col + m_prev
            m_t = jnp.maximum(inter, mi)
            num_ext = jnp.exp(inter - m_t) * _dot(q, c_old.astype(BF16)) + jnp.exp(mi - m_t) * intra
            num = num_ext[:, :hd]
            den = num_ext[:, hd:hd + 1]
            h_out = num / jnp.maximum(jnp.abs(den), jnp.exp(-m_t))

            m_new = jnp.maximum(b_last + m_prev, md)
            c_ref[r * nh + h] = jnp.exp(b_last + m_prev - m_new) * c_old + jnp.exp(md - m_new) * kv
            m_ref[r * nh + h] = m_new

            out_ref[r, rows, cols] = h_out


def _mlstm(qm, kmT, vm, gif, gifT):
    b, s, mw = qm.shape
    L = MLSTM_BLOCK
    nc = s // L
    nh2 = 2 * N_MLSTM_HEADS
    rows = MLSTM_ROWS if b % MLSTM_ROWS == 0 else 1
    row = lambda cdim: pl.BlockSpec((rows, L, cdim), lambda bi, ci: (bi, ci, 0))
    return pl.pallas_call(
        _mlstm_kernel,
        grid=(b // rows, nc),
        in_specs=[
            row(mw),
            pl.BlockSpec((rows, L // SEQ_TILE, mw, SEQ_TILE), lambda bi, ci: (bi, ci, 0, 0)),
            row(mw), row(nh2),
            pl.BlockSpec((rows, nh2, L), lambda bi, ci: (bi, 0, ci)),
        ],
        out_specs=row(mw),
        out_shape=jax.ShapeDtypeStruct((b, s, mw), F32),
        scratch_shapes=[pltpu.VMEM((rows * N_MLSTM_HEADS, MLSTM_HEAD_DIM, 2 * MLSTM_HEAD_DIM), F32),
                        pltpu.VMEM((rows * N_MLSTM_HEADS, 1, 1), F32)],
        compiler_params=pltpu.CompilerParams(
            dimension_semantics=("parallel", "arbitrary"), vmem_limit_bytes=VMEM_LIMIT),
        name="mlstm",
    )(qm, kmT, vm, gif, gifT)


def _mix_kernel(x_ref, a_ref, hm_ref, om_ref, gm_ref, g_ref, wg_ref, wa_ref, wm_ref, wo_ref, out_ref):
    d = x_ref.shape[-1]
    hd = MLSTM_HEAD_DIM
    x = x_ref[...]
    h = _rms(x, g_ref[...]).astype(BF16)
    gate_a = _sigmoid(_dot(h, wg_ref[:, :d]))
    gate_m = _sigmoid(_dot(h, wg_ref[:, d:]))
    gm = gm_ref[...]
    y = jnp.concatenate([_rms(hm_ref[:, c:c + hd], gm[:, c:c + hd]) for c in range(0, hm_ref.shape[1], hd)], axis=1)
    y = (y * _sigmoid(om_ref[...])).astype(BF16)
    merged = gate_a * _dot(a_ref[...], wa_ref[...]) + gate_m * _dot(y, wm_ref[...])
    out_ref[...] = x + _dot(merged.astype(BF16), wo_ref[...])


def _mix(x2d, a2d, hm2d, om2d, gm, g, wg, wa, wm, wo):
    t, d = x2d.shape
    tm = MIX_TILE
    row = lambda cdim: pl.BlockSpec((tm, cdim), lambda i: (i, 0))
    return pl.pallas_call(
        _mix_kernel,
        grid=(t // tm,),
        in_specs=[row(d), row(a2d.shape[1]), row(hm2d.shape[1]), row(om2d.shape[1]), _const_spec(gm.shape),
                  _const_spec(g.shape),
                  _const_spec(wg.shape), _const_spec(wa.shape), _const_spec(wm.shape), _const_spec(wo.shape)],
        out_specs=row(d),
        out_shape=jax.ShapeDtypeStruct((t, d), F32),
        compiler_params=pltpu.CompilerParams(
            dimension_semantics=("parallel",), vmem_limit_bytes=VMEM_LIMIT),
        name="mix",
    )(x2d, a2d, hm2d, om2d, gm, g, wg, wa, wm, wo)


def _ffn_kernel(x_ref, xh_ref, g_ref, wup_ref, cg_ref, cu_ref, wd_ref, gf_ref, out_ref,
                hext_ref, ug_ref, uu_ref, act_ref, actall_ref):
    tm = FFN_TILE
    nslab = ug_ref.shape[0]
    nchunk = actall_ref.shape[1] // FFN_CHUNK
    j = pl.program_id(1)
    x = x_ref[0]
    g = g_ref[...]
    hext_ref[HALO:HALO + tm, :] = _rms(x, g).astype(BF16)
    hh = _rms(xh_ref[0], g)
    hext_ref[0:HALO, :] = jnp.where(j == 0, 0.0, hh).astype(BF16)

    def stage_a(c):
        gate_cols = pl.ds(pl.multiple_of(c * FFN_CHUNK, LANES), FFN_CHUNK)
        up_cols = pl.ds(pl.multiple_of(nchunk * FFN_CHUNK + c * FFN_CHUNK, LANES), FFN_CHUNK)
        ug = _dot(hext_ref[...], wup_ref[:, gate_cols])
        uu = _dot(hext_ref[...], wup_ref[:, up_cols])
        for sl in range(nslab):
            ug_ref[sl] = ug[:, sl * LANES:(sl + 1) * LANES]
            uu_ref[sl] = uu[:, sl * LANES:(sl + 1) * LANES]

    def stage_b(c):
        for sl in range(nslab):
            cg = cg_ref[c][:, sl * LANES:(sl + 1) * LANES]
            cu = cu_ref[c][:, sl * LANES:(sl + 1) * LANES]
            for r0 in range(0, tm, 8 * CONV_STRIDE):
                gates = _conv_block(ug_ref, sl, r0, cg)
                ups = _conv_block(uu_ref, sl, r0, cu)
                for k in range(CONV_STRIDE):
                    act_ref[sl, pl.ds(r0 + k, 8, stride=CONV_STRIDE), :] = gates[k] * _sigmoid(gates[k]) * ups[k]
        act = jnp.concatenate([act_ref[sl] for sl in range(nslab)], axis=1)
        actall_ref[:, pl.ds(pl.multiple_of(c * FFN_CHUNK, LANES), FFN_CHUNK)] = act.astype(BF16)

    stage_a(0)

    def body(c, carry):
        stage_b(c - 1)
        stage_a(c)
        return carry

    lax.fori_loop(1, nchunk, body, 0, unroll=2)
    stage_b(nchunk - 1)
    out_ref[0] = _rms(x + _dot(actall_ref[...], wd_ref[...]), gf_ref[...])


def _ffn(x1, g2, wup, cg, cu, wd, gf):
    b, s, d = x1.shape
    tm = FFN_TILE
    return pl.pallas_call(
        _ffn_kernel,
        grid=(b, s // tm),
        in_specs=[
            pl.BlockSpec((1, tm, d), lambda bi, ji: (bi, ji, 0)),
            pl.BlockSpec((1, HALO, d), lambda bi, ji: (bi, jnp.maximum(ji * (tm // HALO) - 1, 0), 0)),
            _const_spec(g2.shape), _const_spec(wup.shape),
            _const_spec(cg.shape), _const_spec(cu.shape), _const_spec(wd.shape), _const_spec(gf.shape),
        ],
        out_specs=pl.BlockSpec((1, tm, d), lambda bi, ji: (bi, ji, 0)),
        out_shape=jax.ShapeDtypeStruct((b, s, d), F32),
        scratch_shapes=[pltpu.VMEM((HALO + tm, d), BF16),
                        pltpu.VMEM((FFN_CHUNK // LANES, HALO + tm, LANES), F32),
                        pltpu.VMEM((FFN_CHUNK // LANES, HALO + tm, LANES), F32),
                        pltpu.VMEM((FFN_CHUNK // LANES, tm, LANES), F32),
                        pltpu.VMEM((tm, wd.shape[0]), BF16)],
        compiler_params=pltpu.CompilerParams(
            dimension_semantics=("parallel", "parallel"), vmem_limit_bytes=VMEM_LIMIT),
        name="ffn",
    )(x1, x1, g2, wup, cg, cu, wd, gf)


def _chunk_cols(w, chunk):
    kdim, n = w.shape
    return w.reshape(kdim, n // chunk, chunk).transpose(1, 0, 2)


def kernel(x, positions, norm_mix_g, w_in, conv_mlstm, i_bias, f_bias, mlstm_norm_g, w_branch_attn,
           w_branch_mlstm, w_out, norm_ffn_g, w_up, conv_ffn, w_down, norm_final_g):
    b, s, d = x.shape
    aw, mw, nh = ATTN_WIDTH, MLSTM_WIDTH, N_MLSTM_HEADS
    assert s % FFN_TILE == 0 and s % SEQ_TILE == 0 and s % MLSTM_BLOCK == 0 and (b * s) % MIX_TILE == 0 and d % LANES == 0
    assert w_in.shape[0] == 1, "the ffn kernel fuses the output norm, so exactly one layer is supported"
    n_qkv = 3 * aw + 4 * mw
    d_ff = w_down.shape[1]
    assert d_ff % FFN_CHUNK == 0
    half = ATTN_HEAD_DIM // 2
    inv_freq = ROPE_THETA ** (-(jnp.arange(half, dtype=F32) / half))
    cos, sin = _rope_tables(positions, inv_freq)

    for layer in range(w_in.shape[0]):
        wl = w_in[layer]
        w1 = jnp.concatenate(
            [wl[:, :n_qkv].astype(BF16), wl[:, n_qkv:n_qkv + 2 * nh].astype(BF16),
             jnp.zeros((d, LANES - 2 * nh), BF16)], axis=1)
        wg = wl[:, n_qkv + 2 * nh:].astype(BF16)
        bias = jnp.concatenate(
            [i_bias[layer], f_bias[layer], jnp.zeros((LANES - 2 * nh,), F32)]).reshape(1, LANES)

        qT, k, kmean, vT, qm, kmT, vm, om, gif, gifT = _in_proj(
            x, cos, sin, norm_mix_g[layer].reshape(1, d), w1, conv_mlstm[layer], bias)
        attn = _moba(qT, k, vT, kmean.reshape(b, s // MOBA_BLOCK, aw))
        h_m = _mlstm(qm, kmT, vm, gif, gifT)
        x1 = _mix(x.reshape(b * s, d), attn.reshape(b * s, aw), h_m.reshape(b * s, mw), om.reshape(b * s, mw),
                  mlstm_norm_g[layer].reshape(1, mw), norm_mix_g[layer].reshape(1, d), wg, w_branch_attn[layer].astype(BF16),
                  w_branch_mlstm[layer].astype(BF16), w_out[layer].astype(BF16)).reshape(b, s, d)

        wu = w_up[layer].astype(BF16)
        cf = conv_ffn[layer]
        x = _ffn(x1, norm_ffn_g[layer].reshape(1, d), wu,
                 _chunk_cols(cf[:, :d_ff], FFN_CHUNK), _chunk_cols(cf[:, d_ff:], FFN_CHUNK),
                 w_down[layer].astype(BF16), norm_final_g.reshape(1, d))
    return x
```
